```python
import jax, jax.numpy as jnp
from jax import lax
import numpy as np

D_MODEL = 4096
BATCH = 1
SEQ = 8192
DEPTH = 1
DEC_BATCH = 32
DEC_SEQ = 4
PAST_LEN = 8192
PAGE_SIZE = 128

HEAD_DIM = 128
FOX_HEADS = 16
DSA_HEADS = 16
DSA_KV_HEADS = 4
IDX_HEADS = 32
IDX_DIM = 64
TOPK_MAX = 256
ROPE_THETA = 10000.0
PEER_HEADS = 8
PEER_NKEYS = 128
PEER_EXPERTS = PEER_NKEYS * PEER_NKEYS
PEER_QDIM = 256
PEER_HALF = PEER_QDIM // 2
PEER_TOPK = 16
PLE_DIM = 256
Q_BLOCK = 128
EPS = 1e-6
FORGET_BIAS_INIT = 2.0

FOX_W = FOX_HEADS * HEAD_DIM
DSA_QW = DSA_HEADS * HEAD_DIM
DSA_KW = DSA_KV_HEADS * HEAD_DIM
IDX_QW = IDX_HEADS * IDX_DIM
MIX_W = FOX_W + DSA_QW
COLUMN_SPLITS = (FOX_W, FOX_W, FOX_W, FOX_HEADS, DSA_QW, DSA_KW, DSA_KW, IDX_QW, IDX_DIM, IDX_HEADS)
IN_W = FOX_W * 3 + FOX_HEADS + DSA_QW + DSA_KW * 2 + IDX_QW + IDX_DIM + IDX_HEADS

kernel_name = "hymba_fox_dsa_peer_decode_step"


def rms_norm(x, g):
    xf = x.astype(jnp.float32)
    y = xf * lax.rsqrt(jnp.mean(xf * xf, axis=-1, keepdims=True) + EPS)
    return (y * g.astype(jnp.float32)).astype(x.dtype)


def rope(x, pos):
    d = x.shape[-1]
    half = d // 2
    inv = ROPE_THETA ** (-jnp.arange(half, dtype=jnp.float32) * 2.0 / d)
    ang = pos.astype(jnp.float32)[..., None] * inv
    cos = jnp.cos(ang)[:, :, None, :]
    sin = jnp.sin(ang)[:, :, None, :]
    x1 = x[..., :half].astype(jnp.float32)
    x2 = x[..., half:].astype(jnp.float32)
    return jnp.concatenate([x1 * cos - x2 * sin, x2 * cos + x1 * sin], axis=-1).astype(x.dtype)


def take_rows(rows, idx):
    return jax.vmap(lambda r, i: r[i])(rows, idx)


def blocked_over_queries(fn, *arrs):
    B, T = arrs[0].shape[:2]
    nb = T // Q_BLOCK
    split = lambda a: jnp.moveaxis(a.reshape((B, nb, Q_BLOCK) + a.shape[2:]), 1, 0)
    out = lax.map(lambda blk: fn(*blk), tuple(split(a) for a in arrs))
    return jnp.moveaxis(out, 0, 1).reshape((B, T) + out.shape[3:])


def blocked_over_tokens(fn, x):
    n = x.shape[0]
    nb = -(-n // Q_BLOCK)
    xp = jnp.pad(x, ((0, nb * Q_BLOCK - n), (0, 0)))
    out = lax.map(fn, xp.reshape(nb, Q_BLOCK, x.shape[1]))
    return out.reshape(nb * Q_BLOCK, out.shape[-1])[:n]


def project(xn, pos, w_in, b_f, g_q_fox, g_k_fox, g_q_dsa, g_k_dsa):
    B, T, _ = xn.shape
    z = xn @ w_in
    points = np.cumsum(np.array(COLUMN_SPLITS))[:-1].tolist()
    qf, kf, vf, fl, qd, kd, vd, qi, ki, wi = jnp.split(z, points, axis=-1)
    q_fox = rms_norm(qf.reshape(B, T, FOX_HEADS, HEAD_DIM), g_q_fox)
    k_fox = rms_norm(kf.reshape(B, T, FOX_HEADS, HEAD_DIM), g_k_fox)
    v_fox = vf.reshape(B, T, FOX_HEADS, HEAD_DIM)
    logf = jax.nn.log_sigmoid(fl.astype(jnp.float32) + b_f.astype(jnp.float32))
    q_dsa = rope(rms_norm(qd.reshape(B, T, DSA_HEADS, HEAD_DIM), g_q_dsa), pos)
    k_dsa = rope(rms_norm(kd.reshape(B, T, DSA_KV_HEADS, HEAD_DIM), g_k_dsa), pos)
    v_dsa = vd.reshape(B, T, DSA_KV_HEADS, HEAD_DIM)
    q_idx = rope(qi.reshape(B, T, IDX_HEADS, IDX_DIM), pos)
    k_idx = rope(ki.reshape(B, T, 1, IDX_DIM), pos)[:, :, 0]
    w_idx = wi.astype(jnp.float32) * (IDX_HEADS ** -0.5 * IDX_DIM ** -0.5)
    return (q_fox, k_fox, v_fox, logf, q_dsa, k_dsa, v_dsa, q_idx, k_idx, w_idx)


def fox_attend(q, c_q, pos_q, k, v, c_k):
    B, Q = q.shape[:2]
    L = k.shape[1]
    s = jnp.einsum('bqhd,bkhd->bhqk', q, k, preferred_element_type=jnp.float32) * HEAD_DIM ** -0.5
    s = s + (jnp.moveaxis(c_q, 2, 1)[..., :, None] - jnp.moveaxis(c_k, 2, 1)[..., None, :])
    causal = jnp.arange(L)[None, None, None, :] <= pos_q[:, None, :, None]
    p = jax.nn.softmax(jnp.where(causal, s, -jnp.inf), axis=-1)
    o = jnp.einsum('bhqk,bkhd->bqhd', p.astype(v.dtype), v)
    return o.reshape(B, Q, FOX_W)


def index_select(q_idx, w_idx, pos_q, k_idx, n_sel):
    L = k_idx.shape[1]
    dots = jnp.einsum('bqhd,bkd->bqhk', q_idx, k_idx, preferred_element_type=jnp.float32)
    score = jnp.einsum('bqh,bqhk->bqk', w_idx, jax.nn.relu(dots))
    causal = jnp.arange(L)[None, None, :] <= pos_q[:, :, None]
    _, idx = lax.top_k(jnp.where(causal, score, -jnp.inf), n_sel)
    valid = idx <= pos_q[:, :, None]
    return idx, valid


def dsa_attend(q, k_sel, v_sel, valid):
    B, Q = q.shape[:2]
    qg = q.reshape(B, Q, DSA_KV_HEADS, DSA_HEADS // DSA_KV_HEADS, HEAD_DIM)
    s = jnp.einsum('bqhgd,bqshd->bqhgs', qg, k_sel, preferred_element_type=jnp.float32) * HEAD_DIM ** -0.5
    p = jax.nn.softmax(jnp.where(valid[:, :, None, None, :], s, -jnp.inf), axis=-1)
    o = jnp.einsum('bqhgs,bqshd->bqhgd', p.astype(v_sel.dtype), v_sel)
    return o.reshape(B, Q, DSA_QW)


def paged_past(pool, page_table):
    g = pool[page_table]
    B, n_pages = page_table.shape
    return g.reshape((B, n_pages * PAGE_SIZE) + pool.shape[2:])


def gather_paged_rows(pool, page_table, new_rows, idx):
    B, Q, S = idx.shape
    past_idx = jnp.minimum(idx, PAST_LEN - 1)
    phys = jnp.take_along_axis(page_table, (past_idx // PAGE_SIZE).reshape(B, Q * S), axis=1).reshape(B, Q, S)
    from_pool = pool[phys, past_idx % PAGE_SIZE]
    from_new = take_rows(new_rows, jnp.clip(idx - PAST_LEN, 0, new_rows.shape[1] - 1))
    return jnp.where((idx >= PAST_LEN)[..., None, None], from_new.astype(from_pool.dtype), from_pool)


def prompt_mixer(q_fox, k_fox, v_fox, logf, q_dsa, k_dsa, v_dsa, q_idx, k_idx, w_idx, pos):
    T = q_fox.shape[1]
    c = jnp.cumsum(logf, axis=1)
    n_sel = min(TOPK_MAX, T // 4)

    def block(qf, cq, qd, qi, wi, pq):
        o_fox = fox_attend(qf, cq, pq, k_fox, v_fox, c)
        idx, valid = index_select(qi, wi, pq, k_idx, n_sel)
        o_dsa = dsa_attend(qd, take_rows(k_dsa, idx), take_rows(v_dsa, idx), valid)
        return jnp.concatenate([o_fox, o_dsa.astype(o_fox.dtype)], axis=-1)

    return blocked_over_queries(block, q_fox, c, q_dsa, q_idx, w_idx, pos)


def sample_mixer(q_fox, k_fox, v_fox, logf, q_dsa, k_dsa, v_dsa, q_idx, k_idx, w_idx, pos,
                 cache_fox_k, cache_fox_v, cache_fox_logf, cache_dsa_k, cache_dsa_v, cache_idx_k, page_table):
    k_all = jnp.concatenate([paged_past(cache_fox_k, page_table).astype(k_fox.dtype), k_fox], axis=1)
    v_all = jnp.concatenate([paged_past(cache_fox_v, page_table).astype(v_fox.dtype), v_fox], axis=1)
    lf_all = jnp.concatenate([paged_past(cache_fox_logf, page_table).astype(jnp.float32), logf], axis=1)
    c = jnp.cumsum(lf_all, axis=1)
    o_fox = fox_attend(q_fox, c[:, PAST_LEN:], pos, k_all, v_all, c)
    ki_all = jnp.concatenate([paged_past(cache_idx_k, page_table).astype(k_idx.dtype), k_idx], axis=1)
    n_sel = min(TOPK_MAX, ki_all.shape[1] // 4)
    idx, valid = index_select(q_idx, w_idx, pos, ki_all, n_sel)
    k_sel = gather_paged_rows(cache_dsa_k, page_table, k_dsa, idx)
    v_sel = gather_paged_rows(cache_dsa_v, page_table, v_dsa, idx)
    o_dsa = dsa_attend(q_dsa, k_sel, v_sel, valid)
    return jnp.concatenate([o_fox, o_dsa.astype(o_fox.dtype)], axis=-1)


def peer(xn, w_peer_q, peer_keys, peer_u, peer_v):
    N = xn.shape[0]
    q = (xn @ w_peer_q).reshape(N, PEER_HEADS, 2, PEER_HALF)
    s = jnp.einsum('nhcd,hckd->nhck', q, peer_keys, preferred_element_type=jnp.float32)
    sv, si = lax.top_k(s, PEER_TOPK)
    cand = sv[:, :, 0, :, None] + sv[:, :, 1, None, :]
    cand_idx = si[:, :, 0, :, None] * PEER_NKEYS + si[:, :, 1, None, :]
    cv, ci = lax.top_k(cand.reshape(N, PEER_HEADS, PEER_TOPK * PEER_TOPK), PEER_TOPK)
    expert = jnp.take_along_axis(cand_idx.reshape(N, PEER_HEADS, PEER_TOPK * PEER_TOPK), ci, axis=-1)
    g = jax.nn.softmax(cv, axis=-1)
    u = peer_u[expert]
    v = peer_v[expert]
    a = jax.nn.gelu(jnp.einsum('nd,nhkd->nhk', xn, u, preferred_element_type=jnp.float32), approximate=False)
    return jnp.einsum('nhk,nhkd->nd', (g * a).astype(v.dtype), v)


def channel_and_ple(x, mix, p_l, w_o, g_norm2, w_peer_q, peer_keys, peer_u, peer_v,
                    g_norm3, w_ple_gate, w_ple_proj, g_ple):
    h = x + mix @ w_o
    B, T, D = h.shape
    hn = rms_norm(h, g_norm2).reshape(B * T, D)
    h = h + blocked_over_tokens(lambda t: peer(t, w_peer_q, peer_keys, peer_u, peer_v), hn).reshape(B, T, D)
    gate = jax.nn.sigmoid(rms_norm(h, g_norm3) @ w_ple_gate)
    return h + gate * rms_norm(p_l @ w_ple_proj, g_ple)


def setup_inputs(seed: int = 0) -> dict:
    key = jax.random.key(seed)
    k = jax.random.split(key, 32)
    f32 = jnp.float32
    nrm = lambda kk, shape, scale: scale * jax.random.normal(kk, shape, f32)
    gain = lambda kk, shape: 1.0 + nrm(kk, shape, 0.02)
    n_pages = PAST_LEN // PAGE_SIZE
    n_used = DEC_BATCH * n_pages
    n_phys = n_used + max(1, n_used // 4)
    page_table = jax.random.permutation(k[9], n_phys)[:n_used].reshape(DEC_BATCH, n_pages).astype(jnp.int32)
    return {
        "x_prompt": nrm(k[0], (BATCH, SEQ, D_MODEL), 1.0),
        "x_sample": nrm(k[1], (DEC_BATCH, DEC_SEQ, D_MODEL), 1.0),
        "cache_fox_k": nrm(k[2], (DEPTH, n_phys, PAGE_SIZE, FOX_HEADS, HEAD_DIM), 1.0),
        "cache_fox_v": nrm(k[3], (DEPTH, n_phys, PAGE_SIZE, FOX_HEADS, HEAD_DIM), 1.0),
        "cache_fox_logf": jax.nn.log_sigmoid(FORGET_BIAS_INIT + nrm(k[4], (DEPTH, n_phys, PAGE_SIZE, FOX_HEADS), 1.0)),
        "cache_dsa_k": nrm(k[5], (DEPTH, n_phys, PAGE_SIZE, DSA_KV_HEADS, HEAD_DIM), 1.0),
        "cache_dsa_v": nrm(k[6], (DEPTH, n_phys, PAGE_SIZE, DSA_KV_HEADS, HEAD_DIM), 1.0),
        "cache_idx_k": nrm(k[7], (DEPTH, n_phys, PAGE_SIZE, IDX_DIM), 1.0),
        "page_table": page_table,
        "p_prompt": nrm(k[10], (DEPTH, BATCH, SEQ, PLE_DIM), 1.0),
        "p_sample": nrm(k[11], (DEPTH, DEC_BATCH, DEC_SEQ, PLE_DIM), 1.0),
        "g_norm1": gain(k[12], (DEPTH, D_MODEL)),
        "w_in": nrm(k[13], (DEPTH, D_MODEL, IN_W), D_MODEL ** -0.5),
        "b_f": FORGET_BIAS_INIT + nrm(k[14], (DEPTH, FOX_HEADS), 0.1),
        "g_q_fox": gain(k[15], (DEPTH, HEAD_DIM)),
        "g_k_fox": gain(k[16], (DEPTH, HEAD_DIM)),
        "g_q_dsa": gain(k[17], (DEPTH, HEAD_DIM)),
        "g_k_dsa": gain(k[18], (DEPTH, HEAD_DIM)),
        "w_o": nrm(k[19], (DEPTH, MIX_W, D_MODEL), MIX_W ** -0.5),
        "g_norm2": gain(k[20], (DEPTH, D_MODEL)),
        "w_peer_q": nrm(k[21], (DEPTH, D_MODEL, PEER_HEADS * PEER_QDIM), D_MODEL ** -0.5),
        "peer_keys": nrm(k[22], (DEPTH, PEER_HEADS, 2, PEER_NKEYS, PEER_HALF), PEER_HALF ** -0.5),
        "peer_u": nrm(k[23], (DEPTH, PEER_EXPERTS, D_MODEL), D_MODEL ** -0.5),
        "peer_v": nrm(k[24], (DEPTH, PEER_EXPERTS, D_MODEL), PEER_HEADS ** -0.5),
        "g_norm3": gain(k[25], (DEPTH, D_MODEL)),
        "w_ple_gate": nrm(k[26], (DEPTH, D_MODEL, D_MODEL), D_MODEL ** -0.5),
        "w_ple_proj": nrm(k[27], (DEPTH, PLE_DIM, D_MODEL), PLE_DIM ** -0.5),
        "g_ple": gain(k[28], (DEPTH, D_MODEL)),
    }


def reference(x_prompt, x_sample, cache_fox_k, cache_fox_v, cache_fox_logf, cache_dsa_k, cache_dsa_v,
              cache_idx_k, page_table, p_prompt, p_sample, g_norm1, w_in, b_f, g_q_fox, g_k_fox,
              g_q_dsa, g_k_dsa, w_o, g_norm2, w_peer_q, peer_keys, peer_u, peer_v, g_norm3,
              w_ple_gate, w_ple_proj, g_ple):
    Bp, Tp = x_prompt.shape[:2]
    Bs, Ts = x_sample.shape[:2]
    pos_p = jnp.broadcast_to(jnp.arange(Tp, dtype=jnp.int32), (Bp, Tp))
    pos_s = jnp.broadcast_to(PAST_LEN + jnp.arange(Ts, dtype=jnp.int32), (Bs, Ts))
    h_p, h_s = x_prompt, x_sample
    fk_p, fv_p, fl_p, dk_p, dv_p, ik_p = [], [], [], [], [], []
    fk_s, fv_s, fl_s, dk_s, dv_s, ik_s = [], [], [], [], [], []
    for l in range(DEPTH):
        proj_w = (w_in[l], b_f[l], g_q_fox[l], g_k_fox[l], g_q_dsa[l], g_k_dsa[l])
        chan_w = (w_o[l], g_norm2[l], w_peer_q[l], peer_keys[l], peer_u[l], peer_v[l],
                  g_norm3[l], w_ple_gate[l], w_ple_proj[l], g_ple[l])
        pr = project(rms_norm(h_p, g_norm1[l]), pos_p, *proj_w)
        mix_p = prompt_mixer(*pr, pos_p)
        h_p = channel_and_ple(h_p, mix_p, p_prompt[l], *chan_w)
        fk_p.append(pr[1]); fv_p.append(pr[2]); fl_p.append(pr[3])
        dk_p.append(pr[5]); dv_p.append(pr[6]); ik_p.append(pr[8])
        sr = project(rms_norm(h_s, g_norm1[l]), pos_s, *proj_w)
        mix_s = sample_mixer(*sr, pos_s, cache_fox_k[l], cache_fox_v[l], cache_fox_logf[l],
                             cache_dsa_k[l], cache_dsa_v[l], cache_idx_k[l], page_table)
        h_s = channel_and_ple(h_s, mix_s, p_sample[l], *chan_w)
        fk_s.append(sr[1]); fv_s.append(sr[2]); fl_s.append(sr[3])
        dk_s.append(sr[5]); dv_s.append(sr[6]); ik_s.append(sr[8])
    return (h_p, h_s,
            jnp.stack(fk_p), jnp.stack(fv_p), jnp.stack(fl_p), jnp.stack(dk_p), jnp.stack(dv_p), jnp.stack(ik_p),
            jnp.stack(fk_s), jnp.stack(fv_s), jnp.stack(fl_s), jnp.stack(dk_s), jnp.stack(dv_s), jnp.stack(ik_s))
```

```python
import functools
import math

import numpy as np
import jax
import jax.numpy as jnp
from jax import lax
from jax.experimental import pallas as pl
from jax.experimental.pallas import tpu as pltpu

F32 = jnp.float32
BF16 = jnp.bfloat16
I32 = jnp.int32

HEAD_DIM = 128
FOX_HEADS = 16
DSA_HEADS = 16
DSA_KV_HEADS = 4
DSA_GROUP = DSA_HEADS // DSA_KV_HEADS
IDX_HEADS = 32
IDX_DIM = 64
TOPK_MAX = 256
ROPE_THETA = 10000.0
PEER_TOPK = 16
EPS = 1e-6

FOX_W = FOX_HEADS * HEAD_DIM
DSA_QW = DSA_HEADS * HEAD_DIM
DSA_KW = DSA_KV_HEADS * HEAD_DIM
IDX_QW = IDX_HEADS * IDX_DIM

LANES = 128
NEG = -1e30
INT_MIN = -(2 ** 31)
VMEM_LIMIT = 56 * 1024 * 1024
ATT_SCALE = HEAD_DIM ** -0.5


def _cp(sem):
    return pltpu.CompilerParams(dimension_semantics=sem, vmem_limit_bytes=VMEM_LIMIT)


def _tile(n, pref):
    t = min(n, pref)
    while n % t:
        t //= 2
    return t


def _nt_dot(a, b):
    return lax.dot_general(a, b, (((1,), (1,)), ((), ())), preferred_element_type=F32)


def _mm_body(*refs, norm, add2, epi, emit_xn, prologue):
    it = iter(refs)
    x_ref = next(it)
    x2_ref = next(it) if add2 else None
    g_ref = next(it) if norm else None
    w_ref = next(it)
    res_ref = next(it) if epi in ("res", "gate") else None
    res2_ref = next(it) if (epi == "gate" and add2) else None
    aux_ref = next(it) if epi == "gate" else None
    o_ref = next(it)
    xo_ref = next(it) if emit_xn else None
    xn_ref = next(it) if prologue else None

    if prologue:
        @pl.when(pl.program_id(1) == 0)
        def _():
            x = x_ref[...].astype(F32)
            if add2:
                x = x + x2_ref[...]
            if norm:
                ms = jnp.mean(x * x, axis=-1, keepdims=True)
                x = x * lax.rsqrt(ms + EPS) * g_ref[...]
            xb = x.astype(BF16)
            xn_ref[...] = xb
            if emit_xn:
                xo_ref[...] = xb
        lhs = xn_ref[...]
    else:
        lhs = x_ref[...]
    acc = jnp.dot(lhs, w_ref[...], preferred_element_type=F32)
    if epi == "res":
        acc = res_ref[...] + acc
    elif epi == "gate":
        r = res_ref[...]
        if add2:
            r = r + res2_ref[...]
        acc = r + jax.nn.sigmoid(acc) * aux_ref[...]
    o_ref[...] = acc


def _mm(x, w, *, g=None, x2=None, res=None, res2=None, aux=None, epi="none", emit_xn=False,
        tm=512, tn=512):
    M, K = x.shape
    N = w.shape[1]
    tm = _tile(M, tm)
    tn = _tile(N, tn)
    norm = g is not None
    add2 = x2 is not None
    prologue = norm or add2 or x.dtype != BF16
    row = pl.BlockSpec((tm, K), lambda i, j: (i, 0))
    blk = pl.BlockSpec((tm, tn), lambda i, j: (i, j))
    ins, specs = [x], [row]
    if add2:
        ins.append(x2); specs.append(row)
    if norm:
        ins.append(g.reshape(1, K)); specs.append(pl.BlockSpec((1, K), lambda i, j: (0, 0)))
    ins.append(w); specs.append(pl.BlockSpec((K, tn), lambda i, j: (0, j)))
    if epi in ("res", "gate"):
        ins.append(res); specs.append(blk)
    if epi == "gate" and add2:
        ins.append(res2); specs.append(blk)
    if epi == "gate":
        ins.append(aux); specs.append(blk)
    out_shape = [jax.ShapeDtypeStruct((M, N), F32)]
    out_specs = [blk]
    if emit_xn:
        out_shape.append(jax.ShapeDtypeStruct((M, K), BF16))
        out_specs.append(row)
    scratch = [pltpu.VMEM((tm, K), BF16)] if prologue else []
    outs = pl.pallas_call(
        functools.partial(_mm_body, norm=norm, add2=add2, epi=epi, emit_xn=emit_xn, prologue=prologue),
        grid=(M // tm, N // tn),
        in_specs=specs, out_specs=out_specs, out_shape=out_shape, scratch_shapes=scratch,
        compiler_params=_cp(("parallel", "arbitrary")),
    )(*ins)
    return outs if emit_xn else outs[0]


def _rownorm_mm_body(x_ref, w_ref, g_ref, o_ref):
    y = jnp.dot(x_ref[...].astype(BF16), w_ref[...], preferred_element_type=F32)
    ms = jnp.mean(y * y, axis=-1, keepdims=True)
    o_ref[...] = y * lax.rsqrt(ms + EPS) * g_ref[...]


def _rownorm_mm(x, w, g, tm=256):
    M, K = x.shape
    N = w.shape[1]
    tm = _tile(M, tm)
    return pl.pallas_call(
        _rownorm_mm_body,
        grid=(M // tm,),
        in_specs=[pl.BlockSpec((tm, K), lambda i: (i, 0)),
                  pl.BlockSpec((K, N), lambda i: (0, 0)),
                  pl.BlockSpec((1, N), lambda i: (0, 0))],
        out_specs=pl.BlockSpec((tm, N), lambda i: (i, 0)),
        out_shape=jax.ShapeDtypeStruct((M, N), F32),
        compiler_params=_cp(("parallel",)),
    )(x, w, g.reshape(1, N))


def _head_norm(x, g):
    ms = jnp.mean(x * x, axis=-1, keepdims=True)
    return x * lax.rsqrt(ms + EPS) * g


def _rope_full(x, cos, sin):
    return x * cos + pltpu.roll(x, HEAD_DIM // 2, 1) * sin


def _rope_pair(x, cos, sin, lane):
    half = IDX_DIM // 2
    rot = jnp.where((lane % IDX_DIM) < half, pltpu.roll(x, LANES - half, 1), pltpu.roll(x, half, 1))
    return x * cos + rot * sin


def _post_body(zqf, zkf, zvf, zqd, zqi, zkv, zs, cd, sd, ci, si, bf, gqf, gkf, gqd, gkd,
               qf_o, kf_o, kfb_o, vf_o, vfb_o, lf_o, qd_o, kd_o, kdb_o, vd_o, vdb_o,
               qi_o, ki_o, kia_o, kib_o, wi_o):
    tm = zqf.shape[0]
    cos_d, sin_d = cd[...], sd[...]
    cos_i, sin_i = ci[...], si[...]
    lane = lax.broadcasted_iota(I32, (tm, LANES), 1)
    for h in range(FOX_HEADS):
        sl = slice(h * HEAD_DIM, (h + 1) * HEAD_DIM)
        qf_o[:, sl] = _head_norm(zqf[:, sl], gqf[...]).astype(BF16)
        k = _head_norm(zkf[:, sl], gkf[...])
        kf_o[:, sl] = k
        kfb_o[:, sl] = k.astype(BF16)
        v = zvf[:, sl]
        vf_o[:, sl] = v
        vfb_o[:, sl] = v.astype(BF16)
    for h in range(DSA_HEADS):
        sl = slice(h * HEAD_DIM, (h + 1) * HEAD_DIM)
        qd_o[:, sl] = _rope_full(_head_norm(zqd[:, sl], gqd[...]), cos_d, sin_d).astype(BF16)
    for h in range(DSA_KV_HEADS):
        sl = slice(h * HEAD_DIM, (h + 1) * HEAD_DIM)
        k = _rope_full(_head_norm(zkv[:, sl], gkd[...]), cos_d, sin_d)
        kd_o[:, sl] = k
        kdb_o[:, sl] = k.astype(BF16)
        v = zkv[:, DSA_KW + h * HEAD_DIM: DSA_KW + (h + 1) * HEAD_DIM]
        vd_o[:, sl] = v
        vdb_o[:, sl] = v.astype(BF16)
    for j in range(IDX_QW // LANES):
        sl = slice(j * LANES, (j + 1) * LANES)
        qi_o[:, sl] = _rope_pair(zqi[:, sl], cos_i, sin_i, lane).astype(BF16)
    x = zs[...]
    y = _rope_pair(x, cos_i, sin_i, lane)
    ki_o[...] = y[:, :IDX_DIM]
    ka = jnp.where(lane < IDX_DIM, y, 0.0)
    kia_o[...] = ka.astype(BF16)
    kib_o[...] = pltpu.roll(ka, IDX_DIM, 1).astype(BF16)
    wi_o[...] = x[:, IDX_DIM:IDX_DIM + IDX_HEADS] * (IDX_HEADS ** -0.5 * IDX_DIM ** -0.5)
    fl = x[:, IDX_DIM + IDX_HEADS:IDX_DIM + IDX_HEADS + FOX_HEADS] + bf[...]
    lf_o[...] = jnp.minimum(fl, 0.0) - jnp.log1p(jnp.exp(-jnp.abs(fl)))


def _post(z, pos, b_f, g_q_fox, g_k_fox, g_q_dsa, g_k_dsa, tm=128):
    M = z.shape[0]
    tm = _tile(M, tm)
    posf = pos.astype(F32)[:, None]
    hd = HEAD_DIM // 2
    ang = posf * (ROPE_THETA ** (-jnp.arange(hd, dtype=F32) * 2.0 / HEAD_DIM))
    cd = jnp.concatenate([jnp.cos(ang), jnp.cos(ang)], axis=-1)
    sd = jnp.concatenate([-jnp.sin(ang), jnp.sin(ang)], axis=-1)
    hi = IDX_DIM // 2
    angi = posf * (ROPE_THETA ** (-jnp.arange(hi, dtype=F32) * 2.0 / IDX_DIM))
    ci = jnp.tile(jnp.cos(angi), (1, 4))
    si = jnp.tile(jnp.concatenate([-jnp.sin(angi), jnp.sin(angi)], axis=-1), (1, 2))

    def zspec(w, c):
        return pl.BlockSpec((tm, w), lambda i: (i, c))

    def rspec(w):
        return pl.BlockSpec((tm, w), lambda i: (i, 0))

    def cspec(w):
        return pl.BlockSpec((1, w), lambda i: (0, 0))

    small_col = (5 * FOX_W + 2 * DSA_KW) // LANES
    in_specs = [zspec(FOX_W, 0), zspec(FOX_W, 1), zspec(FOX_W, 2), zspec(DSA_QW, 3), zspec(IDX_QW, 4),
                zspec(2 * DSA_KW, 5 * FOX_W // (2 * DSA_KW)), zspec(LANES, small_col),
                rspec(LANES), rspec(LANES), rspec(LANES), rspec(LANES),
                cspec(FOX_HEADS), cspec(HEAD_DIM), cspec(HEAD_DIM), cspec(HEAD_DIM), cspec(HEAD_DIM)]
    outs = [(FOX_W, BF16), (FOX_W, F32), (FOX_W, BF16), (FOX_W, F32), (FOX_W, BF16), (FOX_HEADS, F32),
            (DSA_QW, BF16), (DSA_KW, F32), (DSA_KW, BF16), (DSA_KW, F32), (DSA_KW, BF16),
            (IDX_QW, BF16), (IDX_DIM, F32), (LANES, BF16), (LANES, BF16), (IDX_HEADS, F32)]
    res = pl.pallas_call(
        _post_body,
        grid=(M // tm,),
        in_specs=in_specs,
        out_specs=[rspec(w) for w, _ in outs],
        out_shape=[jax.ShapeDtypeStruct((M, w), d) for w, d in outs],
        compiler_params=_cp(("parallel",)),
    )(z, z, z, z, z, z, z, cd, sd, ci, si, b_f.reshape(1, -1), g_q_fox.reshape(1, -1),
      g_k_fox.reshape(1, -1), g_q_dsa.reshape(1, -1), g_k_dsa.reshape(1, -1))
    names = ("qf", "kf", "kfb", "vf", "vfb", "lf", "qd", "kd", "kdb", "vd", "vdb", "qi", "ki", "kia",
             "kib", "wi")
    return dict(zip(names, res))


def _split3(x):
    hi = x.astype(BF16)
    r1 = x - hi.astype(F32)
    mid = r1.astype(BF16)
    lo = (r1 - mid.astype(F32)).astype(BF16)
    return hi, mid, lo


def _tri_cumsum(x):
    n = x.shape[0]
    r = lax.broadcasted_iota(I32, (n, n), 0)
    c = lax.broadcasted_iota(I32, (n, n), 1)
    tri = jnp.where(c <= r, 1.0, 0.0).astype(BF16)
    hi, mid, lo = _split3(x)
    dot = lambda a: jnp.dot(tri, a, preferred_element_type=F32)
    return dot(hi) + (dot(mid) + dot(lo))


def _cumsum_body(pt_ref, lf_ref, new_ref, c_ref, cn_ref, carry_ref):
    p = pl.program_id(1)

    @pl.when(p == 0)
    def _():
        carry_ref[...] = jnp.zeros_like(carry_ref)

    c = _tri_cumsum(lf_ref[...]) + carry_ref[...]
    c_ref[...] = c
    carry_ref[...] = c[-1:, :]

    @pl.when(p == pl.num_programs(1) - 1)
    def _():
        cn_ref[...] = _tri_cumsum(new_ref[...]) + c[-1:, :]


def _paged_cumsum(pool, page_table, new_rows):
    B, n_pages = page_table.shape
    R, H = pool.shape[1:]
    nr = new_rows.shape[1]
    grid_spec = pltpu.PrefetchScalarGridSpec(
        num_scalar_prefetch=1, grid=(B, n_pages),
        in_specs=[pl.BlockSpec((None, R, H), lambda b, p, pt: (pt[b * n_pages + p], 0, 0)),
                  pl.BlockSpec((None, nr, H), lambda b, p, pt: (b, 0, 0))],
        out_specs=[pl.BlockSpec((None, R, H), lambda b, p, pt: (b, p, 0)),
                   pl.BlockSpec((None, nr, H), lambda b, p, pt: (b, 0, 0))],
        scratch_shapes=[pltpu.VMEM((1, H), F32)])
    c, cn = pl.pallas_call(
        _cumsum_body, grid_spec=grid_spec,
        out_shape=[jax.ShapeDtypeStruct((B, n_pages * R, H), F32),
                   jax.ShapeDtypeStruct((B, nr, H), F32)],
        compiler_params=_cp(("parallel", "arbitrary")),
    )(page_table.reshape(-1).astype(I32), pool, new_rows)
    return c, cn


def _softmax_step(s, v, m_prev, l_prev, acc_prev):
    m_new = jnp.maximum(m_prev, jnp.max(s, axis=-1, keepdims=True))
    alpha = jnp.exp(m_prev - m_new)
    p = jnp.exp(s - m_new)
    l_new = alpha * l_prev + jnp.sum(p, axis=-1, keepdims=True)
    acc_new = alpha * acc_prev + jnp.dot(p.astype(BF16), v, preferred_element_type=F32)
    return m_new, l_new, acc_new


def _causal_pairs(nq, tq, tk):
    qs, ks = [], []
    for qi in range(nq):
        last = ((qi + 1) * tq - 1) // tk
        for ki in range(last + 1):
            qs.append(qi); ks.append(ki)
    return np.asarray(qs, np.int32), np.asarray(ks, np.int32)


def _fox_prompt_body(qi_ref, ki_ref, q_ref, k_ref, v_ref, cq_ref, ck_ref, o_ref, m_ref, l_ref, acc_ref,
                     *, tq, tk):
    p = pl.program_id(0)
    qi, ki = qi_ref[p], ki_ref[p]

    @pl.when(ki == 0)
    def _():
        m_ref[...] = jnp.full_like(m_ref, NEG)
        l_ref[...] = jnp.zeros_like(l_ref)
        acc_ref[...] = jnp.zeros_like(acc_ref)

    qpos = qi * tq + lax.broadcasted_iota(I32, (tq, tk), 0)
    kpos = ki * tk + lax.broadcasted_iota(I32, (tq, tk), 1)
    causal = kpos <= qpos
    for h in range(FOX_HEADS):
        sl = slice(h * HEAD_DIM, (h + 1) * HEAD_DIM)
        s = _nt_dot(q_ref[:, sl], k_ref[:, sl]) * ATT_SCALE
        s = s + (cq_ref[:, h:h + 1] - ck_ref[h:h + 1, :])
        s = jnp.where(causal, s, NEG)
        m, l, acc = _softmax_step(s, v_ref[:, sl], m_ref[h], l_ref[h], acc_ref[:, sl])
        m_ref[h] = m
        l_ref[h] = l
        acc_ref[:, sl] = acc

    @pl.when(ki == ((qi + 1) * tq - 1) // tk)
    def _():
        for h in range(FOX_HEADS):
            sl = slice(h * HEAD_DIM, (h + 1) * HEAD_DIM)
            o_ref[:, sl] = (acc_ref[:, sl] / l_ref[h]).astype(BF16)


def _fox_prompt(q, k, v, c, tq=512, tk=512):
    T = q.shape[0]
    tq = _tile(T, tq)
    tk = _tile(T, tk)
    qs, ks = _causal_pairs(T // tq, tq, tk)
    cT = c.T
    grid_spec = pltpu.PrefetchScalarGridSpec(
        num_scalar_prefetch=2, grid=(len(qs),),
        in_specs=[pl.BlockSpec((tq, FOX_W), lambda p, qi, ki: (qi[p], 0)),
                  pl.BlockSpec((tk, FOX_W), lambda p, qi, ki: (ki[p], 0)),
                  pl.BlockSpec((tk, FOX_W), lambda p, qi, ki: (ki[p], 0)),
                  pl.BlockSpec((tq, FOX_HEADS), lambda p, qi, ki: (qi[p], 0)),
                  pl.BlockSpec((FOX_HEADS, tk), lambda p, qi, ki: (0, ki[p]))],
        out_specs=pl.BlockSpec((tq, FOX_W), lambda p, qi, ki: (qi[p], 0)),
        scratch_shapes=[pltpu.VMEM((FOX_HEADS, tq, 1), F32), pltpu.VMEM((FOX_HEADS, tq, 1), F32),
                        pltpu.VMEM((tq, FOX_W), F32)])
    return pl.pallas_call(
        functools.partial(_fox_prompt_body, tq=tq, tk=tk), grid_spec=grid_spec,
        out_shape=jax.ShapeDtypeStruct((T, FOX_W), BF16),
        compiler_params=_cp(("arbitrary",)),
    )(jnp.asarray(qs), jnp.asarray(ks), q, k, v, c, cT)


def _float_key(x):
    b = pltpu.bitcast(x, I32)
    return b ^ ((b >> 31) & jnp.int32(0x7FFFFFFF))


def _kth_key(keys_ref, nchunks, k, row0, rows):
    def count_ge(cand):
        def body(c, acc):
            kk = keys_ref[c, row0:row0 + rows, :]
            return acc + jnp.where(kk >= cand, 1.0, 0.0)
        acc = lax.fori_loop(0, nchunks, body, jnp.zeros((rows, LANES), F32))
        return jnp.sum(acc, axis=1, keepdims=True)

    def bit_body(i, t):
        cand = t + lax.shift_left(jnp.int32(1), 31 - i)
        return jnp.where(count_ge(cand) >= k, cand, t)

    return lax.fori_loop(0, 32, bit_body, jnp.full((rows, 1), INT_MIN, I32))


def _idx_prompt_body(q_ref, w_ref, ka_ref, kb_ref, keys_ref, thr_ref, *, tq, kc, n_sel, row_group):
    qi = pl.program_id(0)
    nck = keys_ref.shape[0]
    per = kc // LANES
    n_super = ((qi + 1) * tq + kc - 1) // kc

    def super_body(sc, carry):
        k0 = pl.multiple_of(sc * kc, kc)
        ka = ka_ref[pl.ds(k0, kc), :]
        kb = kb_ref[pl.ds(k0, kc), :]
        acc = jnp.zeros((tq, kc), F32)
        for j in range(IDX_QW // LANES):
            q2 = q_ref[:, j * LANES:(j + 1) * LANES]
            acc = acc + w_ref[:, 2 * j:2 * j + 1] * jnp.maximum(_nt_dot(q2, ka), 0.0)
            acc = acc + w_ref[:, 2 * j + 1:2 * j + 2] * jnp.maximum(_nt_dot(q2, kb), 0.0)
        qpos = qi * tq + lax.broadcasted_iota(I32, (tq, kc), 0)
        kpos = sc * kc + lax.broadcasted_iota(I32, (tq, kc), 1)
        key = jnp.where(kpos <= qpos, _float_key(acc), INT_MIN)
        for c in range(per):
            keys_ref[sc * per + c] = key[:, c * LANES:(c + 1) * LANES]
        return carry

    lax.fori_loop(0, n_super, super_body, 0)

    def fill_body(c, carry):
        keys_ref[c] = jnp.full((tq, LANES), INT_MIN, I32)
        return carry

    lax.fori_loop(n_super * per, nck, fill_body, 0)
    for rg in range(tq // row_group):
        thr_ref[rg * row_group:(rg + 1) * row_group, :] = _kth_key(
            keys_ref, n_super * per, n_sel, rg * row_group, row_group)


def _idx_prompt(qi_bf, wi, kia, kib, n_sel, tq=256, kc=512):
    T = qi_bf.shape[0]
    tq = _tile(T, tq)
    kc = _tile(T, kc)
    nck = T // LANES
    row_group = min(tq, 128)
    keys, thr = pl.pallas_call(
        functools.partial(_idx_prompt_body, tq=tq, kc=kc, n_sel=n_sel, row_group=row_group),
        grid=(T // tq,),
        in_specs=[pl.BlockSpec((tq, IDX_QW), lambda i: (i, 0)),
                  pl.BlockSpec((tq, IDX_HEADS), lambda i: (i, 0)),
                  pl.BlockSpec((T, LANES), lambda i: (0, 0)),
                  pl.BlockSpec((T, LANES), lambda i: (0, 0))],
        out_specs=[pl.BlockSpec((None, nck, tq, LANES), lambda i: (i, 0, 0, 0)),
                   pl.BlockSpec((tq, 1), lambda i: (i, 0))],
        out_shape=[jax.ShapeDtypeStruct((T // tq, nck, tq, LANES), I32),
                   jax.ShapeDtypeStruct((T, 1), I32)],
        compiler_params=_cp(("parallel",)),
    )(qi_bf, wi, kia, kib)
    return keys, thr, tq


def _dsa_prompt_body(qi_ref, ki_ref, q_ref, k_ref, v_ref, keys_ref, thr_ref, o_ref, m_ref, l_ref, acc_ref,
                     *, tq, tk):
    p = pl.program_id(0)
    qi, ki = qi_ref[p], ki_ref[p]

    @pl.when(ki == 0)
    def _():
        m_ref[...] = jnp.full_like(m_ref, NEG)
        l_ref[...] = jnp.zeros_like(l_ref)
        acc_ref[...] = jnp.zeros_like(acc_ref)

    thr = thr_ref[...]
    qpos = qi * tq + lax.broadcasted_iota(I32, (tq, LANES), 0)
    lane = lax.broadcasted_iota(I32, (tq, LANES), 1)
    parts = []
    for c in range(tk // LANES):
        kpos = ki * tk + c * LANES + lane
        parts.append(jnp.where((keys_ref[c] >= thr) & (kpos <= qpos), 0.0, NEG))
    bias = jnp.concatenate(parts, axis=1)
    bias = jnp.concatenate([bias] * DSA_GROUP, axis=0)
    for g in range(DSA_KV_HEADS):
        kv = slice(g * HEAD_DIM, (g + 1) * HEAD_DIM)
        qs = jnp.concatenate(
            [q_ref[:, (g * DSA_GROUP + i) * HEAD_DIM:(g * DSA_GROUP + i + 1) * HEAD_DIM]
             for i in range(DSA_GROUP)], axis=0)
        s = _nt_dot(qs, k_ref[:, kv]) * ATT_SCALE + bias
        m, l, acc = _softmax_step(s, v_ref[:, kv], m_ref[g], l_ref[g], acc_ref[g])
        m_ref[g] = m
        l_ref[g] = l
        acc_ref[g] = acc

    @pl.when(ki == ((qi + 1) * tq - 1) // tk)
    def _():
        for g in range(DSA_KV_HEADS):
            o = acc_ref[g] / l_ref[g]
            for i in range(DSA_GROUP):
                h = g * DSA_GROUP + i
                o_ref[:, h * HEAD_DIM:(h + 1) * HEAD_DIM] = o[i * tq:(i + 1) * tq, :].astype(BF16)


def _dsa_prompt(q, k, v, keys, thr, tq, tk=512):
    T = q.shape[0]
    tk = _tile(T, tk)
    qs, ks = _causal_pairs(T // tq, tq, tk)
    per = tk // LANES
    grid_spec = pltpu.PrefetchScalarGridSpec(
        num_scalar_prefetch=2, grid=(len(qs),),
        in_specs=[pl.BlockSpec((tq, DSA_QW), lambda p, qi, ki: (qi[p], 0)),
                  pl.BlockSpec((tk, DSA_KW), lambda p, qi, ki: (ki[p], 0)),
                  pl.BlockSpec((tk, DSA_KW), lambda p, qi, ki: (ki[p], 0)),
                  pl.BlockSpec((None, per, tq, LANES), lambda p, qi, ki: (qi[p], ki[p], 0, 0)),
                  pl.BlockSpec((tq, 1), lambda p, qi, ki: (qi[p], 0))],
        out_specs=pl.BlockSpec((tq, DSA_QW), lambda p, qi, ki: (qi[p], 0)),
        scratch_shapes=[pltpu.VMEM((DSA_KV_HEADS, DSA_GROUP * tq, 1), F32),
                        pltpu.VMEM((DSA_KV_HEADS, DSA_GROUP * tq, 1), F32),
                        pltpu.VMEM((DSA_KV_HEADS, DSA_GROUP * tq, HEAD_DIM), F32)])
    return pl.pallas_call(
        functools.partial(_dsa_prompt_body, tq=tq, tk=tk), grid_spec=grid_spec,
        out_shape=jax.ShapeDtypeStruct((T, DSA_QW), BF16),
        compiler_params=_cp(("arbitrary",)),
    )(jnp.asarray(qs), jnp.asarray(ks), q, k, v, keys, thr)


def _fox_sample_body(pt_ref, q_ref, cq_ref, ck_ref, k_ref, v_ref, kn_ref, vn_ref, ckn_ref, o_ref,
                     m_ref, l_ref, acc_ref, *, ts):
    p = pl.program_id(1)
    rows = ts * FOX_HEADS

    @pl.when(p == 0)
    def _():
        m_ref[...] = jnp.full_like(m_ref, NEG)
        l_ref[...] = jnp.zeros_like(l_ref)
        acc_ref[...] = jnp.zeros_like(acc_ref)

    q = q_ref[...]
    cq = cq_ref[...]

    def step(k, v, ck, mask):
        s = _nt_dot(q, k) * ATT_SCALE + (cq - jnp.concatenate([ck] * ts, axis=0))
        if mask is not None:
            s = jnp.where(mask, s, NEG)
        m, l, acc = _softmax_step(s, v, m_ref[...], l_ref[...], acc_ref[...])
        m_ref[...] = m
        l_ref[...] = l
        acc_ref[...] = acc

    step(k_ref[...].astype(BF16), v_ref[...].astype(BF16), ck_ref[...], None)

    @pl.when(p == pl.num_programs(1) - 1)
    def _():
        tok = lax.broadcasted_iota(I32, (rows, LANES), 0) // FOX_HEADS
        col = lax.broadcasted_iota(I32, (rows, LANES), 1)
        step(kn_ref[...], vn_ref[...], ckn_ref[...], col <= tok)
        o_ref[...] = acc_ref[...] / l_ref[...]


def _fox_sample(qbd, cq, ckT, cache_k, cache_v, page_table, knew, vnew, cknT, ts):
    B, n_pages = page_table.shape
    R = cache_k.shape[1]
    rows = ts * FOX_HEADS
    paged = lambda b, p, pt: (pt[b * n_pages + p], 0, 0)
    per_b = lambda b, p, pt: (b, 0, 0)
    grid_spec = pltpu.PrefetchScalarGridSpec(
        num_scalar_prefetch=1, grid=(B, n_pages),
        in_specs=[pl.BlockSpec((None, rows, FOX_W), per_b),
                  pl.BlockSpec((None, rows, 1), per_b),
                  pl.BlockSpec((None, FOX_HEADS, R), lambda b, p, pt: (b, 0, p)),
                  pl.BlockSpec((None, R, FOX_W), paged),
                  pl.BlockSpec((None, R, FOX_W), paged),
                  pl.BlockSpec((None, LANES, FOX_W), per_b),
                  pl.BlockSpec((None, LANES, FOX_W), per_b),
                  pl.BlockSpec((None, FOX_HEADS, LANES), per_b)],
        out_specs=pl.BlockSpec((None, rows, FOX_W), per_b),
        scratch_shapes=[pltpu.VMEM((rows, 1), F32), pltpu.VMEM((rows, 1), F32),
                        pltpu.VMEM((rows, FOX_W), F32)])
    return pl.pallas_call(
        functools.partial(_fox_sample_body, ts=ts), grid_spec=grid_spec,
        out_shape=jax.ShapeDtypeStruct((B, rows, FOX_W), F32),
        compiler_params=_cp(("parallel", "arbitrary")),
    )(page_table.reshape(-1).astype(I32), qbd, cq, ckT, cache_k, cache_v, knew, vnew, cknT)


def _idx_sample_body(pt_ref, q_ref, w_ref, k_ref, kn_ref, keys_ref, thr_ref, *, ts, n_sel):
    p = pl.program_id(1)
    n_pages = pl.num_programs(1)
    pad = jnp.full((8 - ts, LANES), INT_MIN, I32)

    def score(k):
        r = jnp.maximum(_nt_dot(q_ref[...], k), 0.0) * w_ref[...]
        return jnp.sum(r.reshape(ts, IDX_HEADS, LANES), axis=1)

    keys_ref[p] = jnp.concatenate([_float_key(score(k_ref[...].astype(BF16))), pad], axis=0)

    @pl.when(p == n_pages - 1)
    def _():
        tok = lax.broadcasted_iota(I32, (ts, LANES), 0)
        col = lax.broadcasted_iota(I32, (ts, LANES), 1)
        kn = jnp.where(col <= tok, _float_key(score(kn_ref[...])), INT_MIN)
        keys_ref[n_pages] = jnp.concatenate([kn, pad], axis=0)
        thr_ref[...] = _kth_key(keys_ref, n_pages + 1, n_sel, 0, 8)


def _idx_sample(q, w, cache_k, page_table, knew, ts, n_sel):
    B, n_pages = page_table.shape
    R = cache_k.shape[1]
    rows = ts * IDX_HEADS
    per_b = lambda b, p, pt: (b, 0, 0)
    grid_spec = pltpu.PrefetchScalarGridSpec(
        num_scalar_prefetch=1, grid=(B, n_pages),
        in_specs=[pl.BlockSpec((None, rows, IDX_DIM), per_b),
                  pl.BlockSpec((None, rows, 1), per_b),
                  pl.BlockSpec((None, R, IDX_DIM), lambda b, p, pt: (pt[b * n_pages + p], 0, 0)),
                  pl.BlockSpec((None, LANES, IDX_DIM), per_b)],
        out_specs=[pl.BlockSpec((None, n_pages + 1, 8, LANES), lambda b, p, pt: (b, 0, 0, 0)),
                   pl.BlockSpec((None, 8, 1), per_b)])
    return pl.pallas_call(
        functools.partial(_idx_sample_body, ts=ts, n_sel=n_sel), grid_spec=grid_spec,
        out_shape=[jax.ShapeDtypeStruct((B, n_pages + 1, 8, LANES), I32),
                   jax.ShapeDtypeStruct((B, 8, 1), I32)],
        compiler_params=_cp(("parallel", "arbitrary")),
    )(page_table.reshape(-1).astype(I32), q, w, cache_k, knew)


def _dsa_sample_body(pt_ref, q_ref, keys_ref, keysn_ref, thr_ref, k_ref, v_ref, kn_ref, vn_ref, o_ref,
                     m_ref, l_ref, acc_ref, *, ts):
    p = pl.program_id(1)

    @pl.when(p == 0)
    def _():
        m_ref[...] = jnp.full_like(m_ref, NEG)
        l_ref[...] = jnp.zeros_like(l_ref)
        acc_ref[...] = jnp.zeros_like(acc_ref)

    q = q_ref[...]
    thr = thr_ref[...]

    def step(k, v, keys):
        sel = jnp.where(keys >= thr, 0.0, NEG)
        bias = jnp.concatenate(
            [jnp.broadcast_to(sel[t:t + 1, :], (DSA_HEADS, LANES)) for t in range(ts)], axis=0)
        s = _nt_dot(q, k) * ATT_SCALE + bias
        m, l, acc = _softmax_step(s, v, m_ref[...], l_ref[...], acc_ref[...])
        m_ref[...] = m
        l_ref[...] = l
        acc_ref[...] = acc

    step(k_ref[...].astype(BF16), v_ref[...].astype(BF16), keys_ref[...])

    @pl.when(p == pl.num_programs(1) - 1)
    def _():
        step(kn_ref[...], vn_ref[...], keysn_ref[...])
        o_ref[...] = acc_ref[...] / l_ref[...]


def _dsa_sample(qbd, keys, thr, cache_k, cache_v, page_table, knew, vnew, ts):
    B, n_pages = page_table.shape
    R = cache_k.shape[1]
    rows = ts * DSA_HEADS
    paged = lambda b, p, pt: (pt[b * n_pages + p], 0, 0)
    per_b = lambda b, p, pt: (b, 0, 0)
    grid_spec = pltpu.PrefetchScalarGridSpec(
        num_scalar_prefetch=1, grid=(B, n_pages),
        in_specs=[pl.BlockSpec((None, rows, DSA_KW), per_b),
                  pl.BlockSpec((None, None, 8, LANES), lambda b, p, pt: (b, p, 0, 0)),
                  pl.BlockSpec((None, None, 8, LANES), lambda b, p, pt: (b, n_pages, 0, 0)),
                  pl.BlockSpec((None, 8, 1), per_b),
                  pl.BlockSpec((None, R, DSA_KW), paged),
                  pl.BlockSpec((None, R, DSA_KW), paged),
                  pl.BlockSpec((None, LANES, DSA_KW), per_b),
                  pl.BlockSpec((None, LANES, DSA_KW), per_b)],
        out_specs=pl.BlockSpec((None, rows, DSA_KW), per_b),
        scratch_shapes=[pltpu.VMEM((rows, 1), F32), pltpu.VMEM((rows, 1), F32),
                        pltpu.VMEM((rows, DSA_KW), F32)])
    return pl.pallas_call(
        functools.partial(_dsa_sample_body, ts=ts), grid_spec=grid_spec,
        out_shape=jax.ShapeDtypeStruct((B, rows, DSA_KW), F32),
        compiler_params=_cp(("parallel", "arbitrary")),
    )(page_table.reshape(-1).astype(I32), qbd, keys, keys, thr, cache_k, cache_v, knew, vnew)


def _top_rows(x, n):
    out = []
    for _ in range(n):
        m = jnp.max(x, axis=0, keepdims=True)
        out.append(m)
        x = jnp.where(x == m, -jnp.inf, x)
    return out


def _peer_select_body(q_ref, keys_ref, s_ref, st_ref, *, n_heads):
    nk = keys_ref.shape[2]
    thr, mx, zinv = [], [], []
    for h in range(n_heads):
        tops = []
        for c in range(2):
            col = (2 * h + c) * nk
            s = _nt_dot(q_ref[:, col:col + nk].astype(BF16), keys_ref[h, c])
            s_ref[:, col:col + nk] = s
            tops.append(jnp.concatenate(_top_rows(s.T, PEER_TOPK), axis=0))
        cand = jnp.concatenate([tops[0][r:r + 1, :] + tops[1] for r in range(PEER_TOPK)], axis=0)
        best = _top_rows(cand, PEER_TOPK)
        z = jnp.ones_like(best[0])
        for r in range(1, PEER_TOPK):
            z = z + jnp.exp(best[r] - best[0])
        thr.append(best[-1]); mx.append(best[0]); zinv.append(1.0 / z)
    st_ref[...] = jnp.concatenate(thr + mx + zinv, axis=0)


def _peer_select(q, keys_bf, tm=128):
    M = q.shape[0]
    n_heads, _, nk, _ = keys_bf.shape
    tm = _tile(M, tm)
    return pl.pallas_call(
        functools.partial(_peer_select_body, n_heads=n_heads),
        grid=(M // tm,),
        in_specs=[pl.BlockSpec((tm, q.shape[1]), lambda i: (i, 0)),
                  pl.BlockSpec(keys_bf.shape, lambda i: (0, 0, 0, 0))],
        out_specs=[pl.BlockSpec((tm, q.shape[1]), lambda i: (i, 0)),
                   pl.BlockSpec((3 * n_heads, tm), lambda i: (0, i))],
        out_shape=[jax.ShapeDtypeStruct((M, q.shape[1]), F32),
                   jax.ShapeDtypeStruct((3 * n_heads, M), F32)],
        compiler_params=_cp(("parallel",)),
    )(q, keys_bf)


def _peer_expert_body(x_ref, ut_ref, v_ref, s0_ref, s1_ref, st_ref, o_ref, *, n_heads, nk, ni):
    e = pl.program_id(1)

    @pl.when(e == 0)
    def _():
        o_ref[...] = jnp.zeros_like(o_ref)

    a = jnp.dot(x_ref[...], ut_ref[...], preferred_element_type=F32)
    act = 0.5 * a * (1.0 + lax.erf(a * (2.0 ** -0.5)))
    gates = []
    for il in range(ni):
        w = None
        for h in range(n_heads):
            tot = s0_ref[:, il * n_heads + h:il * n_heads + h + 1] + s1_ref[:, h * nk:(h + 1) * nk]
            val = jnp.exp(tot - st_ref[:, n_heads + h:n_heads + h + 1]) * st_ref[:, 2 * n_heads + h:2 * n_heads + h + 1]
            val = jnp.where(tot >= st_ref[:, h:h + 1], val, 0.0)
            w = val if w is None else w + val
        gates.append(w)
    gate = jnp.concatenate(gates, axis=1)
    o_ref[...] += jnp.dot((gate * act).astype(BF16), v_ref[...], preferred_element_type=F32)


def _peer_expert(hn_bf, ut_bf, v_bf, s, st, n_heads, nk, tm=512, te=256):
    M, D = hn_bf.shape
    E = ut_bf.shape[1]
    tm = _tile(M, tm)
    te = _tile(E, te)
    ni = te // nk
    s4 = s.reshape(M, n_heads, 2, nk)
    s1 = s4[:, :, 1, :].reshape(M, n_heads * nk)
    s0 = jnp.transpose(s4[:, :, 0, :].reshape(M, n_heads, E // te, ni), (2, 0, 3, 1)).reshape(E // te, M, ni * n_heads)
    return pl.pallas_call(
        functools.partial(_peer_expert_body, n_heads=n_heads, nk=nk, ni=ni),
        grid=(M // tm, E // te),
        in_specs=[pl.BlockSpec((tm, D), lambda i, e: (i, 0)),
                  pl.BlockSpec((D, te), lambda i, e: (0, e)),
                  pl.BlockSpec((te, D), lambda i, e: (e, 0)),
                  pl.BlockSpec((None, tm, ni * n_heads), lambda i, e: (e, i, 0)),
                  pl.BlockSpec((tm, n_heads * nk), lambda i, e: (i, 0)),
                  pl.BlockSpec((tm, 3 * n_heads), lambda i, e: (i, 0))],
        out_specs=pl.BlockSpec((tm, D), lambda i, e: (i, 0)),
        out_shape=jax.ShapeDtypeStruct((M, D), F32),
        compiler_params=_cp(("parallel", "arbitrary")),
    )(hn_bf, ut_bf, v_bf, s0, s1, st)


def _prep_w_in(w_in):
    o = np.cumsum([0, FOX_W, FOX_W, FOX_W, FOX_HEADS, DSA_QW, DSA_KW, DSA_KW, IDX_QW, IDX_DIM, IDX_HEADS])
    seg = lambda i: w_in[:, o[i]:o[i + 1]]
    cols = [seg(0), seg(1), seg(2), seg(4), seg(7), seg(5), seg(6), seg(8), seg(9), seg(3)]
    used = int(o[-1])
    total = -(-used // 768) * 768
    cols.append(jnp.zeros((w_in.shape[0], total - used), w_in.dtype))
    return jnp.concatenate(cols, axis=1).astype(BF16)


def _channel(x, mix_bf, p_l, w_o_bf, g_norm2, w_pq_bf, keys_bf, ut_bf, v_bf, g_norm3, w_gate_bf, w_proj_bf,
             g_ple):
    n_heads, _, nk, _ = keys_bf.shape
    h1 = _mm(mix_bf, w_o_bf, res=x, epi="res")
    q, hn_bf = _mm(h1, w_pq_bf, g=g_norm2, emit_xn=True)
    s, st = _peer_select(q, keys_bf)
    moe = _peer_expert(hn_bf, ut_bf, v_bf, s, st.T, n_heads, nk)
    pn = _rownorm_mm(p_l, w_proj_bf, g_ple)
    return _mm(h1, w_gate_bf, g=g_norm3, x2=moe, res=h1, res2=moe, aux=pn, epi="gate", tm=256)


def _diag_heads(o, ts, n_heads, group):
    B = o.shape[0]
    o5 = o.reshape(B, ts, n_heads, -1, HEAD_DIM)
    idx = (jnp.arange(n_heads) // group).reshape(1, 1, n_heads, 1, 1)
    return jnp.take_along_axis(o5, idx, axis=3).reshape(B * ts, n_heads * HEAD_DIM)


def _block_diag_q(q, ts, n_heads, group):
    B = q.shape[0] // ts
    n_kv = n_heads // group
    q4 = q.reshape(B, ts, n_heads, 1, HEAD_DIM)
    onehot = (jnp.arange(n_heads)[:, None] // group == jnp.arange(n_kv)[None, :])
    return jnp.where(onehot[None, None, :, :, None], q4, jnp.zeros((), q.dtype)).reshape(
        B, ts * n_heads, n_kv * HEAD_DIM)


def _pad_rows(x, n):
    return jnp.pad(x, ((0, 0), (0, n - x.shape[1]), (0, 0)))


def kernel(x_prompt, x_sample, cache_fox_k, cache_fox_v, cache_fox_logf, cache_dsa_k, cache_dsa_v, cache_idx_k, page_table, p_prompt, p_sample, g_norm1, w_in, b_f, g_q_fox, g_k_fox, g_q_dsa, g_k_dsa, w_o, g_norm2, w_peer_q, peer_keys, peer_u, peer_v, g_norm3, w_ple_gate, w_ple_proj, g_ple):
    Bp, Tp, D = x_prompt.shape
    Bs, Ts = x_sample.shape[:2]
    depth = w_in.shape[0]
    n_pages = page_table.shape[1]
    page = cache_fox_k.shape[2]
    past = n_pages * page
    assert Bp == 1 and Ts <= 8 and page == LANES

    h_p = x_prompt.reshape(Tp, D)
    h_s = x_sample.reshape(Bs * Ts, D)
    pos_p = jnp.arange(Tp, dtype=I32)
    pos_s = jnp.tile(past + jnp.arange(Ts, dtype=I32), Bs)
    outs = [[] for _ in range(12)]
    for l in range(depth):
        w_in_bf = _prep_w_in(w_in[l])
        chan_w = (w_o[l].astype(BF16), g_norm2[l], w_peer_q[l].astype(BF16), peer_keys[l].astype(BF16),
                  peer_u[l].T.astype(BF16), peer_v[l].astype(BF16), g_norm3[l], w_ple_gate[l].astype(BF16),
                  w_ple_proj[l].astype(BF16), g_ple[l])
        norm_w = (b_f[l], g_q_fox[l], g_k_fox[l], g_q_dsa[l], g_k_dsa[l])

        z = _mm(h_p, w_in_bf, g=g_norm1[l], tn=768)
        pr = _post(z, pos_p, *norm_w)
        ident = jnp.arange(Tp // page, dtype=I32).reshape(1, -1)
        c, _ = _paged_cumsum(pr["lf"].reshape(Tp // page, page, FOX_HEADS), ident,
                             jnp.zeros((1, 8, FOX_HEADS), F32))
        o_fox = _fox_prompt(pr["qf"], pr["kfb"], pr["vfb"], c[0])
        keys, thr, tq = _idx_prompt(pr["qi"], pr["wi"], pr["kia"], pr["kib"], min(TOPK_MAX, Tp // 4))
        o_dsa = _dsa_prompt(pr["qd"], pr["kdb"], pr["vdb"], keys, thr, tq)
        mix = jnp.concatenate([o_fox, o_dsa], axis=-1)
        h_p = _channel(h_p, mix, p_prompt[l].reshape(Tp, -1), *chan_w)
        for i, (name, shape) in enumerate((("kf", (Bp, Tp, FOX_HEADS, HEAD_DIM)), ("vf", (Bp, Tp, FOX_HEADS, HEAD_DIM)),
                                           ("lf", (Bp, Tp, FOX_HEADS)), ("kd", (Bp, Tp, DSA_KV_HEADS, HEAD_DIM)),
                                           ("vd", (Bp, Tp, DSA_KV_HEADS, HEAD_DIM)), ("ki", (Bp, Tp, IDX_DIM)))):
            outs[i].append(pr[name].reshape(shape))

        zs = _mm(h_s, w_in_bf, g=g_norm1[l], tn=768)
        sr = _post(zs, pos_s, *norm_w)
        lf_new = _pad_rows(sr["lf"].reshape(Bs, Ts, FOX_HEADS), 8)
        c_past, c_new = _paged_cumsum(cache_fox_logf[l], page_table, lf_new)
        qbd = _block_diag_q(sr["qf"], Ts, FOX_HEADS, 1)
        cq = c_new[:, :Ts, :].reshape(Bs, Ts * FOX_HEADS, 1)
        ckT = jnp.transpose(c_past, (0, 2, 1))
        cknT = jnp.transpose(_pad_rows(c_new, LANES), (0, 2, 1))
        o = _fox_sample(qbd, cq, ckT, cache_fox_k[l].reshape(-1, page, FOX_W),
                        cache_fox_v[l].reshape(-1, page, FOX_W), page_table,
                        _pad_rows(sr["kfb"].reshape(Bs, Ts, FOX_W), LANES),
                        _pad_rows(sr["vfb"].reshape(Bs, Ts, FOX_W), LANES), cknT, Ts)
        o_fox_s = _diag_heads(o, Ts, FOX_HEADS, 1)
        n_sel = min(TOPK_MAX, (past + Ts) // 4)
        keys_s, thr_s = _idx_sample(sr["qi"].reshape(Bs, Ts * IDX_HEADS, IDX_DIM),
                                    sr["wi"].reshape(Bs, Ts * IDX_HEADS, 1), cache_idx_k[l], page_table,
                                    _pad_rows(sr["kia"][:, :IDX_DIM].reshape(Bs, Ts, IDX_DIM), LANES), Ts, n_sel)
        od = _dsa_sample(_block_diag_q(sr["qd"], Ts, DSA_HEADS, DSA_GROUP), keys_s, thr_s,
                         cache_dsa_k[l].reshape(-1, page, DSA_KW), cache_dsa_v[l].reshape(-1, page, DSA_KW),
                         page_table, _pad_rows(sr["kdb"].reshape(Bs, Ts, DSA_KW), LANES),
                         _pad_rows(sr["vdb"].reshape(Bs, Ts, DSA_KW), LANES), Ts)
        o_dsa_s = _diag_heads(od, Ts, DSA_HEADS, DSA_GROUP)
        mix_s = jnp.concatenate([o_fox_s, o_dsa_s], axis=-1).astype(BF16)
        h_s = _channel(h_s, mix_s, p_sample[l].reshape(Bs * Ts, -1), *chan_w)
        for i, (name, shape) in enumerate((("kf", (Bs, Ts, FOX_HEADS, HEAD_DIM)), ("vf", (Bs, Ts, FOX_HEADS, HEAD_DIM)),
                                           ("lf", (Bs, Ts, FOX_HEADS)), ("kd", (Bs, Ts, DSA_KV_HEADS, HEAD_DIM)),
                                           ("vd", (Bs, Ts, DSA_KV_HEADS, HEAD_DIM)), ("ki", (Bs, Ts, IDX_DIM)))):
            outs[6 + i].append(sr[name].reshape(shape))
    return (h_p.reshape(Bp, Tp, D), h_s.reshape(Bs, Ts, D)) + tuple(jnp.stack(o) for o in outs)
```

```python
import functools
import math

import numpy as np
import jax
import jax.numpy as jnp
from jax import lax
from jax.experimental import pallas as pl
from jax.experimental.pallas import tpu as pltpu

F32 = jnp.float32
BF16 = jnp.bfloat16
I32 = jnp.int32

HEAD_DIM = 128
FOX_HEADS = 16
DSA_HEADS = 16
DSA_KV_HEADS = 4
DSA_GROUP = DSA_HEADS // DSA_KV_HEADS
IDX_HEADS = 32
IDX_DIM = 64
TOPK_MAX = 256
ROPE_THETA = 10000.0
PEER_TOPK = 16
EPS = 1e-6

FOX_W = FOX_HEADS * HEAD_DIM
DSA_QW = DSA_HEADS * HEAD_DIM
DSA_KW = DSA_KV_HEADS * HEAD_DIM
IDX_QW = IDX_HEADS * IDX_DIM

LANES = 128
NEG = -1e30
INT_MIN = -(2 ** 31)
VMEM_LIMIT = 56 * 1024 * 1024
ATT_SCALE = HEAD_DIM ** -0.5
LOG2E = math.log2(math.e)
QS = 128
KS = 256
WIDTH = 4


def _cp(sem):
    return pltpu.CompilerParams(dimension_semantics=sem, vmem_limit_bytes=VMEM_LIMIT)


def _tile(n, pref):
    t = min(n, pref)
    while n % t:
        t //= 2
    return t


def _nt_dot(a, b):
    return lax.dot_general(a, b, (((1,), (1,)), ((), ())), preferred_element_type=F32)


def _mm_body(*refs, norm, add2, epi, emit_xn, prologue):
    it = iter(refs)
    x_ref = next(it)
    x2_ref = next(it) if add2 else None
    g_ref = next(it) if norm else None
    w_ref = next(it)
    res_ref = next(it) if epi in ("res", "gate") else None
    res2_ref = next(it) if (epi == "gate" and add2) else None
    aux_ref = next(it) if epi == "gate" else None
    o_ref = next(it)
    xo_ref = next(it) if emit_xn else None
    xn_ref = next(it) if prologue else None

    if prologue:
        @pl.when(pl.program_id(1) == 0)
        def _():
            x = x_ref[...].astype(F32)
            if add2:
                x = x + x2_ref[...]
            if norm:
                ms = jnp.mean(x * x, axis=-1, keepdims=True)
                x = x * lax.rsqrt(ms + EPS) * g_ref[...]
            xb = x.astype(BF16)
            xn_ref[...] = xb
            if emit_xn:
                xo_ref[...] = xb
        lhs = xn_ref[...]
    else:
        lhs = x_ref[...]
    acc = jnp.dot(lhs, w_ref[...], preferred_element_type=F32)
    if epi == "res":
        acc = res_ref[...] + acc
    elif epi == "gate":
        r = res_ref[...]
        if add2:
            r = r + res2_ref[...]
        acc = r + jax.nn.sigmoid(acc) * aux_ref[...]
    o_ref[...] = acc


def _mm(x, w, *, g=None, x2=None, res=None, res2=None, aux=None, epi="none", emit_xn=False,
        tm=512, tn=512):
    M, K = x.shape
    N = w.shape[1]
    tm = _tile(M, tm)
    tn = _tile(N, tn)
    norm = g is not None
    add2 = x2 is not None
    prologue = norm or add2 or x.dtype != BF16
    row = pl.BlockSpec((tm, K), lambda i, j: (i, 0))
    blk = pl.BlockSpec((tm, tn), lambda i, j: (i, j))
    ins, specs = [x], [row]
    if add2:
        ins.append(x2); specs.append(row)
    if norm:
        ins.append(g.reshape(1, K)); specs.append(pl.BlockSpec((1, K), lambda i, j: (0, 0)))
    ins.append(w); specs.append(pl.BlockSpec((K, tn), lambda i, j: (0, j)))
    if epi in ("res", "gate"):
        ins.append(res); specs.append(blk)
    if epi == "gate" and add2:
        ins.append(res2); specs.append(blk)
    if epi == "gate":
        ins.append(aux); specs.append(blk)
    out_shape = [jax.ShapeDtypeStruct((M, N), F32)]
    out_specs = [blk]
    if emit_xn:
        out_shape.append(jax.ShapeDtypeStruct((M, K), BF16))
        out_specs.append(row)
    scratch = [pltpu.VMEM((tm, K), BF16)] if prologue else []
    outs = pl.pallas_call(
        functools.partial(_mm_body, norm=norm, add2=add2, epi=epi, emit_xn=emit_xn, prologue=prologue),
        grid=(M // tm, N // tn),
        in_specs=specs, out_specs=out_specs, out_shape=out_shape, scratch_shapes=scratch,
        compiler_params=_cp(("parallel", "arbitrary")), name="mm_" + epi,
    )(*ins)
    return outs if emit_xn else outs[0]


def _rownorm_mm_body(x_ref, w_ref, g_ref, o_ref):
    y = jnp.dot(x_ref[...].astype(BF16), w_ref[...], preferred_element_type=F32)
    ms = jnp.mean(y * y, axis=-1, keepdims=True)
    o_ref[...] = y * lax.rsqrt(ms + EPS) * g_ref[...]


def _rownorm_mm(x, w, g, tm=256):
    M, K = x.shape
    N = w.shape[1]
    tm = _tile(M, tm)
    return pl.pallas_call(
        _rownorm_mm_body,
        grid=(M // tm,),
        in_specs=[pl.BlockSpec((tm, K), lambda i: (i, 0)),
                  pl.BlockSpec((K, N), lambda i: (0, 0)),
                  pl.BlockSpec((1, N), lambda i: (0, 0))],
        out_specs=pl.BlockSpec((tm, N), lambda i: (i, 0)),
        out_shape=jax.ShapeDtypeStruct((M, N), F32),
        compiler_params=_cp(("parallel",)), name="rownorm_mm",
    )(x, w, g.reshape(1, N))


def _head_norm(x, g):
    ms = jnp.mean(x * x, axis=-1, keepdims=True)
    return x * lax.rsqrt(ms + EPS) * g


def _rope_full(x, cos, sin):
    return x * cos + pltpu.roll(x, HEAD_DIM // 2, 1) * sin


def _rope_pair(x, cos, sin, lane):
    half = IDX_DIM // 2
    rot = jnp.where((lane % IDX_DIM) < half, pltpu.roll(x, LANES - half, 1), pltpu.roll(x, half, 1))
    return x * cos + rot * sin


def _post_body(zqf, zkf, zvf, zqd, zqi, zkv, zs, cd, sd, ci, si, bf, gqf, gkf, gqd, gkd,
               qf_o, kf_o, kfb_o, vf_o, vfb_o, lf_o, qd_o, kd_o, kdb_o, vd_o, vdb_o,
               qi_o, ki_o, kia_o, kib_o, wi_o):
    tm = zqf.shape[0]
    cos_d, sin_d = cd[...], sd[...]
    cos_i, sin_i = ci[...], si[...]
    lane = lax.broadcasted_iota(I32, (tm, LANES), 1)
    for h in range(FOX_HEADS):
        sl = slice(h * HEAD_DIM, (h + 1) * HEAD_DIM)
        qf_o[:, sl] = _head_norm(zqf[:, sl], gqf[...]).astype(BF16)
        k = _head_norm(zkf[:, sl], gkf[...])
        kf_o[:, sl] = k
        kfb_o[:, sl] = k.astype(BF16)
        v = zvf[:, sl]
        vf_o[:, sl] = v
        vfb_o[:, sl] = v.astype(BF16)
    for h in range(DSA_HEADS):
        sl = slice(h * HEAD_DIM, (h + 1) * HEAD_DIM)
        qd_o[:, sl] = _rope_full(_head_norm(zqd[:, sl], gqd[...]), cos_d, sin_d).astype(BF16)
    for h in range(DSA_KV_HEADS):
        sl = slice(h * HEAD_DIM, (h + 1) * HEAD_DIM)
        k = _rope_full(_head_norm(zkv[:, sl], gkd[...]), cos_d, sin_d)
        kd_o[:, sl] = k
        kdb_o[:, sl] = k.astype(BF16)
        v = zkv[:, DSA_KW + h * HEAD_DIM: DSA_KW + (h + 1) * HEAD_DIM]
        vd_o[:, sl] = v
        vdb_o[:, sl] = v.astype(BF16)
    for j in range(IDX_QW // LANES):
        sl = slice(j * LANES, (j + 1) * LANES)
        qi_o[:, sl] = _rope_pair(zqi[:, sl], cos_i, sin_i, lane).astype(BF16)
    x = zs[...]
    y = _rope_pair(x, cos_i, sin_i, lane)
    ki_o[...] = y[:, :IDX_DIM]
    ka = jnp.where(lane < IDX_DIM, y, 0.0)
    kia_o[...] = ka.astype(BF16)
    kib_o[...] = pltpu.roll(ka, IDX_DIM, 1).astype(BF16)
    wi_o[...] = x[:, IDX_DIM:IDX_DIM + IDX_HEADS] * (IDX_HEADS ** -0.5 * IDX_DIM ** -0.5)
    fl = x[:, IDX_DIM + IDX_HEADS:IDX_DIM + IDX_HEADS + FOX_HEADS] + bf[...]
    lf_o[...] = jnp.minimum(fl, 0.0) - jnp.log1p(jnp.exp(-jnp.abs(fl)))


def _post(z, pos, b_f, g_q_fox, g_k_fox, g_q_dsa, g_k_dsa, tm=128):
    M = z.shape[0]
    tm = _tile(M, tm)
    posf = pos.astype(F32)[:, None]
    hd = HEAD_DIM // 2
    ang = posf * (ROPE_THETA ** (-jnp.arange(hd, dtype=F32) * 2.0 / HEAD_DIM))
    cd = jnp.concatenate([jnp.cos(ang), jnp.cos(ang)], axis=-1)
    sd = jnp.concatenate([-jnp.sin(ang), jnp.sin(ang)], axis=-1)
    hi = IDX_DIM // 2
    angi = posf * (ROPE_THETA ** (-jnp.arange(hi, dtype=F32) * 2.0 / IDX_DIM))
    ci = jnp.tile(jnp.cos(angi), (1, 4))
    si = jnp.tile(jnp.concatenate([-jnp.sin(angi), jnp.sin(angi)], axis=-1), (1, 2))

    def zspec(w, c):
        return pl.BlockSpec((tm, w), lambda i: (i, c))

    def rspec(w):
        return pl.BlockSpec((tm, w), lambda i: (i, 0))

    def cspec(w):
        return pl.BlockSpec((1, w), lambda i: (0, 0))

    small_col = (5 * FOX_W + 2 * DSA_KW) // LANES
    in_specs = [zspec(FOX_W, 0), zspec(FOX_W, 1), zspec(FOX_W, 2), zspec(DSA_QW, 3), zspec(IDX_QW, 4),
                zspec(2 * DSA_KW, 5 * FOX_W // (2 * DSA_KW)), zspec(LANES, small_col),
                rspec(LANES), rspec(LANES), rspec(LANES), rspec(LANES),
                cspec(FOX_HEADS), cspec(HEAD_DIM), cspec(HEAD_DIM), cspec(HEAD_DIM), cspec(HEAD_DIM)]
    outs = [(FOX_W, BF16), (FOX_W, F32), (FOX_W, BF16), (FOX_W, F32), (FOX_W, BF16), (FOX_HEADS, F32),
            (DSA_QW, BF16), (DSA_KW, F32), (DSA_KW, BF16), (DSA_KW, F32), (DSA_KW, BF16),
            (IDX_QW, BF16), (IDX_DIM, F32), (LANES, BF16), (LANES, BF16), (IDX_HEADS, F32)]
    res = pl.pallas_call(
        _post_body,
        grid=(M // tm,),
        in_specs=in_specs,
        out_specs=[rspec(w) for w, _ in outs],
        out_shape=[jax.ShapeDtypeStruct((M, w), d) for w, d in outs],
        compiler_params=_cp(("parallel",)), name="post_proj",
    )(z, z, z, z, z, z, z, cd, sd, ci, si, b_f.reshape(1, -1), g_q_fox.reshape(1, -1),
      g_k_fox.reshape(1, -1), g_q_dsa.reshape(1, -1), g_k_dsa.reshape(1, -1))
    names = ("qf", "kf", "kfb", "vf", "vfb", "lf", "qd", "kd", "kdb", "vd", "vdb", "qi", "ki", "kia",
             "kib", "wi")
    return dict(zip(names, res))


def _split3(x):
    hi = x.astype(BF16)
    r1 = x - hi.astype(F32)
    mid = r1.astype(BF16)
    lo = (r1 - mid.astype(F32)).astype(BF16)
    return hi, mid, lo


def _tri_ones(n):
    r = lax.broadcasted_iota(I32, (n, n), 0)
    c = lax.broadcasted_iota(I32, (n, n), 1)
    return jnp.where(c <= r, 1.0, 0.0).astype(BF16)


def _tri_cumsum(tri, x):
    hi, mid, lo = _split3(x)
    dot = lambda a: jnp.dot(tri, a, preferred_element_type=F32)
    return dot(hi) + (dot(mid) + dot(lo))


def _page_group(n_pages, pref):
    g = min(n_pages, pref)
    while n_pages % g:
        g -= 1
    return g


def _paged_specs(G, n_pages, block):
    def spec(g):
        return pl.BlockSpec(block, lambda b, p, pt: (pt[b * n_pages + p * G + g], 0, 0))
    return [spec(g) for g in range(G)]


def _cumsum_body(pt_ref, *refs, G):
    lf_refs = refs[:G]
    new_ref, c_ref, cn_ref, carry_ref = refs[G:]
    p = pl.program_id(1)
    R = lf_refs[0].shape[0]

    @pl.when(p == 0)
    def _():
        carry_ref[...] = jnp.zeros_like(carry_ref)

    tri = _tri_ones(R)
    carry = carry_ref[...]
    for g in range(G):
        c = _tri_cumsum(tri, lf_refs[g][...]) + carry
        c_ref[g * R:(g + 1) * R, :] = c
        carry = c[-1:, :]
    carry_ref[...] = carry

    @pl.when(p == pl.num_programs(1) - 1)
    def _():
        nr = new_ref.shape[0]
        cn_ref[...] = _tri_cumsum(_tri_ones(nr), new_ref[...]) + carry


def _paged_cumsum(pool, page_table, new_rows, group=16):
    B, n_pages = page_table.shape
    R, H = pool.shape[1:]
    nr = new_rows.shape[1]
    G = _page_group(n_pages, group)
    grid_spec = pltpu.PrefetchScalarGridSpec(
        num_scalar_prefetch=1, grid=(B, n_pages // G),
        in_specs=_paged_specs(G, n_pages, (None, R, H)) + [
            pl.BlockSpec((None, nr, H), lambda b, p, pt: (b, 0, 0))],
        out_specs=[pl.BlockSpec((None, G * R, H), lambda b, p, pt: (b, p, 0)),
                   pl.BlockSpec((None, nr, H), lambda b, p, pt: (b, 0, 0))],
        scratch_shapes=[pltpu.VMEM((1, H), F32)])
    c, cn = pl.pallas_call(
        functools.partial(_cumsum_body, G=G), grid_spec=grid_spec,
        out_shape=[jax.ShapeDtypeStruct((B, n_pages * R, H), F32),
                   jax.ShapeDtypeStruct((B, nr, H), F32)],
        compiler_params=_cp(("parallel", "arbitrary")), name="paged_cumsum",
    )(page_table.reshape(-1).astype(I32), *([pool] * G), new_rows)
    return c, cn


def _softmax_step(s, vs, m_prev, l_prev, acc_prev):
    m_new = jnp.maximum(m_prev, jnp.max(s, axis=-1, keepdims=True))
    alpha = jnp.exp(m_prev - m_new)
    p = jnp.exp(s - m_new)
    l_new = alpha * l_prev + jnp.sum(p, axis=-1, keepdims=True)
    pb = p.astype(BF16)
    pv = None
    for g, v in enumerate(vs):
        d = jnp.dot(pb[:, g * LANES:(g + 1) * LANES], v, preferred_element_type=F32)
        pv = d if pv is None else pv + d
    return m_new, l_new, alpha * acc_prev + pv


def _causal_pairs(nq):
    qs, ks = [], []
    for qi in range(nq):
        for ki in range(qi + 1):
            qs.append(qi); ks.append(ki)
    return np.asarray(qs, np.int32), np.asarray(ks, np.int32)


def _init_softmax_state(m_ref, l_ref, acc_ref):
    m_ref[...] = jnp.full_like(m_ref, NEG)
    l_ref[...] = jnp.zeros_like(l_ref)
    acc_ref[...] = jnp.zeros_like(acc_ref)


def _tile_plan(t, diag):
    plan = []
    for r in range(t // QS):
        for c in range(t // KS):
            if diag and c * KS > r * QS + QS - 1:
                continue
            plan.append((r, c, diag and (c + 1) * KS - 1 > r * QS))
    return plan


def _attn_tiles_t(ks, qs, vts, ms, ls, accs, *, biases=None, masks=None, key_sub=None, qry_add=None):
    n = len(ks)
    ts = [_nt_dot(ks[i], qs[i]) * (ATT_SCALE * LOG2E) for i in range(n)]
    if key_sub is not None:
        ts = [ts[i] - key_sub for i in range(n)]
    if biases is not None:
        ts = [ts[i] + biases[i] for i in range(n)]
    if masks is not None:
        ts = [ts[i] if masks[i] is None else jnp.where(masks[i], ts[i], NEG) for i in range(n)]
    mx = [jnp.max(ts[i], axis=0, keepdims=True) for i in range(n)]
    if qry_add is not None:
        mx = [mx[i] + qry_add[i] for i in range(n)]
    m_new = [jnp.maximum(ms[i], mx[i]) for i in range(n)]
    shift = [-m_new[i] if qry_add is None else qry_add[i] - m_new[i] for i in range(n)]
    ps = [jnp.exp2(ts[i] + shift[i]) for i in range(n)]
    alphas = [jnp.exp2(ms[i] - m_new[i]) for i in range(n)]
    l_new = [alphas[i] * ls[i] + jnp.sum(ps[i], axis=0, keepdims=True) for i in range(n)]
    pv = [jnp.dot(vts[i], ps[i].astype(BF16), preferred_element_type=F32) for i in range(n)]
    acc_new = [alphas[i] * accs[i] + pv[i] for i in range(n)]
    return m_new, l_new, acc_new


def _tile_causal_t(r, c):
    key = lax.broadcasted_iota(I32, (KS, QS), 0)
    qry = lax.broadcasted_iota(I32, (KS, QS), 1)
    return key + (c * KS - r * QS) <= qry


def _fox_prompt_body(qi_ref, ki_ref, q_ref, k_ref, vt_ref, cq_ref, ck_ref, o_ref, m_ref, l_ref, acc_ref, *, t):
    p = pl.program_id(0)
    qi, ki = qi_ref[p], ki_ref[p]

    @pl.when(ki == 0)
    def _():
        _init_softmax_state(m_ref, l_ref, acc_ref)

    def run(diag):
        plan = _tile_plan(t, diag)
        for h in range(FOX_HEADS):
            hd = slice(h * HEAD_DIM, (h + 1) * HEAD_DIM)
            qcs = [slice(r * QS, (r + 1) * QS) for r in range(t // QS)]
            qs = [q_ref[qc, hd] for qc in qcs]
            cq2 = [cq_ref[h:h + 1, qc] * LOG2E for qc in qcs]
            ms = [m_ref[h:h + 1, qc] for qc in qcs]
            ls = [l_ref[h:h + 1, qc] for qc in qcs]
            accs = [acc_ref[hd, qc] for qc in qcs]
            for c in range(t // KS):
                kc = slice(c * KS, (c + 1) * KS)
                act = [(r, msk) for r, cc, msk in plan if cc == c]
                ck2 = jnp.broadcast_to(ck_ref[kc, h:h + 1] * LOG2E, (KS, QS))
                for g0 in range(0, len(act), WIDTH):
                    grp = act[g0:g0 + WIDTH]
                    k = k_ref[kc, hd]
                    vt = vt_ref[hd, kc]
                    rs = [r for r, _ in grp]
                    mo, lo, ao = _attn_tiles_t(
                        [k] * len(grp), [qs[r] for r in rs], [vt] * len(grp),
                        [ms[r] for r in rs], [ls[r] for r in rs], [accs[r] for r in rs],
                        key_sub=ck2, qry_add=[cq2[r] for r in rs],
                        masks=[_tile_causal_t(r, c) if msk else None for r, msk in grp])
                    for i, r in enumerate(rs):
                        ms[r], ls[r], accs[r] = mo[i], lo[i], ao[i]
            for r, qc in enumerate(qcs):
                if diag:
                    o_ref[hd, qc] = (accs[r] / ls[r]).astype(BF16)
                else:
                    m_ref[h:h + 1, qc] = ms[r]
                    l_ref[h:h + 1, qc] = ls[r]
                    acc_ref[hd, qc] = accs[r]

    @pl.when(ki == qi)
    def _():
        run(True)

    @pl.when(ki != qi)
    def _():
        run(False)


def _fox_prompt(q, k, v, c, t=512):
    T = q.shape[0]
    t = max(_tile(T, t), KS)
    qs, ks = _causal_pairs(T // t)
    grid_spec = pltpu.PrefetchScalarGridSpec(
        num_scalar_prefetch=2, grid=(len(qs),),
        in_specs=[pl.BlockSpec((t, FOX_W), lambda p, qi, ki: (qi[p], 0)),
                  pl.BlockSpec((t, FOX_W), lambda p, qi, ki: (ki[p], 0)),
                  pl.BlockSpec((FOX_W, t), lambda p, qi, ki: (0, ki[p])),
                  pl.BlockSpec((FOX_HEADS, t), lambda p, qi, ki: (0, qi[p])),
                  pl.BlockSpec((t, FOX_HEADS), lambda p, qi, ki: (ki[p], 0))],
        out_specs=pl.BlockSpec((FOX_W, t), lambda p, qi, ki: (0, qi[p])),
        scratch_shapes=[pltpu.VMEM((FOX_HEADS, t), F32), pltpu.VMEM((FOX_HEADS, t), F32),
                        pltpu.VMEM((FOX_W, t), F32)])
    ot = pl.pallas_call(
        functools.partial(_fox_prompt_body, t=t), grid_spec=grid_spec,
        out_shape=jax.ShapeDtypeStruct((FOX_W, T), BF16),
        compiler_params=_cp(("arbitrary",)), name="fox_prompt",
    )(jnp.asarray(qs), jnp.asarray(ks), q, k, v.T, c.T, c)
    return ot.T


def _float_key(x):
    b = pltpu.bitcast(x, I32)
    return b ^ ((b >> 31) & jnp.int32(0x7FFFFFFF))


def _kth_key(keys_ref, nchunks, k, row0, rows):
    def count_ge(cand):
        def body(c, acc):
            kk = keys_ref[c, row0:row0 + rows, :]
            return acc + jnp.where(kk >= cand, 1.0, 0.0)
        acc = lax.fori_loop(0, nchunks, body, jnp.zeros((rows, LANES), F32))
        return jnp.sum(acc, axis=1, keepdims=True)

    def bit_body(i, t):
        cand = t + lax.shift_left(jnp.int32(1), 31 - i)
        return jnp.where(count_ge(cand) >= k, cand, t)

    return lax.fori_loop(0, 32, bit_body, jnp.full((rows, 1), INT_MIN, I32))


def _idx_prompt_body(q_ref, w_ref, ka_ref, kb_ref, keys_ref, thr_ref, *, tq, kc, n_sel, row_group):
    qi = pl.program_id(0)
    nck = keys_ref.shape[0]
    per = kc // LANES
    n_super = ((qi + 1) * tq + kc - 1) // kc

    def super_body(sc, carry):
        k0 = pl.multiple_of(sc * kc, kc)
        ka = ka_ref[pl.ds(k0, kc), :]
        kb = kb_ref[pl.ds(k0, kc), :]
        acc = jnp.zeros((tq, kc), F32)
        for j in range(IDX_QW // LANES):
            q2 = q_ref[:, j * LANES:(j + 1) * LANES]
            acc = acc + w_ref[:, 2 * j:2 * j + 1] * jnp.maximum(_nt_dot(q2, ka), 0.0)
            acc = acc + w_ref[:, 2 * j + 1:2 * j + 2] * jnp.maximum(_nt_dot(q2, kb), 0.0)
        qpos = qi * tq + lax.broadcasted_iota(I32, (tq, kc), 0)
        kpos = sc * kc + lax.broadcasted_iota(I32, (tq, kc), 1)
        key = jnp.where(kpos <= qpos, _float_key(acc), INT_MIN)
        for c in range(per):
            keys_ref[sc * per + c] = key[:, c * LANES:(c + 1) * LANES]
        return carry

    lax.fori_loop(0, n_super, super_body, 0)

    def fill_body(c, carry):
        keys_ref[c] = jnp.full((tq, LANES), INT_MIN, I32)
        return carry

    lax.fori_loop(n_super * per, nck, fill_body, 0)
    for rg in range(tq // row_group):
        thr_ref[rg * row_group:(rg + 1) * row_group, :] = _kth_key(
            keys_ref, n_super * per, n_sel, rg * row_group, row_group)


def _idx_prompt(qi_bf, wi, kia, kib, n_sel, tq=256, kc=512):
    T = qi_bf.shape[0]
    tq = _tile(T, tq)
    kc = _tile(T, kc)
    nck = T // LANES
    row_group = min(tq, 128)
    keys, thr = pl.pallas_call(
        functools.partial(_idx_prompt_body, tq=tq, kc=kc, n_sel=n_sel, row_group=row_group),
        grid=(T // tq,),
        in_specs=[pl.BlockSpec((tq, IDX_QW), lambda i: (i, 0)),
                  pl.BlockSpec((tq, IDX_HEADS), lambda i: (i, 0)),
                  pl.BlockSpec((T, LANES), lambda i: (0, 0)),
                  pl.BlockSpec((T, LANES), lambda i: (0, 0))],
        out_specs=[pl.BlockSpec((None, nck, tq, LANES), lambda i: (i, 0, 0, 0)),
                   pl.BlockSpec((tq, 1), lambda i: (i, 0))],
        out_shape=[jax.ShapeDtypeStruct((T // tq, nck, tq, LANES), I32),
                   jax.ShapeDtypeStruct((T, 1), I32)],
        compiler_params=_cp(("parallel",)), name="idx_prompt",
    )(qi_bf, wi, kia, kib)
    return keys, thr, tq


def _dsa_prompt_body(qi_ref, ki_ref, q_ref, k_ref, vt_ref, keys_ref, thr_ref, o_ref, m_ref, l_ref, acc_ref, *, t):
    p = pl.program_id(0)
    qi, ki = qi_ref[p], ki_ref[p]

    @pl.when(ki == 0)
    def _():
        _init_softmax_state(m_ref, l_ref, acc_ref)

    def run(diag):
        for r, c, msk in _tile_plan(t, diag):
            qc = slice(r * QS, (r + 1) * QS)
            kc = slice(c * KS, (c + 1) * KS)
            keys = jnp.concatenate([keys_ref[c * (KS // LANES) + j, qc, :] for j in range(KS // LANES)], axis=1)
            bias = jnp.where(keys >= thr_ref[qc, :], 0.0, NEG).T
            if msk:
                bias = jnp.where(_tile_causal_t(r, c), bias, NEG)
            for g in range(DSA_KV_HEADS):
                kv = slice(g * HEAD_DIM, (g + 1) * HEAD_DIM)
                hs = [g * DSA_GROUP + i for i in range(DSA_GROUP)]
                hds = [slice(h * HEAD_DIM, (h + 1) * HEAD_DIM) for h in hs]
                n = len(hs)
                mo, lo, ao = _attn_tiles_t(
                    [k_ref[kc, kv]] * n, [q_ref[qc, hd] for hd in hds], [vt_ref[kv, kc]] * n,
                    [m_ref[h:h + 1, qc] for h in hs], [l_ref[h:h + 1, qc] for h in hs],
                    [acc_ref[hd, qc] for hd in hds], biases=[bias] * n)
                for i, h in enumerate(hs):
                    m_ref[h:h + 1, qc] = mo[i]
                    l_ref[h:h + 1, qc] = lo[i]
                    acc_ref[hds[i], qc] = ao[i]

    @pl.when(ki == qi)
    def _():
        run(True)
        for h in range(DSA_HEADS):
            hd = slice(h * HEAD_DIM, (h + 1) * HEAD_DIM)
            o_ref[hd, :] = (acc_ref[hd, :] / l_ref[h:h + 1, :]).astype(BF16)

    @pl.when(ki != qi)
    def _():
        run(False)


def _dsa_prompt(q, k, v, keys, thr, t):
    T = q.shape[0]
    qs, ks = _causal_pairs(T // t)
    per = t // LANES
    grid_spec = pltpu.PrefetchScalarGridSpec(
        num_scalar_prefetch=2, grid=(len(qs),),
        in_specs=[pl.BlockSpec((t, DSA_QW), lambda p, qi, ki: (qi[p], 0)),
                  pl.BlockSpec((t, DSA_KW), lambda p, qi, ki: (ki[p], 0)),
                  pl.BlockSpec((DSA_KW, t), lambda p, qi, ki: (0, ki[p])),
                  pl.BlockSpec((None, per, t, LANES), lambda p, qi, ki: (qi[p], ki[p], 0, 0)),
                  pl.BlockSpec((t, 1), lambda p, qi, ki: (qi[p], 0))],
        out_specs=pl.BlockSpec((DSA_QW, t), lambda p, qi, ki: (0, qi[p])),
        scratch_shapes=[pltpu.VMEM((DSA_HEADS, t), F32), pltpu.VMEM((DSA_HEADS, t), F32),
                        pltpu.VMEM((DSA_QW, t), F32)])
    ot = pl.pallas_call(
        functools.partial(_dsa_prompt_body, t=t), grid_spec=grid_spec,
        out_shape=jax.ShapeDtypeStruct((DSA_QW, T), BF16),
        compiler_params=_cp(("arbitrary",)), name="dsa_prompt",
    )(jnp.asarray(qs), jnp.asarray(ks), q, k, v.T, keys, thr)
    return ot.T


def _fox_sample_body(pt_ref, q_ref, cq_ref, ck_ref, *refs, ts, G):
    k_refs, v_refs = refs[:G], refs[G:2 * G]
    kn_ref, vn_ref, ckn_ref, o_ref, m_ref, l_ref, acc_ref = refs[2 * G:]
    p = pl.program_id(1)
    rows = ts * FOX_HEADS

    @pl.when(p == 0)
    def _():
        _init_softmax_state(m_ref, l_ref, acc_ref)

    q = q_ref[...]
    cq = cq_ref[...]

    def step(ks, vs, ck, mask):
        s = jnp.concatenate([_nt_dot(q, k) for k in ks], axis=1) * ATT_SCALE
        s = s + (cq - jnp.concatenate([ck] * ts, axis=0))
        if mask is not None:
            s = jnp.where(mask, s, NEG)
        m, l, acc = _softmax_step(s, vs, m_ref[...], l_ref[...], acc_ref[...])
        m_ref[...] = m
        l_ref[...] = l
        acc_ref[...] = acc

    step([r[...].astype(BF16) for r in k_refs], [r[...].astype(BF16) for r in v_refs], ck_ref[...], None)

    @pl.when(p == pl.num_programs(1) - 1)
    def _():
        tok = lax.broadcasted_iota(I32, (rows, LANES), 0) // FOX_HEADS
        col = lax.broadcasted_iota(I32, (rows, LANES), 1)
        step([kn_ref[...]], [vn_ref[...]], ckn_ref[...], col <= tok)
        o_ref[...] = acc_ref[...] / l_ref[...]


def _fox_sample(qbd, cq, ckT, cache_k, cache_v, page_table, knew, vnew, cknT, ts, group=4):
    B, n_pages = page_table.shape
    R = cache_k.shape[1]
    rows = ts * FOX_HEADS
    G = _page_group(n_pages, group)
    per_b = lambda b, p, pt: (b, 0, 0)
    grid_spec = pltpu.PrefetchScalarGridSpec(
        num_scalar_prefetch=1, grid=(B, n_pages // G),
        in_specs=[pl.BlockSpec((None, rows, FOX_W), per_b),
                  pl.BlockSpec((None, rows, 1), per_b),
                  pl.BlockSpec((None, FOX_HEADS, G * R), lambda b, p, pt: (b, 0, p))]
        + _paged_specs(G, n_pages, (None, R, FOX_W)) + _paged_specs(G, n_pages, (None, R, FOX_W))
        + [pl.BlockSpec((None, LANES, FOX_W), per_b),
           pl.BlockSpec((None, LANES, FOX_W), per_b),
           pl.BlockSpec((None, FOX_HEADS, LANES), per_b)],
        out_specs=pl.BlockSpec((None, rows, FOX_W), per_b),
        scratch_shapes=[pltpu.VMEM((rows, 1), F32), pltpu.VMEM((rows, 1), F32),
                        pltpu.VMEM((rows, FOX_W), F32)])
    return pl.pallas_call(
        functools.partial(_fox_sample_body, ts=ts, G=G), grid_spec=grid_spec,
        out_shape=jax.ShapeDtypeStruct((B, rows, FOX_W), F32),
        compiler_params=_cp(("parallel", "arbitrary")), name="fox_sample",
    )(page_table.reshape(-1).astype(I32), qbd, cq, ckT, *([cache_k] * G), *([cache_v] * G), knew, vnew, cknT)


def _idx_sample_body(pt_ref, q_ref, w_ref, *refs, ts, n_sel, G):
    k_refs = refs[:G]
    kn_ref, keys_ref, thr_ref = refs[G:]
    p = pl.program_id(1)
    n_pages = keys_ref.shape[0] - 1
    pad = jnp.full((8 - ts, LANES), INT_MIN, I32)

    def score(k):
        r = jnp.maximum(_nt_dot(q_ref[...], k), 0.0) * w_ref[...]
        return jnp.sum(r.reshape(ts, IDX_HEADS, LANES), axis=1)

    for g in range(G):
        keys_ref[p * G + g] = jnp.concatenate([_float_key(score(k_refs[g][...].astype(BF16))), pad], axis=0)

    @pl.when(p == pl.num_programs(1) - 1)
    def _():
        tok = lax.broadcasted_iota(I32, (ts, LANES), 0)
        col = lax.broadcasted_iota(I32, (ts, LANES), 1)
        kn = jnp.where(col <= tok, _float_key(score(kn_ref[...])), INT_MIN)
        keys_ref[n_pages] = jnp.concatenate([kn, pad], axis=0)
        thr_ref[...] = _kth_key(keys_ref, n_pages + 1, n_sel, 0, 8)


def _idx_sample(q, w, cache_k, page_table, knew, ts, n_sel, group=16):
    B, n_pages = page_table.shape
    R = cache_k.shape[1]
    rows = ts * IDX_HEADS
    G = _page_group(n_pages, group)
    per_b = lambda b, p, pt: (b, 0, 0)
    grid_spec = pltpu.PrefetchScalarGridSpec(
        num_scalar_prefetch=1, grid=(B, n_pages // G),
        in_specs=[pl.BlockSpec((None, rows, IDX_DIM), per_b),
                  pl.BlockSpec((None, rows, 1), per_b)]
        + _paged_specs(G, n_pages, (None, R, IDX_DIM))
        + [pl.BlockSpec((None, LANES, IDX_DIM), per_b)],
        out_specs=[pl.BlockSpec((None, n_pages + 1, 8, LANES), lambda b, p, pt: (b, 0, 0, 0)),
                   pl.BlockSpec((None, 8, 1), per_b)])
    return pl.pallas_call(
        functools.partial(_idx_sample_body, ts=ts, n_sel=n_sel, G=G), grid_spec=grid_spec,
        out_shape=[jax.ShapeDtypeStruct((B, n_pages + 1, 8, LANES), I32),
                   jax.ShapeDtypeStruct((B, 8, 1), I32)],
        compiler_params=_cp(("parallel", "arbitrary")), name="idx_sample",
    )(page_table.reshape(-1).astype(I32), q, w, *([cache_k] * G), knew)


def _dsa_sample_body(pt_ref, q_ref, keys_ref, keysn_ref, thr_ref, *refs, ts, G):
    k_refs, v_refs = refs[:G], refs[G:2 * G]
    kn_ref, vn_ref, o_ref, m_ref, l_ref, acc_ref = refs[2 * G:]
    p = pl.program_id(1)

    @pl.when(p == 0)
    def _():
        _init_softmax_state(m_ref, l_ref, acc_ref)

    q = q_ref[...]
    thr = thr_ref[...]

    def step(ks, vs, keys):
        parts = []
        for kk in keys:
            sel = jnp.where(kk >= thr, 0.0, NEG)
            parts.append(jnp.concatenate(
                [jnp.broadcast_to(sel[t:t + 1, :], (DSA_HEADS, LANES)) for t in range(ts)], axis=0))
        s = jnp.concatenate([_nt_dot(q, k) for k in ks], axis=1) * ATT_SCALE + jnp.concatenate(parts, axis=1)
        m, l, acc = _softmax_step(s, vs, m_ref[...], l_ref[...], acc_ref[...])
        m_ref[...] = m
        l_ref[...] = l
        acc_ref[...] = acc

    step([r[...].astype(BF16) for r in k_refs], [r[...].astype(BF16) for r in v_refs],
         [keys_ref[g] for g in range(G)])

    @pl.when(p == pl.num_programs(1) - 1)
    def _():
        step([kn_ref[...]], [vn_ref[...]], [keysn_ref[...]])
        o_ref[...] = acc_ref[...] / l_ref[...]


def _dsa_sample(qbd, keys, thr, cache_k, cache_v, page_table, knew, vnew, ts, group=8):
    B, n_pages = page_table.shape
    R = cache_k.shape[1]
    rows = ts * DSA_HEADS
    G = _page_group(n_pages, group)
    per_b = lambda b, p, pt: (b, 0, 0)
    grid_spec = pltpu.PrefetchScalarGridSpec(
        num_scalar_prefetch=1, grid=(B, n_pages // G),
        in_specs=[pl.BlockSpec((None, rows, DSA_KW), per_b),
                  pl.BlockSpec((None, G, 8, LANES), lambda b, p, pt: (b, p, 0, 0)),
                  pl.BlockSpec((None, None, 8, LANES), lambda b, p, pt: (b, n_pages, 0, 0)),
                  pl.BlockSpec((None, 8, 1), per_b)]
        + _paged_specs(G, n_pages, (None, R, DSA_KW)) + _paged_specs(G, n_pages, (None, R, DSA_KW))
        + [pl.BlockSpec((None, LANES, DSA_KW), per_b),
           pl.BlockSpec((None, LANES, DSA_KW), per_b)],
        out_specs=pl.BlockSpec((None, rows, DSA_KW), per_b),
        scratch_shapes=[pltpu.VMEM((rows, 1), F32), pltpu.VMEM((rows, 1), F32),
                        pltpu.VMEM((rows, DSA_KW), F32)])
    return pl.pallas_call(
        functools.partial(_dsa_sample_body, ts=ts, G=G), grid_spec=grid_spec,
        out_shape=jax.ShapeDtypeStruct((B, rows, DSA_KW), F32),
        compiler_params=_cp(("parallel", "arbitrary")), name="dsa_sample",
    )(page_table.reshape(-1).astype(I32), qbd, keys, keys, thr, *([cache_k] * G), *([cache_v] * G), knew, vnew)


def _top_rows(x, n):
    out = []
    for _ in range(n):
        m = jnp.max(x, axis=0, keepdims=True)
        out.append(m)
        x = jnp.where(x == m, -jnp.inf, x)
    return out


def _peer_select_body(q_ref, keys_ref, s0_ref, s1_ref, st_ref, *, n_heads):
    nk = keys_ref.shape[2]
    thr, off = [], []
    for h in range(n_heads):
        tops = []
        for c, s_ref in enumerate((s0_ref, s1_ref)):
            col = (2 * h + c) * nk
            st = _nt_dot(q_ref[:, col:col + nk].astype(BF16), keys_ref[h, c]).T
            s_ref[h * nk:(h + 1) * nk, :] = st
            tops.append(jnp.concatenate(_top_rows(st, PEER_TOPK), axis=0))
        cand = jnp.concatenate([tops[0][r:r + 1, :] + tops[1] for r in range(PEER_TOPK)], axis=0)
        best = _top_rows(cand, PEER_TOPK)
        z = jnp.ones_like(best[0])
        for r in range(1, PEER_TOPK):
            z = z + jnp.exp(best[r] - best[0])
        thr.append(best[-1])
        off.append(-(best[0] + jnp.log(z)))
    st_ref[...] = jnp.concatenate(thr + off, axis=0)


def _peer_select(q, keys_bf, tm=128):
    M = q.shape[0]
    n_heads, _, nk, _ = keys_bf.shape
    tm = _tile(M, tm)
    col = lambda i: (0, i)
    return pl.pallas_call(
        functools.partial(_peer_select_body, n_heads=n_heads),
        grid=(M // tm,),
        in_specs=[pl.BlockSpec((tm, q.shape[1]), lambda i: (i, 0)),
                  pl.BlockSpec(keys_bf.shape, lambda i: (0, 0, 0, 0))],
        out_specs=[pl.BlockSpec((n_heads * nk, tm), col), pl.BlockSpec((n_heads * nk, tm), col),
                   pl.BlockSpec((2 * n_heads, tm), col)],
        out_shape=[jax.ShapeDtypeStruct((n_heads * nk, M), F32), jax.ShapeDtypeStruct((n_heads * nk, M), F32),
                   jax.ShapeDtypeStruct((2 * n_heads, M), F32)],
        compiler_params=_cp(("parallel",)), name="peer_select",
    )(q, keys_bf)


def _peer_expert_body(x_ref, u_ref, v_ref, s0_ref, s1_ref, st_ref, o_ref, *, n_heads, nk, ni):
    e = pl.program_id(1)

    @pl.when(e == 0)
    def _():
        o_ref[...] = jnp.zeros_like(o_ref)

    a = _nt_dot(u_ref[...], x_ref[...])
    act = 0.5 * a * (1.0 + lax.erf(a * (2.0 ** -0.5)))
    gates = []
    for il in range(ni):
        w = None
        for h in range(n_heads):
            tot = s0_ref[il * n_heads + h:il * n_heads + h + 1, :] + s1_ref[h * nk:(h + 1) * nk, :]
            val = jnp.where(tot >= st_ref[h:h + 1, :], jnp.exp(tot + st_ref[n_heads + h:n_heads + h + 1, :]), 0.0)
            w = val if w is None else w + val
        gates.append(w)
    ga = (jnp.concatenate(gates, axis=0) * act).astype(BF16)
    o_ref[...] += lax.dot_general(ga, v_ref[...], (((0,), (0,)), ((), ())), preferred_element_type=F32)


def _peer_expert(hn_bf, u_bf, v_bf, s0t, s1t, st, n_heads, nk, tm=512, te=256):
    M, D = hn_bf.shape
    E = u_bf.shape[0]
    tm = _tile(M, tm)
    te = _tile(E, te)
    ni = te // nk
    s0g = jnp.transpose(s0t.reshape(n_heads, E // te, ni, M), (1, 2, 0, 3)).reshape(E // te, ni * n_heads, M)
    return pl.pallas_call(
        functools.partial(_peer_expert_body, n_heads=n_heads, nk=nk, ni=ni),
        grid=(M // tm, E // te),
        in_specs=[pl.BlockSpec((tm, D), lambda i, e: (i, 0)),
                  pl.BlockSpec((te, D), lambda i, e: (e, 0)),
                  pl.BlockSpec((te, D), lambda i, e: (e, 0)),
                  pl.BlockSpec((None, ni * n_heads, tm), lambda i, e: (e, 0, i)),
                  pl.BlockSpec((n_heads * nk, tm), lambda i, e: (0, i)),
                  pl.BlockSpec((2 * n_heads, tm), lambda i, e: (0, i))],
        out_specs=pl.BlockSpec((tm, D), lambda i, e: (i, 0)),
        out_shape=jax.ShapeDtypeStruct((M, D), F32),
        compiler_params=_cp(("parallel", "arbitrary")), name="peer_expert",
    )(hn_bf, u_bf, v_bf, s0g, s1t, st)


def _prep_w_in(w_in):
    o = np.cumsum([0, FOX_W, FOX_W, FOX_W, FOX_HEADS, DSA_QW, DSA_KW, DSA_KW, IDX_QW, IDX_DIM, IDX_HEADS])
    seg = lambda i: w_in[:, o[i]:o[i + 1]]
    cols = [seg(0), seg(1), seg(2), seg(4), seg(7), seg(5), seg(6), seg(8), seg(9), seg(3)]
    used = int(o[-1])
    total = -(-used // 768) * 768
    cols.append(jnp.zeros((w_in.shape[0], total - used), w_in.dtype))
    return jnp.concatenate(cols, axis=1).astype(BF16)


def _channel(x, mix_bf, p_l, w_o_bf, g_norm2, w_pq_bf, keys_bf, u_bf, v_bf, g_norm3, w_gate_bf, w_proj_bf,
             g_ple):
    n_heads, _, nk, _ = keys_bf.shape
    h1 = _mm(mix_bf, w_o_bf, res=x, epi="res")
    q, hn_bf = _mm(h1, w_pq_bf, g=g_norm2, emit_xn=True)
    s0t, s1t, st = _peer_select(q, keys_bf)
    moe = _peer_expert(hn_bf, u_bf, v_bf, s0t, s1t, st, n_heads, nk)
    pn = _rownorm_mm(p_l, w_proj_bf, g_ple)
    return _mm(h1, w_gate_bf, g=g_norm3, x2=moe, res=h1, res2=moe, aux=pn, epi="gate", tm=256)


def _diag_heads(o, ts, n_heads, group):
    B = o.shape[0]
    o5 = o.reshape(B, ts, n_heads, -1, HEAD_DIM)
    idx = (jnp.arange(n_heads) // group).reshape(1, 1, n_heads, 1, 1)
    return jnp.take_along_axis(o5, idx, axis=3).reshape(B * ts, n_heads * HEAD_DIM)


def _block_diag_q(q, ts, n_heads, group):
    B = q.shape[0] // ts
    n_kv = n_heads // group
    q4 = q.reshape(B, ts, n_heads, 1, HEAD_DIM)
    onehot = (jnp.arange(n_heads)[:, None] // group == jnp.arange(n_kv)[None, :])
    return jnp.where(onehot[None, None, :, :, None], q4, jnp.zeros((), q.dtype)).reshape(
        B, ts * n_heads, n_kv * HEAD_DIM)


def _pad_rows(x, n):
    return jnp.pad(x, ((0, 0), (0, n - x.shape[1]), (0, 0)))


def kernel(x_prompt, x_sample, cache_fox_k, cache_fox_v, cache_fox_logf, cache_dsa_k, cache_dsa_v, cache_idx_k, page_table, p_prompt, p_sample, g_norm1, w_in, b_f, g_q_fox, g_k_fox, g_q_dsa, g_k_dsa, w_o, g_norm2, w_peer_q, peer_keys, peer_u, peer_v, g_norm3, w_ple_gate, w_ple_proj, g_ple):
    Bp, Tp, D = x_prompt.shape
    Bs, Ts = x_sample.shape[:2]
    depth = w_in.shape[0]
    n_pages = page_table.shape[1]
    page = cache_fox_k.shape[2]
    past = n_pages * page
    n_phys = cache_fox_k.shape[1]
    assert Bp == 1 and Ts <= 8 and page == LANES

    def pool(cache):
        return cache.reshape((depth * n_phys, page, -1))

    fox_k_pool, fox_v_pool, fox_lf_pool = pool(cache_fox_k), pool(cache_fox_v), pool(cache_fox_logf)
    dsa_k_pool, dsa_v_pool, idx_k_pool = pool(cache_dsa_k), pool(cache_dsa_v), pool(cache_idx_k)

    h_p = x_prompt.reshape(Tp, D)
    h_s = x_sample.reshape(Bs * Ts, D)
    pos_p = jnp.arange(Tp, dtype=I32)
    pos_s = jnp.tile(past + jnp.arange(Ts, dtype=I32), Bs)
    outs = [[] for _ in range(12)]
    for l in range(depth):
        w_in_bf = _prep_w_in(w_in[l])
        chan_w = (w_o[l].astype(BF16), g_norm2[l], w_peer_q[l].astype(BF16), peer_keys[l].astype(BF16),
                  peer_u[l].astype(BF16), peer_v[l].astype(BF16), g_norm3[l], w_ple_gate[l].astype(BF16),
                  w_ple_proj[l].astype(BF16), g_ple[l])
        norm_w = (b_f[l], g_q_fox[l], g_k_fox[l], g_q_dsa[l], g_k_dsa[l])
        pt_l = page_table + l * n_phys

        z = _mm(h_p, w_in_bf, g=g_norm1[l], tn=768)
        pr = _post(z, pos_p, *norm_w)
        ident = jnp.arange(Tp // page, dtype=I32).reshape(1, -1)
        c, _ = _paged_cumsum(pr["lf"].reshape(Tp // page, page, FOX_HEADS), ident,
                             jnp.zeros((1, 8, FOX_HEADS), F32))
        o_fox = _fox_prompt(pr["qf"], pr["kfb"], pr["vfb"], c[0])
        keys, thr, tq = _idx_prompt(pr["qi"], pr["wi"], pr["kia"], pr["kib"], min(TOPK_MAX, Tp // 4))
        o_dsa = _dsa_prompt(pr["qd"], pr["kdb"], pr["vdb"], keys, thr, tq)
        mix = jnp.concatenate([o_fox, o_dsa], axis=-1)
        h_p = _channel(h_p, mix, p_prompt[l].reshape(Tp, -1), *chan_w)
        for i, (name, shape) in enumerate((("kf", (Bp, Tp, FOX_HEADS, HEAD_DIM)), ("vf", (Bp, Tp, FOX_HEADS, HEAD_DIM)),
                                           ("lf", (Bp, Tp, FOX_HEADS)), ("kd", (Bp, Tp, DSA_KV_HEADS, HEAD_DIM)),
                                           ("vd", (Bp, Tp, DSA_KV_HEADS, HEAD_DIM)), ("ki", (Bp, Tp, IDX_DIM)))):
            outs[i].append(pr[name].reshape(shape))

        zs = _mm(h_s, w_in_bf, g=g_norm1[l], tn=768)
        sr = _post(zs, pos_s, *norm_w)
        lf_new = _pad_rows(sr["lf"].reshape(Bs, Ts, FOX_HEADS), 8)
        c_past, c_new = _paged_cumsum(fox_lf_pool, pt_l, lf_new)
        qbd = _block_diag_q(sr["qf"], Ts, FOX_HEADS, 1)
        cq = c_new[:, :Ts, :].reshape(Bs, Ts * FOX_HEADS, 1)
        ckT = jnp.transpose(c_past, (0, 2, 1))
        cknT = jnp.transpose(_pad_rows(c_new, LANES), (0, 2, 1))
        o = _fox_sample(qbd, cq, ckT, fox_k_pool, fox_v_pool, pt_l,
                        _pad_rows(sr["kfb"].reshape(Bs, Ts, FOX_W), LANES),
                        _pad_rows(sr["vfb"].reshape(Bs, Ts, FOX_W), LANES), cknT, Ts)
        o_fox_s = _diag_heads(o, Ts, FOX_HEADS, 1)
        n_sel = min(TOPK_MAX, (past + Ts) // 4)
        keys_s, thr_s = _idx_sample(sr["qi"].reshape(Bs, Ts * IDX_HEADS, IDX_DIM),
                                    sr["wi"].reshape(Bs, Ts * IDX_HEADS, 1), idx_k_pool, pt_l,
                                    _pad_rows(sr["kia"][:, :IDX_DIM].reshape(Bs, Ts, IDX_DIM), LANES), Ts, n_sel)
        od = _dsa_sample(_block_diag_q(sr["qd"], Ts, DSA_HEADS, DSA_GROUP), keys_s, thr_s,
                         dsa_k_pool, dsa_v_pool, pt_l, _pad_rows(sr["kdb"].reshape(Bs, Ts, DSA_KW), LANES),
                         _pad_rows(sr["vdb"].reshape(Bs, Ts, DSA_KW), LANES), Ts)
        o_dsa_s = _diag_heads(od, Ts, DSA_HEADS, DSA_GROUP)
        mix_s = jnp.concatenate([o_fox_s, o_dsa_s], axis=-1).astype(BF16)
        h_s = _channel(h_s, mix_s, p_sample[l].reshape(Bs * Ts, -1), *chan_w)
        for i, (name, shape) in enumerate((("kf", (Bs, Ts, FOX_HEADS, HEAD_DIM)), ("vf", (Bs, Ts, FOX_HEADS, HEAD_DIM)),
                                           ("lf", (Bs, Ts, FOX_HEADS)), ("kd", (Bs, Ts, DSA_KV_HEADS, HEAD_DIM)),
                                           ("vd", (Bs, Ts, DSA_KV_HEADS, HEAD_DIM)), ("ki", (Bs, Ts, IDX_DIM)))):
            outs[6 + i].append(sr[name].reshape(shape))
    return (h_p.reshape(Bp, Tp, D), h_s.reshape(Bs, Ts, D)) + tuple(jnp.stack(o) for o in outs)
```

```python
import functools
import math

import numpy as np
import jax
import jax.numpy as jnp
from jax import lax
from jax.experimental import pallas as pl
from jax.experimental.pallas import tpu as pltpu

F32 = jnp.float32
BF16 = jnp.bfloat16
I32 = jnp.int32

HEAD_DIM = 128
FOX_HEADS = 16
DSA_HEADS = 16
DSA_KV_HEADS = 4
DSA_GROUP = DSA_HEADS // DSA_KV_HEADS
IDX_HEADS = 32
IDX_DIM = 64
TOPK_MAX = 256
ROPE_THETA = 10000.0
PEER_TOPK = 16
EPS = 1e-6

FOX_W = FOX_HEADS * HEAD_DIM
DSA_QW = DSA_HEADS * HEAD_DIM
DSA_KW = DSA_KV_HEADS * HEAD_DIM
IDX_QW = IDX_HEADS * IDX_DIM

LANES = 128
NEG = -1e30
INT_MIN = -(2 ** 31)
VMEM_LIMIT = 56 * 1024 * 1024
ATT_SCALE = HEAD_DIM ** -0.5
LOG2E = math.log2(math.e)
QS = 128
KS = 256
WIDTH = 4


def _cp(sem):
    return pltpu.CompilerParams(dimension_semantics=sem, vmem_limit_bytes=VMEM_LIMIT)


def _tile(n, pref):
    t = min(n, pref)
    while n % t:
        t //= 2
    return t


def _nt_dot(a, b):
    return lax.dot_general(a, b, (((1,), (1,)), ((), ())), preferred_element_type=F32)


def _mm_body(*refs, norm, add2, epi, emit_xn, prologue):
    it = iter(refs)
    x_ref = next(it)
    x2_ref = next(it) if add2 else None
    g_ref = next(it) if norm else None
    w_ref = next(it)
    res_ref = next(it) if epi in ("res", "gate") else None
    res2_ref = next(it) if (epi == "gate" and add2) else None
    aux_ref = next(it) if epi == "gate" else None
    o_ref = next(it)
    xo_ref = next(it) if emit_xn else None
    xn_ref = next(it) if prologue else None

    if prologue:
        @pl.when(pl.program_id(1) == 0)
        def _():
            x = x_ref[...].astype(F32)
            if add2:
                x = x + x2_ref[...]
            if norm:
                ms = jnp.mean(x * x, axis=-1, keepdims=True)
                x = x * lax.rsqrt(ms + EPS) * g_ref[...]
            xb = x.astype(BF16)
            xn_ref[...] = xb
            if emit_xn:
                xo_ref[...] = xb
        lhs = xn_ref[...]
    else:
        lhs = x_ref[...]
    acc = jnp.dot(lhs, w_ref[...], preferred_element_type=F32)
    if epi == "res":
        acc = res_ref[...] + acc
    elif epi == "gate":
        r = res_ref[...]
        if add2:
            r = r + res2_ref[...]
        acc = r + jax.nn.sigmoid(acc) * aux_ref[...]
    o_ref[...] = acc


def _mm(x, w, *, g=None, x2=None, res=None, res2=None, aux=None, epi="none", emit_xn=False,
        tm=512, tn=512):
    M, K = x.shape
    N = w.shape[1]
    tm = _tile(M, tm)
    tn = _tile(N, tn)
    norm = g is not None
    add2 = x2 is not None
    prologue = norm or add2 or x.dtype != BF16
    row = pl.BlockSpec((tm, K), lambda i, j: (i, 0))
    blk = pl.BlockSpec((tm, tn), lambda i, j: (i, j))
    ins, specs = [x], [row]
    if add2:
        ins.append(x2); specs.append(row)
    if norm:
        ins.append(g.reshape(1, K)); specs.append(pl.BlockSpec((1, K), lambda i, j: (0, 0)))
    ins.append(w); specs.append(pl.BlockSpec((K, tn), lambda i, j: (0, j)))
    if epi in ("res", "gate"):
        ins.append(res); specs.append(blk)
    if epi == "gate" and add2:
        ins.append(res2); specs.append(blk)
    if epi == "gate":
        ins.append(aux); specs.append(blk)
    out_shape = [jax.ShapeDtypeStruct((M, N), F32)]
    out_specs = [blk]
    if emit_xn:
        out_shape.append(jax.ShapeDtypeStruct((M, K), BF16))
        out_specs.append(row)
    scratch = [pltpu.VMEM((tm, K), BF16)] if prologue else []
    outs = pl.pallas_call(
        functools.partial(_mm_body, norm=norm, add2=add2, epi=epi, emit_xn=emit_xn, prologue=prologue),
        grid=(M // tm, N // tn),
        in_specs=specs, out_specs=out_specs, out_shape=out_shape, scratch_shapes=scratch,
        compiler_params=_cp(("parallel", "arbitrary")), name="mm_" + epi,
    )(*ins)
    return outs if emit_xn else outs[0]


def _rownorm_mm_body(x_ref, w_ref, g_ref, o_ref):
    y = jnp.dot(x_ref[...].astype(BF16), w_ref[...], preferred_element_type=F32)
    ms = jnp.mean(y * y, axis=-1, keepdims=True)
    o_ref[...] = y * lax.rsqrt(ms + EPS) * g_ref[...]


def _rownorm_mm(x, w, g, tm=256):
    M, K = x.shape
    N = w.shape[1]
    tm = _tile(M, tm)
    return pl.pallas_call(
        _rownorm_mm_body,
        grid=(M // tm,),
        in_specs=[pl.BlockSpec((tm, K), lambda i: (i, 0)),
                  pl.BlockSpec((K, N), lambda i: (0, 0)),
                  pl.BlockSpec((1, N), lambda i: (0, 0))],
        out_specs=pl.BlockSpec((tm, N), lambda i: (i, 0)),
        out_shape=jax.ShapeDtypeStruct((M, N), F32),
        compiler_params=_cp(("parallel",)), name="rownorm_mm",
    )(x, w, g.reshape(1, N))


def _head_norm(x, g):
    ms = jnp.mean(x * x, axis=-1, keepdims=True)
    return x * lax.rsqrt(ms + EPS) * g


def _rope_full(x, cos, sin):
    return x * cos + pltpu.roll(x, HEAD_DIM // 2, 1) * sin


def _rope_pair(x, cos, sin, lane):
    half = IDX_DIM // 2
    rot = jnp.where((lane % IDX_DIM) < half, pltpu.roll(x, LANES - half, 1), pltpu.roll(x, half, 1))
    return x * cos + rot * sin


def _post_body(zqf, zkf, zvf, zqd, zqi, zkv, zs, cd, sd, ci, si, bf, gqf, gkf, gqd, gkd,
               qf_o, kf_o, kfb_o, vf_o, vfb_o, lf_o, qd_o, kd_o, kdb_o, vd_o, vdb_o,
               qi_o, ki_o, kia_o, kib_o, wi_o):
    tm = zqf.shape[0]
    cos_d, sin_d = cd[...], sd[...]
    cos_i, sin_i = ci[...], si[...]
    lane = lax.broadcasted_iota(I32, (tm, LANES), 1)
    for h in range(FOX_HEADS):
        sl = slice(h * HEAD_DIM, (h + 1) * HEAD_DIM)
        qf_o[:, sl] = _head_norm(zqf[:, sl], gqf[...]).astype(BF16)
        k = _head_norm(zkf[:, sl], gkf[...])
        kf_o[:, sl] = k
        kfb_o[:, sl] = k.astype(BF16)
        v = zvf[:, sl]
        vf_o[:, sl] = v
        vfb_o[:, sl] = v.astype(BF16)
    for h in range(DSA_HEADS):
        sl = slice(h * HEAD_DIM, (h + 1) * HEAD_DIM)
        qd_o[:, sl] = _rope_full(_head_norm(zqd[:, sl], gqd[...]), cos_d, sin_d).astype(BF16)
    for h in range(DSA_KV_HEADS):
        sl = slice(h * HEAD_DIM, (h + 1) * HEAD_DIM)
        k = _rope_full(_head_norm(zkv[:, sl], gkd[...]), cos_d, sin_d)
        kd_o[:, sl] = k
        kdb_o[:, sl] = k.astype(BF16)
        v = zkv[:, DSA_KW + h * HEAD_DIM: DSA_KW + (h + 1) * HEAD_DIM]
        vd_o[:, sl] = v
        vdb_o[:, sl] = v.astype(BF16)
    for j in range(IDX_QW // LANES):
        sl = slice(j * LANES, (j + 1) * LANES)
        qi_o[:, sl] = _rope_pair(zqi[:, sl], cos_i, sin_i, lane).astype(BF16)
    x = zs[...]
    y = _rope_pair(x, cos_i, sin_i, lane)
    ki_o[...] = y[:, :IDX_DIM]
    ka = jnp.where(lane < IDX_DIM, y, 0.0)
    kia_o[...] = ka.astype(BF16)
    kib_o[...] = pltpu.roll(ka, IDX_DIM, 1).astype(BF16)
    wi_o[...] = x[:, IDX_DIM:IDX_DIM + IDX_HEADS] * (IDX_HEADS ** -0.5 * IDX_DIM ** -0.5)
    fl = x[:, IDX_DIM + IDX_HEADS:IDX_DIM + IDX_HEADS + FOX_HEADS] + bf[...]
    lf_o[...] = jnp.minimum(fl, 0.0) - jnp.log1p(jnp.exp(-jnp.abs(fl)))


def _post(z, pos, b_f, g_q_fox, g_k_fox, g_q_dsa, g_k_dsa, tm=128):
    M = z.shape[0]
    tm = _tile(M, tm)
    posf = pos.astype(F32)[:, None]
    hd = HEAD_DIM // 2
    ang = posf * (ROPE_THETA ** (-jnp.arange(hd, dtype=F32) * 2.0 / HEAD_DIM))
    cd = jnp.concatenate([jnp.cos(ang), jnp.cos(ang)], axis=-1)
    sd = jnp.concatenate([-jnp.sin(ang), jnp.sin(ang)], axis=-1)
    hi = IDX_DIM // 2
    angi = posf * (ROPE_THETA ** (-jnp.arange(hi, dtype=F32) * 2.0 / IDX_DIM))
    ci = jnp.tile(jnp.cos(angi), (1, 4))
    si = jnp.tile(jnp.concatenate([-jnp.sin(angi), jnp.sin(angi)], axis=-1), (1, 2))

    def zspec(w, c):
        return pl.BlockSpec((tm, w), lambda i: (i, c))

    def rspec(w):
        return pl.BlockSpec((tm, w), lambda i: (i, 0))

    def cspec(w):
        return pl.BlockSpec((1, w), lambda i: (0, 0))

    small_col = (5 * FOX_W + 2 * DSA_KW) // LANES
    in_specs = [zspec(FOX_W, 0), zspec(FOX_W, 1), zspec(FOX_W, 2), zspec(DSA_QW, 3), zspec(IDX_QW, 4),
                zspec(2 * DSA_KW, 5 * FOX_W // (2 * DSA_KW)), zspec(LANES, small_col),
                rspec(LANES), rspec(LANES), rspec(LANES), rspec(LANES),
                cspec(FOX_HEADS), cspec(HEAD_DIM), cspec(HEAD_DIM), cspec(HEAD_DIM), cspec(HEAD_DIM)]
    outs = [(FOX_W, BF16), (FOX_W, F32), (FOX_W, BF16), (FOX_W, F32), (FOX_W, BF16), (FOX_HEADS, F32),
            (DSA_QW, BF16), (DSA_KW, F32), (DSA_KW, BF16), (DSA_KW, F32), (DSA_KW, BF16),
            (IDX_QW, BF16), (IDX_DIM, F32), (LANES, BF16), (LANES, BF16), (IDX_HEADS, F32)]
    res = pl.pallas_call(
        _post_body,
        grid=(M // tm,),
        in_specs=in_specs,
        out_specs=[rspec(w) for w, _ in outs],
        out_shape=[jax.ShapeDtypeStruct((M, w), d) for w, d in outs],
        compiler_params=_cp(("parallel",)), name="post_proj",
    )(z, z, z, z, z, z, z, cd, sd, ci, si, b_f.reshape(1, -1), g_q_fox.reshape(1, -1),
      g_k_fox.reshape(1, -1), g_q_dsa.reshape(1, -1), g_k_dsa.reshape(1, -1))
    names = ("qf", "kf", "kfb", "vf", "vfb", "lf", "qd", "kd", "kdb", "vd", "vdb", "qi", "ki", "kia",
             "kib", "wi")
    return dict(zip(names, res))


def _split3(x):
    hi = x.astype(BF16)
    r1 = x - hi.astype(F32)
    mid = r1.astype(BF16)
    lo = (r1 - mid.astype(F32)).astype(BF16)
    return hi, mid, lo


def _tri_ones(n):
    r = lax.broadcasted_iota(I32, (n, n), 0)
    c = lax.broadcasted_iota(I32, (n, n), 1)
    return jnp.where(c <= r, 1.0, 0.0).astype(BF16)


def _tri_cumsum(tri, x):
    hi, mid, lo = _split3(x)
    dot = lambda a: jnp.dot(tri, a, preferred_element_type=F32)
    return dot(hi) + (dot(mid) + dot(lo))


def _page_group(n_pages, pref):
    g = min(n_pages, pref)
    while n_pages % g:
        g -= 1
    return g


def _paged_specs(G, n_pages, block):
    def spec(g):
        return pl.BlockSpec(block, lambda b, p, pt: (pt[b * n_pages + p * G + g], 0, 0))
    return [spec(g) for g in range(G)]


def _cumsum_body(pt_ref, *refs, G):
    lf_refs = refs[:G]
    new_ref, c_ref, cn_ref, carry_ref = refs[G:]
    p = pl.program_id(1)
    R = lf_refs[0].shape[0]

    @pl.when(p == 0)
    def _():
        carry_ref[...] = jnp.zeros_like(carry_ref)

    tri = _tri_ones(R)
    carry = carry_ref[...]
    for g in range(G):
        c = _tri_cumsum(tri, lf_refs[g][...]) + carry
        c_ref[g * R:(g + 1) * R, :] = c
        carry = c[-1:, :]
    carry_ref[...] = carry

    @pl.when(p == pl.num_programs(1) - 1)
    def _():
        nr = new_ref.shape[0]
        cn_ref[...] = _tri_cumsum(_tri_ones(nr), new_ref[...]) + carry


def _paged_cumsum(pool, page_table, new_rows, group=16):
    B, n_pages = page_table.shape
    R, H = pool.shape[1:]
    nr = new_rows.shape[1]
    G = _page_group(n_pages, group)
    grid_spec = pltpu.PrefetchScalarGridSpec(
        num_scalar_prefetch=1, grid=(B, n_pages // G),
        in_specs=_paged_specs(G, n_pages, (None, R, H)) + [
            pl.BlockSpec((None, nr, H), lambda b, p, pt: (b, 0, 0))],
        out_specs=[pl.BlockSpec((None, G * R, H), lambda b, p, pt: (b, p, 0)),
                   pl.BlockSpec((None, nr, H), lambda b, p, pt: (b, 0, 0))],
        scratch_shapes=[pltpu.VMEM((1, H), F32)])
    c, cn = pl.pallas_call(
        functools.partial(_cumsum_body, G=G), grid_spec=grid_spec,
        out_shape=[jax.ShapeDtypeStruct((B, n_pages * R, H), F32),
                   jax.ShapeDtypeStruct((B, nr, H), F32)],
        compiler_params=_cp(("parallel", "arbitrary")), name="paged_cumsum",
    )(page_table.reshape(-1).astype(I32), *([pool] * G), new_rows)
    return c, cn


def _softmax_step(s, vs, m_prev, l_prev, acc_prev):
    m_new = jnp.maximum(m_prev, jnp.max(s, axis=-1, keepdims=True))
    alpha = jnp.exp(m_prev - m_new)
    p = jnp.exp(s - m_new)
    l_new = alpha * l_prev + jnp.sum(p, axis=-1, keepdims=True)
    pb = p.astype(BF16)
    pv = None
    for g, v in enumerate(vs):
        d = jnp.dot(pb[:, g * LANES:(g + 1) * LANES], v, preferred_element_type=F32)
        pv = d if pv is None else pv + d
    return m_new, l_new, alpha * acc_prev + pv


def _causal_pairs(nq):
    qs, ks = [], []
    for qi in range(nq):
        for ki in range(qi + 1):
            qs.append(qi); ks.append(ki)
    return np.asarray(qs, np.int32), np.asarray(ks, np.int32)


def _init_softmax_state(m_ref, l_ref, acc_ref):
    m_ref[...] = jnp.full_like(m_ref, NEG)
    l_ref[...] = jnp.zeros_like(l_ref)
    acc_ref[...] = jnp.zeros_like(acc_ref)


def _tile_plan(t, diag):
    plan = []
    for r in range(t // QS):
        for c in range(t // KS):
            if diag and c * KS > r * QS + QS - 1:
                continue
            plan.append((r, c, diag and (c + 1) * KS - 1 > r * QS))
    return plan


def _attn_tiles_t(ks, qs, vts, ms, ls, accs, *, biases=None, masks=None, key_sub=None, qry_add=None):
    n = len(ks)
    ts = [_nt_dot(ks[i], qs[i]) * (ATT_SCALE * LOG2E) for i in range(n)]
    if key_sub is not None:
        ts = [ts[i] - key_sub for i in range(n)]
    if biases is not None:
        ts = [ts[i] + biases[i] for i in range(n)]
    if masks is not None:
        ts = [ts[i] if masks[i] is None else jnp.where(masks[i], ts[i], NEG) for i in range(n)]
    mx = [jnp.max(ts[i], axis=0, keepdims=True) for i in range(n)]
    if qry_add is not None:
        mx = [mx[i] + qry_add[i] for i in range(n)]
    m_new = [jnp.maximum(ms[i], mx[i]) for i in range(n)]
    shift = [-m_new[i] if qry_add is None else qry_add[i] - m_new[i] for i in range(n)]
    ps = [jnp.exp2(ts[i] + shift[i]) for i in range(n)]
    alphas = [jnp.exp2(ms[i] - m_new[i]) for i in range(n)]
    l_new = [alphas[i] * ls[i] + jnp.sum(ps[i], axis=0, keepdims=True) for i in range(n)]
    pv = [jnp.dot(vts[i], ps[i].astype(BF16), preferred_element_type=F32) for i in range(n)]
    acc_new = [alphas[i] * accs[i] + pv[i] for i in range(n)]
    return m_new, l_new, acc_new


def _tile_causal_t(r, c):
    key = lax.broadcasted_iota(I32, (KS, QS), 0)
    qry = lax.broadcasted_iota(I32, (KS, QS), 1)
    return key + (c * KS - r * QS) <= qry


def _fox_prompt_body(qi_ref, ki_ref, q_ref, k_ref, vt_ref, cq_ref, ck_ref, o_ref, m_ref, l_ref, acc_ref, *, t):
    p = pl.program_id(0)
    qi, ki = qi_ref[p], ki_ref[p]

    @pl.when(ki == 0)
    def _():
        _init_softmax_state(m_ref, l_ref, acc_ref)

    def run(diag):
        plan = _tile_plan(t, diag)
        for h in range(FOX_HEADS):
            hd = slice(h * HEAD_DIM, (h + 1) * HEAD_DIM)
            qcs = [slice(r * QS, (r + 1) * QS) for r in range(t // QS)]
            qs = [q_ref[qc, hd] for qc in qcs]
            cq2 = [cq_ref[h:h + 1, qc] * LOG2E for qc in qcs]
            ms = [m_ref[h:h + 1, qc] for qc in qcs]
            ls = [l_ref[h:h + 1, qc] for qc in qcs]
            accs = [acc_ref[hd, qc] for qc in qcs]
            for c in range(t // KS):
                kc = slice(c * KS, (c + 1) * KS)
                act = [(r, msk) for r, cc, msk in plan if cc == c]
                ck2 = jnp.broadcast_to(ck_ref[kc, h:h + 1] * LOG2E, (KS, QS))
                for g0 in range(0, len(act), WIDTH):
                    grp = act[g0:g0 + WIDTH]
                    k = k_ref[kc, hd]
                    vt = vt_ref[hd, kc]
                    rs = [r for r, _ in grp]
                    mo, lo, ao = _attn_tiles_t(
                        [k] * len(grp), [qs[r] for r in rs], [vt] * len(grp),
                        [ms[r] for r in rs], [ls[r] for r in rs], [accs[r] for r in rs],
                        key_sub=ck2, qry_add=[cq2[r] for r in rs],
                        masks=[_tile_causal_t(r, c) if msk else None for r, msk in grp])
                    for i, r in enumerate(rs):
                        ms[r], ls[r], accs[r] = mo[i], lo[i], ao[i]
            for r, qc in enumerate(qcs):
                if diag:
                    o_ref[hd, qc] = (accs[r] / ls[r]).astype(BF16)
                else:
                    m_ref[h:h + 1, qc] = ms[r]
                    l_ref[h:h + 1, qc] = ls[r]
                    acc_ref[hd, qc] = accs[r]

    @pl.when(ki == qi)
    def _():
        run(True)

    @pl.when(ki != qi)
    def _():
        run(False)


def _fox_prompt(q, k, v, c, t=512):
    T = q.shape[0]
    t = max(_tile(T, t), KS)
    qs, ks = _causal_pairs(T // t)
    grid_spec = pltpu.PrefetchScalarGridSpec(
        num_scalar_prefetch=2, grid=(len(qs),),
        in_specs=[pl.BlockSpec((t, FOX_W), lambda p, qi, ki: (qi[p], 0)),
                  pl.BlockSpec((t, FOX_W), lambda p, qi, ki: (ki[p], 0)),
                  pl.BlockSpec((FOX_W, t), lambda p, qi, ki: (0, ki[p])),
                  pl.BlockSpec((FOX_HEADS, t), lambda p, qi, ki: (0, qi[p])),
                  pl.BlockSpec((t, FOX_HEADS), lambda p, qi, ki: (ki[p], 0))],
        out_specs=pl.BlockSpec((FOX_W, t), lambda p, qi, ki: (0, qi[p])),
        scratch_shapes=[pltpu.VMEM((FOX_HEADS, t), F32), pltpu.VMEM((FOX_HEADS, t), F32),
                        pltpu.VMEM((FOX_W, t), F32)])
    ot = pl.pallas_call(
        functools.partial(_fox_prompt_body, t=t), grid_spec=grid_spec,
        out_shape=jax.ShapeDtypeStruct((FOX_W, T), BF16),
        compiler_params=_cp(("arbitrary",)), name="fox_prompt",
    )(jnp.asarray(qs), jnp.asarray(ks), q, k, v.T, c.T, c)
    return ot.T


def _float_key(x):
    b = pltpu.bitcast(x, I32)
    return b ^ ((b >> 31) & jnp.int32(0x7FFFFFFF))


def _kth_key(keys_ref, nchunks, k, row0, rows):
    def count_ge(cand):
        def body(c, acc):
            kk = keys_ref[c, row0:row0 + rows, :]
            return acc + jnp.where(kk >= cand, 1.0, 0.0)
        acc = lax.fori_loop(0, nchunks, body, jnp.zeros((rows, LANES), F32))
        return jnp.sum(acc, axis=1, keepdims=True)

    def bit_body(i, t):
        cand = t + lax.shift_left(jnp.int32(1), 31 - i)
        return jnp.where(count_ge(cand) >= k, cand, t)

    return lax.fori_loop(0, 32, bit_body, jnp.full((rows, 1), INT_MIN, I32))


def _idx_prompt_body(q_ref, w_ref, ka_ref, kb_ref, keys_ref, thr_ref, *, tq, kc, n_sel, row_group):
    qi = pl.program_id(0)
    nck = keys_ref.shape[0]
    per = kc // LANES
    n_super = ((qi + 1) * tq + kc - 1) // kc

    def super_body(sc, carry):
        k0 = pl.multiple_of(sc * kc, kc)
        ka = ka_ref[pl.ds(k0, kc), :]
        kb = kb_ref[pl.ds(k0, kc), :]
        acc = jnp.zeros((tq, kc), F32)
        for j in range(IDX_QW // LANES):
            q2 = q_ref[:, j * LANES:(j + 1) * LANES]
            acc = acc + w_ref[:, 2 * j:2 * j + 1] * jnp.maximum(_nt_dot(q2, ka), 0.0)
            acc = acc + w_ref[:, 2 * j + 1:2 * j + 2] * jnp.maximum(_nt_dot(q2, kb), 0.0)
        qpos = qi * tq + lax.broadcasted_iota(I32, (tq, kc), 0)
        kpos = sc * kc + lax.broadcasted_iota(I32, (tq, kc), 1)
        key = jnp.where(kpos <= qpos, _float_key(acc), INT_MIN)
        for c in range(per):
            keys_ref[sc * per + c] = key[:, c * LANES:(c + 1) * LANES]
        return carry

    lax.fori_loop(0, n_super, super_body, 0)

    def fill_body(c, carry):
        keys_ref[c] = jnp.full((tq, LANES), INT_MIN, I32)
        return carry

    lax.fori_loop(n_super * per, nck, fill_body, 0)
    for rg in range(tq // row_group):
        thr_ref[rg * row_group:(rg + 1) * row_group, :] = _kth_key(
            keys_ref, n_super * per, n_sel, rg * row_group, row_group)


def _idx_prompt(qi_bf, wi, kia, kib, n_sel, tq=256, kc=512):
    T = qi_bf.shape[0]
    tq = _tile(T, tq)
    kc = _tile(T, kc)
    nck = T // LANES
    row_group = min(tq, 128)
    keys, thr = pl.pallas_call(
        functools.partial(_idx_prompt_body, tq=tq, kc=kc, n_sel=n_sel, row_group=row_group),
        grid=(T // tq,),
        in_specs=[pl.BlockSpec((tq, IDX_QW), lambda i: (i, 0)),
                  pl.BlockSpec((tq, IDX_HEADS), lambda i: (i, 0)),
                  pl.BlockSpec((T, LANES), lambda i: (0, 0)),
                  pl.BlockSpec((T, LANES), lambda i: (0, 0))],
        out_specs=[pl.BlockSpec((None, nck, tq, LANES), lambda i: (i, 0, 0, 0)),
                   pl.BlockSpec((tq, 1), lambda i: (i, 0))],
        out_shape=[jax.ShapeDtypeStruct((T // tq, nck, tq, LANES), I32),
                   jax.ShapeDtypeStruct((T, 1), I32)],
        compiler_params=_cp(("parallel",)), name="idx_prompt",
    )(qi_bf, wi, kia, kib)
    return keys, thr, tq


def _dsa_prompt_body(qi_ref, ki_ref, q_ref, k_ref, vt_ref, keys_ref, thr_ref, o_ref, m_ref, l_ref, acc_ref, *, t):
    p = pl.program_id(0)
    qi, ki = qi_ref[p], ki_ref[p]

    @pl.when(ki == 0)
    def _():
        _init_softmax_state(m_ref, l_ref, acc_ref)

    def run(diag):
        for r, c, msk in _tile_plan(t, diag):
            qc = slice(r * QS, (r + 1) * QS)
            kc = slice(c * KS, (c + 1) * KS)
            keys = jnp.concatenate([keys_ref[c * (KS // LANES) + j, qc, :] for j in range(KS // LANES)], axis=1)
            bias = jnp.where(keys >= thr_ref[qc, :], 0.0, NEG).T
            if msk:
                bias = jnp.where(_tile_causal_t(r, c), bias, NEG)
            for g in range(DSA_KV_HEADS):
                kv = slice(g * HEAD_DIM, (g + 1) * HEAD_DIM)
                hs = [g * DSA_GROUP + i for i in range(DSA_GROUP)]
                hds = [slice(h * HEAD_DIM, (h + 1) * HEAD_DIM) for h in hs]
                n = len(hs)
                mo, lo, ao = _attn_tiles_t(
                    [k_ref[kc, kv]] * n, [q_ref[qc, hd] for hd in hds], [vt_ref[kv, kc]] * n,
                    [m_ref[h:h + 1, qc] for h in hs], [l_ref[h:h + 1, qc] for h in hs],
                    [acc_ref[hd, qc] for hd in hds], biases=[bias] * n)
                for i, h in enumerate(hs):
                    m_ref[h:h + 1, qc] = mo[i]
                    l_ref[h:h + 1, qc] = lo[i]
                    acc_ref[hds[i], qc] = ao[i]

    @pl.when(ki == qi)
    def _():
        run(True)
        for h in range(DSA_HEADS):
            hd = slice(h * HEAD_DIM, (h + 1) * HEAD_DIM)
            o_ref[hd, :] = (acc_ref[hd, :] / l_ref[h:h + 1, :]).astype(BF16)

    @pl.when(ki != qi)
    def _():
        run(False)


def _dsa_prompt(q, k, v, keys, thr, t):
    T = q.shape[0]
    qs, ks = _causal_pairs(T // t)
    per = t // LANES
    grid_spec = pltpu.PrefetchScalarGridSpec(
        num_scalar_prefetch=2, grid=(len(qs),),
        in_specs=[pl.BlockSpec((t, DSA_QW), lambda p, qi, ki: (qi[p], 0)),
                  pl.BlockSpec((t, DSA_KW), lambda p, qi, ki: (ki[p], 0)),
                  pl.BlockSpec((DSA_KW, t), lambda p, qi, ki: (0, ki[p])),
                  pl.BlockSpec((None, per, t, LANES), lambda p, qi, ki: (qi[p], ki[p], 0, 0)),
                  pl.BlockSpec((t, 1), lambda p, qi, ki: (qi[p], 0))],
        out_specs=pl.BlockSpec((DSA_QW, t), lambda p, qi, ki: (0, qi[p])),
        scratch_shapes=[pltpu.VMEM((DSA_HEADS, t), F32), pltpu.VMEM((DSA_HEADS, t), F32),
                        pltpu.VMEM((DSA_QW, t), F32)])
    ot = pl.pallas_call(
        functools.partial(_dsa_prompt_body, t=t), grid_spec=grid_spec,
        out_shape=jax.ShapeDtypeStruct((DSA_QW, T), BF16),
        compiler_params=_cp(("arbitrary",)), name="dsa_prompt",
    )(jnp.asarray(qs), jnp.asarray(ks), q, k, v.T, keys, thr)
    return ot.T


TPAD = 8


def _head_rows(ref, h, n_heads):
    return ref[pl.ds(h, ref.shape[0] // n_heads, stride=n_heads), :].astype(BF16)


def _decode_step(qs, kget, vget, n_blocks, bias, state, rows_per_q):
    m_ref, l_ref, acc_ref = state
    n_q = len(qs)
    s = jnp.concatenate(
        [jnp.concatenate([_nt_dot(qs[j], kget(b, j)) for j in range(n_q)], axis=0) for b in range(n_blocks)],
        axis=1) * ATT_SCALE + bias
    m_prev = m_ref[...]
    m_new = jnp.maximum(m_prev, jnp.max(s, axis=-1, keepdims=True))
    alpha = jnp.exp(m_prev - m_new)
    p = jnp.exp(s - m_new)
    l_ref[...] = alpha * l_ref[...] + jnp.sum(p, axis=-1, keepdims=True)
    m_ref[...] = m_new
    pv = []
    for j in range(n_q):
        rows = slice(j * rows_per_q, (j + 1) * rows_per_q)
        d = None
        for b in range(n_blocks):
            t = jnp.dot(p[rows, b * LANES:(b + 1) * LANES].astype(BF16), vget(b, j), preferred_element_type=F32)
            d = t if d is None else d + t
        pv.append(d)
    acc_ref[...] = alpha * acc_ref[...] + jnp.concatenate(pv, axis=0)


def _fox_sample_body(pt_ref, q_ref, cq_ref, ck_ref, *refs, G):
    k_refs, v_refs = refs[:G], refs[G:2 * G]
    kn_ref, vn_ref, ckn_ref, o_ref, m_ref, l_ref, acc_ref = refs[2 * G:]
    p = pl.program_id(1)
    state = (m_ref, l_ref, acc_ref)

    @pl.when(p == 0)
    def _():
        _init_softmax_state(*state)

    qs = [q_ref[h].astype(BF16) for h in range(FOX_HEADS)]
    cq = cq_ref[...]

    def forget_bias(ck):
        return cq - jnp.concatenate(
            [jnp.broadcast_to(ck[h:h + 1, :], (TPAD, ck.shape[1])) for h in range(FOX_HEADS)], axis=0)

    _decode_step(qs, lambda b, h: _head_rows(k_refs[b], h, FOX_HEADS), lambda b, h: _head_rows(v_refs[b], h, FOX_HEADS),
                 G, forget_bias(ck_ref[...]), state, TPAD)

    @pl.when(p == pl.num_programs(1) - 1)
    def _():
        rows = FOX_HEADS * TPAD
        tok = lax.broadcasted_iota(I32, (rows, LANES), 0) % TPAD
        col = lax.broadcasted_iota(I32, (rows, LANES), 1)
        bias = jnp.where(col <= tok, forget_bias(ckn_ref[...]), NEG)
        _decode_step(qs, lambda b, h: kn_ref[h], lambda b, h: vn_ref[h], 1, bias, state, TPAD)
        o_ref[...] = acc_ref[...] / l_ref[...]


def _fox_sample(q, cq, ckT, cache_k, cache_v, page_table, knew, vnew, cknT, group=4):
    B, n_pages = page_table.shape
    R = cache_k.shape[1]
    page = R // FOX_HEADS
    rows = FOX_HEADS * TPAD
    G = _page_group(n_pages, group)
    per_b3 = lambda b, p, pt: (b, 0, 0)
    per_b4 = lambda b, p, pt: (b, 0, 0, 0)
    grid_spec = pltpu.PrefetchScalarGridSpec(
        num_scalar_prefetch=1, grid=(B, n_pages // G),
        in_specs=[pl.BlockSpec((None, FOX_HEADS, TPAD, HEAD_DIM), per_b4),
                  pl.BlockSpec((None, rows, 1), per_b3),
                  pl.BlockSpec((None, FOX_HEADS, G * page), lambda b, p, pt: (b, 0, p))]
        + _paged_specs(G, n_pages, (None, R, HEAD_DIM)) + _paged_specs(G, n_pages, (None, R, HEAD_DIM))
        + [pl.BlockSpec((None, FOX_HEADS, LANES, HEAD_DIM), per_b4),
           pl.BlockSpec((None, FOX_HEADS, LANES, HEAD_DIM), per_b4),
           pl.BlockSpec((None, FOX_HEADS, LANES), per_b3)],
        out_specs=pl.BlockSpec((None, rows, HEAD_DIM), per_b3),
        scratch_shapes=[pltpu.VMEM((rows, 1), F32), pltpu.VMEM((rows, 1), F32),
                        pltpu.VMEM((rows, HEAD_DIM), F32)])
    return pl.pallas_call(
        functools.partial(_fox_sample_body, G=G), grid_spec=grid_spec,
        out_shape=jax.ShapeDtypeStruct((B, rows, HEAD_DIM), F32),
        compiler_params=_cp(("parallel", "arbitrary")), name="fox_sample",
    )(page_table.reshape(-1).astype(I32), q, cq, ckT, *([cache_k] * G), *([cache_v] * G), knew, vnew, cknT)


def _idx_sample_body(pt_ref, q_ref, w_ref, *refs, ts, n_sel, G):
    k_refs = refs[:G]
    kn_ref, keys_ref, thr_ref = refs[G:]
    p = pl.program_id(1)
    n_pages = keys_ref.shape[0] - 1
    pad = jnp.full((TPAD - ts, LANES), INT_MIN, I32)

    def score(k):
        r = jnp.maximum(_nt_dot(q_ref[...], k), 0.0) * w_ref[...]
        return jnp.sum(r.reshape(ts, IDX_HEADS, LANES), axis=1)

    for g in range(G):
        keys_ref[p * G + g] = jnp.concatenate([_float_key(score(k_refs[g][...].astype(BF16))), pad], axis=0)

    @pl.when(p == pl.num_programs(1) - 1)
    def _():
        tok = lax.broadcasted_iota(I32, (ts, LANES), 0)
        col = lax.broadcasted_iota(I32, (ts, LANES), 1)
        kn = jnp.where(col <= tok, _float_key(score(kn_ref[...])), INT_MIN)
        keys_ref[n_pages] = jnp.concatenate([kn, pad], axis=0)
        thr_ref[...] = _kth_key(keys_ref, n_pages + 1, n_sel, 0, 8)


def _idx_sample(q, w, cache_k, page_table, knew, ts, n_sel, group=16):
    B, n_pages = page_table.shape
    R = cache_k.shape[1]
    rows = ts * IDX_HEADS
    G = _page_group(n_pages, group)
    per_b = lambda b, p, pt: (b, 0, 0)
    grid_spec = pltpu.PrefetchScalarGridSpec(
        num_scalar_prefetch=1, grid=(B, n_pages // G),
        in_specs=[pl.BlockSpec((None, rows, IDX_DIM), per_b),
                  pl.BlockSpec((None, rows, 1), per_b)]
        + _paged_specs(G, n_pages, (None, R, IDX_DIM))
        + [pl.BlockSpec((None, LANES, IDX_DIM), per_b)],
        out_specs=[pl.BlockSpec((None, n_pages + 1, 8, LANES), lambda b, p, pt: (b, 0, 0, 0)),
                   pl.BlockSpec((None, 8, 1), per_b)])
    return pl.pallas_call(
        functools.partial(_idx_sample_body, ts=ts, n_sel=n_sel, G=G), grid_spec=grid_spec,
        out_shape=[jax.ShapeDtypeStruct((B, n_pages + 1, 8, LANES), I32),
                   jax.ShapeDtypeStruct((B, 8, 1), I32)],
        compiler_params=_cp(("parallel", "arbitrary")), name="idx_sample",
    )(page_table.reshape(-1).astype(I32), q, w, *([cache_k] * G), knew)


def _dsa_sample_body(pt_ref, q_ref, keys_ref, keysn_ref, thr_ref, *refs, G):
    k_refs, v_refs = refs[:G], refs[G:2 * G]
    kn_ref, vn_ref, o_ref, m_ref, l_ref, acc_ref = refs[2 * G:]
    p = pl.program_id(1)
    state = (m_ref, l_ref, acc_ref)

    @pl.when(p == 0)
    def _():
        _init_softmax_state(*state)

    qs = [q_ref[g].astype(BF16) for g in range(DSA_KV_HEADS)]
    thr = thr_ref[...]

    def select_bias(keys):
        sel = jnp.concatenate([jnp.where(kk >= thr, 0.0, NEG) for kk in keys], axis=1)
        return jnp.concatenate([sel] * DSA_HEADS, axis=0)

    _decode_step(qs, lambda b, g: _head_rows(k_refs[b], g, DSA_KV_HEADS),
                 lambda b, g: _head_rows(v_refs[b], g, DSA_KV_HEADS),
                 G, select_bias([keys_ref[b] for b in range(G)]), state, DSA_GROUP * TPAD)

    @pl.when(p == pl.num_programs(1) - 1)
    def _():
        _decode_step(qs, lambda b, g: kn_ref[g], lambda b, g: vn_ref[g], 1, select_bias([keysn_ref[...]]),
                     state, DSA_GROUP * TPAD)
        o_ref[...] = acc_ref[...] / l_ref[...]


def _dsa_sample(q, keys, thr, cache_k, cache_v, page_table, knew, vnew, group=8):
    B, n_pages = page_table.shape
    R = cache_k.shape[1]
    rows = DSA_HEADS * TPAD
    G = _page_group(n_pages, group)
    per_b3 = lambda b, p, pt: (b, 0, 0)
    per_b4 = lambda b, p, pt: (b, 0, 0, 0)
    grid_spec = pltpu.PrefetchScalarGridSpec(
        num_scalar_prefetch=1, grid=(B, n_pages // G),
        in_specs=[pl.BlockSpec((None, DSA_KV_HEADS, DSA_GROUP * TPAD, HEAD_DIM), per_b4),
                  pl.BlockSpec((None, G, TPAD, LANES), lambda b, p, pt: (b, p, 0, 0)),
                  pl.BlockSpec((None, None, TPAD, LANES), lambda b, p, pt: (b, n_pages, 0, 0)),
                  pl.BlockSpec((None, TPAD, 1), per_b3)]
        + _paged_specs(G, n_pages, (None, R, HEAD_DIM)) + _paged_specs(G, n_pages, (None, R, HEAD_DIM))
        + [pl.BlockSpec((None, DSA_KV_HEADS, LANES, HEAD_DIM), per_b4),
           pl.BlockSpec((None, DSA_KV_HEADS, LANES, HEAD_DIM), per_b4)],
        out_specs=pl.BlockSpec((None, rows, HEAD_DIM), per_b3),
        scratch_shapes=[pltpu.VMEM((rows, 1), F32), pltpu.VMEM((rows, 1), F32),
                        pltpu.VMEM((rows, HEAD_DIM), F32)])
    return pl.pallas_call(
        functools.partial(_dsa_sample_body, G=G), grid_spec=grid_spec,
        out_shape=jax.ShapeDtypeStruct((B, rows, HEAD_DIM), F32),
        compiler_params=_cp(("parallel", "arbitrary")), name="dsa_sample",
    )(page_table.reshape(-1).astype(I32), q, keys, keys, thr, *([cache_k] * G), *([cache_v] * G), knew, vnew)


def _top_rows(x, n):
    out = []
    for _ in range(n):
        m = jnp.max(x, axis=0, keepdims=True)
        out.append(m)
        x = jnp.where(x == m, -jnp.inf, x)
    return out


def _peer_select_body(q_ref, keys_ref, s0_ref, s1_ref, st_ref, *, n_heads):
    nk = keys_ref.shape[2]
    thr, off = [], []
    for h in range(n_heads):
        tops = []
        for c, s_ref in enumerate((s0_ref, s1_ref)):
            col = (2 * h + c) * nk
            st = _nt_dot(q_ref[:, col:col + nk].astype(BF16), keys_ref[h, c]).T
            s_ref[h * nk:(h + 1) * nk, :] = st
            tops.append(jnp.concatenate(_top_rows(st, PEER_TOPK), axis=0))
        cand = jnp.concatenate([tops[0][r:r + 1, :] + tops[1] for r in range(PEER_TOPK)], axis=0)
        best = _top_rows(cand, PEER_TOPK)
        z = jnp.ones_like(best[0])
        for r in range(1, PEER_TOPK):
            z = z + jnp.exp(best[r] - best[0])
        thr.append(best[-1])
        off.append(-(best[0] + jnp.log(z)))
    st_ref[...] = jnp.concatenate(thr + off, axis=0)


def _peer_select(q, keys_bf, tm=128):
    M = q.shape[0]
    n_heads, _, nk, _ = keys_bf.shape
    tm = _tile(M, tm)
    col = lambda i: (0, i)
    return pl.pallas_call(
        functools.partial(_peer_select_body, n_heads=n_heads),
        grid=(M // tm,),
        in_specs=[pl.BlockSpec((tm, q.shape[1]), lambda i: (i, 0)),
                  pl.BlockSpec(keys_bf.shape, lambda i: (0, 0, 0, 0))],
        out_specs=[pl.BlockSpec((n_heads * nk, tm), col), pl.BlockSpec((n_heads * nk, tm), col),
                   pl.BlockSpec((2 * n_heads, tm), col)],
        out_shape=[jax.ShapeDtypeStruct((n_heads * nk, M), F32), jax.ShapeDtypeStruct((n_heads * nk, M), F32),
                   jax.ShapeDtypeStruct((2 * n_heads, M), F32)],
        compiler_params=_cp(("parallel",)), name="peer_select",
    )(q, keys_bf)


def _peer_expert_body(x_ref, u_ref, v_ref, s0_ref, s1_ref, st_ref, o_ref, *, n_heads, nk, ni):
    e = pl.program_id(1)

    @pl.when(e == 0)
    def _():
        o_ref[...] = jnp.zeros_like(o_ref)

    a = _nt_dot(u_ref[...], x_ref[...])
    act = 0.5 * a * (1.0 + lax.erf(a * (2.0 ** -0.5)))
    gates = []
    for il in range(ni):
        w = None
        for h in range(n_heads):
            tot = s0_ref[il * n_heads + h:il * n_heads + h + 1, :] + s1_ref[h * nk:(h + 1) * nk, :]
            val = jnp.where(tot >= st_ref[h:h + 1, :], jnp.exp(tot + st_ref[n_heads + h:n_heads + h + 1, :]), 0.0)
            w = val if w is None else w + val
        gates.append(w)
    ga = (jnp.concatenate(gates, axis=0) * act).astype(BF16)
    o_ref[...] += lax.dot_general(ga, v_ref[...], (((0,), (0,)), ((), ())), preferred_element_type=F32)


def _peer_expert(hn_bf, u_bf, v_bf, s0t, s1t, st, n_heads, nk, tm=512, te=256):
    M, D = hn_bf.shape
    E = u_bf.shape[0]
    tm = _tile(M, tm)
    te = _tile(E, te)
    ni = te // nk
    s0g = jnp.transpose(s0t.reshape(n_heads, E // te, ni, M), (1, 2, 0, 3)).reshape(E // te, ni * n_heads, M)
    return pl.pallas_call(
        functools.partial(_peer_expert_body, n_heads=n_heads, nk=nk, ni=ni),
        grid=(M // tm, E // te),
        in_specs=[pl.BlockSpec((tm, D), lambda i, e: (i, 0)),
                  pl.BlockSpec((te, D), lambda i, e: (e, 0)),
                  pl.BlockSpec((te, D), lambda i, e: (e, 0)),
                  pl.BlockSpec((None, ni * n_heads, tm), lambda i, e: (e, 0, i)),
                  pl.BlockSpec((n_heads * nk, tm), lambda i, e: (0, i)),
                  pl.BlockSpec((2 * n_heads, tm), lambda i, e: (0, i))],
        out_specs=pl.BlockSpec((tm, D), lambda i, e: (i, 0)),
        out_shape=jax.ShapeDtypeStruct((M, D), F32),
        compiler_params=_cp(("parallel", "arbitrary")), name="peer_expert",
    )(hn_bf, u_bf, v_bf, s0g, s1t, st)


def _prep_w_in(w_in):
    o = np.cumsum([0, FOX_W, FOX_W, FOX_W, FOX_HEADS, DSA_QW, DSA_KW, DSA_KW, IDX_QW, IDX_DIM, IDX_HEADS])
    seg = lambda i: w_in[:, o[i]:o[i + 1]]
    cols = [seg(0), seg(1), seg(2), seg(4), seg(7), seg(5), seg(6), seg(8), seg(9), seg(3)]
    used = int(o[-1])
    total = -(-used // 768) * 768
    cols.append(jnp.zeros((w_in.shape[0], total - used), w_in.dtype))
    return jnp.concatenate(cols, axis=1).astype(BF16)


def _channel(x, mix_bf, p_l, w_o_bf, g_norm2, w_pq_bf, keys_bf, u_bf, v_bf, g_norm3, w_gate_bf, w_proj_bf,
             g_ple):
    n_heads, _, nk, _ = keys_bf.shape
    h1 = _mm(mix_bf, w_o_bf, res=x, epi="res")
    q, hn_bf = _mm(h1, w_pq_bf, g=g_norm2, emit_xn=True)
    s0t, s1t, st = _peer_select(q, keys_bf)
    moe = _peer_expert(hn_bf, u_bf, v_bf, s0t, s1t, st, n_heads, nk)
    pn = _rownorm_mm(p_l, w_proj_bf, g_ple)
    return _mm(h1, w_gate_bf, g=g_norm3, x2=moe, res=h1, res2=moe, aux=pn, epi="gate", tm=256)


def _head_major(x, B, ts, n_heads, n_rows):
    x4 = jnp.transpose(x.reshape(B, ts, n_heads, HEAD_DIM), (0, 2, 1, 3))
    return jnp.pad(x4, ((0, 0), (0, 0), (0, n_rows - ts), (0, 0)))


def _token_major(o, B, ts, n_heads):
    o4 = o.reshape(B, n_heads, TPAD, HEAD_DIM)[:, :, :ts]
    return jnp.transpose(o4, (0, 2, 1, 3)).reshape(B * ts, n_heads * HEAD_DIM)


def _pad_rows(x, n):
    return jnp.pad(x, ((0, 0), (0, n - x.shape[1]), (0, 0)))


def kernel(x_prompt, x_sample, cache_fox_k, cache_fox_v, cache_fox_logf, cache_dsa_k, cache_dsa_v, cache_idx_k, page_table, p_prompt, p_sample, g_norm1, w_in, b_f, g_q_fox, g_k_fox, g_q_dsa, g_k_dsa, w_o, g_norm2, w_peer_q, peer_keys, peer_u, peer_v, g_norm3, w_ple_gate, w_ple_proj, g_ple):
    Bp, Tp, D = x_prompt.shape
    Bs, Ts = x_sample.shape[:2]
    depth = w_in.shape[0]
    n_pages = page_table.shape[1]
    page = cache_fox_k.shape[2]
    past = n_pages * page
    n_phys = cache_fox_k.shape[1]
    assert Bp == 1 and Ts <= TPAD and page == LANES

    def pool(cache):
        return cache.reshape((depth * n_phys, -1) + cache.shape[-1:])

    fox_k_pool, fox_v_pool, dsa_k_pool, dsa_v_pool = pool(cache_fox_k), pool(cache_fox_v), pool(cache_dsa_k), pool(cache_dsa_v)
    fox_lf_pool = cache_fox_logf.reshape(depth * n_phys, page, FOX_HEADS)
    idx_k_pool = cache_idx_k.reshape(depth * n_phys, page, IDX_DIM)

    h_p = x_prompt.reshape(Tp, D)
    h_s = x_sample.reshape(Bs * Ts, D)
    pos_p = jnp.arange(Tp, dtype=I32)
    pos_s = jnp.tile(past + jnp.arange(Ts, dtype=I32), Bs)
    outs = [[] for _ in range(12)]
    for l in range(depth):
        w_in_bf = _prep_w_in(w_in[l])
        chan_w = (w_o[l].astype(BF16), g_norm2[l], w_peer_q[l].astype(BF16), peer_keys[l].astype(BF16),
                  peer_u[l].astype(BF16), peer_v[l].astype(BF16), g_norm3[l], w_ple_gate[l].astype(BF16),
                  w_ple_proj[l].astype(BF16), g_ple[l])
        norm_w = (b_f[l], g_q_fox[l], g_k_fox[l], g_q_dsa[l], g_k_dsa[l])
        pt_l = page_table + l * n_phys

        z = _mm(h_p, w_in_bf, g=g_norm1[l], tn=768)
        pr = _post(z, pos_p, *norm_w)
        ident = jnp.arange(Tp // page, dtype=I32).reshape(1, -1)
        c, _ = _paged_cumsum(pr["lf"].reshape(Tp // page, page, FOX_HEADS), ident,
                             jnp.zeros((1, 8, FOX_HEADS), F32))
        o_fox = _fox_prompt(pr["qf"], pr["kfb"], pr["vfb"], c[0])
        keys, thr, tq = _idx_prompt(pr["qi"], pr["wi"], pr["kia"], pr["kib"], min(TOPK_MAX, Tp // 4))
        o_dsa = _dsa_prompt(pr["qd"], pr["kdb"], pr["vdb"], keys, thr, tq)
        mix = jnp.concatenate([o_fox, o_dsa], axis=-1)
        h_p = _channel(h_p, mix, p_prompt[l].reshape(Tp, -1), *chan_w)
        for i, (name, shape) in enumerate((("kf", (Bp, Tp, FOX_HEADS, HEAD_DIM)), ("vf", (Bp, Tp, FOX_HEADS, HEAD_DIM)),
                                           ("lf", (Bp, Tp, FOX_HEADS)), ("kd", (Bp, Tp, DSA_KV_HEADS, HEAD_DIM)),
                                           ("vd", (Bp, Tp, DSA_KV_HEADS, HEAD_DIM)), ("ki", (Bp, Tp, IDX_DIM)))):
            outs[i].append(pr[name].reshape(shape))

        zs = _mm(h_s, w_in_bf, g=g_norm1[l], tn=768)
        sr = _post(zs, pos_s, *norm_w)
        lf_new = _pad_rows(sr["lf"].reshape(Bs, Ts, FOX_HEADS), TPAD)
        c_past, c_new = _paged_cumsum(fox_lf_pool, pt_l, lf_new)
        cq = jnp.transpose(c_new, (0, 2, 1)).reshape(Bs, FOX_HEADS * TPAD, 1)
        ckT = jnp.transpose(c_past, (0, 2, 1))
        cknT = jnp.transpose(_pad_rows(c_new, LANES), (0, 2, 1))
        o = _fox_sample(_head_major(sr["qf"], Bs, Ts, FOX_HEADS, TPAD).astype(F32), cq, ckT, fox_k_pool, fox_v_pool, pt_l,
                        _head_major(sr["kfb"], Bs, Ts, FOX_HEADS, LANES), _head_major(sr["vfb"], Bs, Ts, FOX_HEADS, LANES),
                        cknT)
        o_fox_s = _token_major(o, Bs, Ts, FOX_HEADS)
        n_sel = min(TOPK_MAX, (past + Ts) // 4)
        keys_s, thr_s = _idx_sample(sr["qi"].reshape(Bs, Ts * IDX_HEADS, IDX_DIM),
                                    sr["wi"].reshape(Bs, Ts * IDX_HEADS, 1), idx_k_pool, pt_l,
                                    _pad_rows(sr["kia"][:, :IDX_DIM].reshape(Bs, Ts, IDX_DIM), LANES), Ts, n_sel)
        qd = _head_major(sr["qd"], Bs, Ts, DSA_HEADS, TPAD).astype(F32)
        od = _dsa_sample(qd.reshape(Bs, DSA_KV_HEADS, DSA_GROUP * TPAD, HEAD_DIM), keys_s, thr_s,
                         dsa_k_pool, dsa_v_pool, pt_l, _head_major(sr["kdb"], Bs, Ts, DSA_KV_HEADS, LANES),
                         _head_major(sr["vdb"], Bs, Ts, DSA_KV_HEADS, LANES))
        o_dsa_s = _token_major(od, Bs, Ts, DSA_HEADS)
        mix_s = jnp.concatenate([o_fox_s, o_dsa_s], axis=-1).astype(BF16)
        h_s = _channel(h_s, mix_s, p_sample[l].reshape(Bs * Ts, -1), *chan_w)
        for i, (name, shape) in enumerate((("kf", (Bs, Ts, FOX_HEADS, HEAD_DIM)), ("vf", (Bs, Ts, FOX_HEADS, HEAD_DIM)),
                                           ("lf", (Bs, Ts, FOX_HEADS)), ("kd", (Bs, Ts, DSA_KV_HEADS, HEAD_DIM)),
                                           ("vd", (Bs, Ts, DSA_KV_HEADS, HEAD_DIM)), ("ki", (Bs, Ts, IDX_DIM)))):
            outs[6 + i].append(sr[name].reshape(shape))
    return (h_p.reshape(Bp, Tp, D), h_s.reshape(Bs, Ts, D)) + tuple(jnp.stack(o) for o in outs)
```

```python
import functools
import math

import numpy as np
import jax
import jax.numpy as jnp
from jax import lax
from jax.experimental import pallas as pl
from jax.experimental.pallas import tpu as pltpu

F32 = jnp.float32
BF16 = jnp.bfloat16
I32 = jnp.int32

HEAD_DIM = 128
FOX_HEADS = 16
DSA_HEADS = 16
DSA_KV_HEADS = 4
DSA_GROUP = DSA_HEADS // DSA_KV_HEADS
IDX_HEADS = 32
IDX_DIM = 64
TOPK_MAX = 256
ROPE_THETA = 10000.0
PEER_TOPK = 16
EPS = 1e-6

FOX_W = FOX_HEADS * HEAD_DIM
DSA_QW = DSA_HEADS * HEAD_DIM
DSA_KW = DSA_KV_HEADS * HEAD_DIM
IDX_QW = IDX_HEADS * IDX_DIM

LANES = 128
NEG = -1e30
INT_MIN = -(2 ** 31)
VMEM_LIMIT = 56 * 1024 * 1024
ATT_SCALE = HEAD_DIM ** -0.5
LOG2E = math.log2(math.e)
QS = 128
KS = 256
WIDTH = 4


def _cp(sem):
    return pltpu.CompilerParams(dimension_semantics=sem, vmem_limit_bytes=VMEM_LIMIT)


def _tile(n, pref):
    t = min(n, pref)
    while n % t:
        t //= 2
    return t


def _nt_dot(a, b):
    return lax.dot_general(a, b, (((1,), (1,)), ((), ())), preferred_element_type=F32)


def _mm_body(*refs, norm, add2, epi, emit_xn, prologue):
    it = iter(refs)
    x_ref = next(it)
    x2_ref = next(it) if add2 else None
    g_ref = next(it) if norm else None
    w_ref = next(it)
    res_ref = next(it) if epi in ("res", "gate") else None
    res2_ref = next(it) if (epi == "gate" and add2) else None
    aux_ref = next(it) if epi == "gate" else None
    o_ref = next(it)
    xo_ref = next(it) if emit_xn else None
    xn_ref = next(it) if prologue else None

    if prologue:
        @pl.when(pl.program_id(1) == 0)
        def _():
            x = x_ref[...].astype(F32)
            if add2:
                x = x + x2_ref[...]
            if norm:
                ms = jnp.mean(x * x, axis=-1, keepdims=True)
                x = x * lax.rsqrt(ms + EPS) * g_ref[...]
            xb = x.astype(BF16)
            xn_ref[...] = xb
            if emit_xn:
                xo_ref[...] = xb
        lhs = xn_ref[...]
    else:
        lhs = x_ref[...]
    acc = jnp.dot(lhs, w_ref[...], preferred_element_type=F32)
    if epi == "res":
        acc = res_ref[...] + acc
    elif epi == "gate":
        r = res_ref[...]
        if add2:
            r = r + res2_ref[...]
        acc = r + jax.nn.sigmoid(acc) * aux_ref[...]
    o_ref[...] = acc


def _mm(x, w, *, g=None, x2=None, res=None, res2=None, aux=None, epi="none", emit_xn=False,
        tm=512, tn=512):
    M, K = x.shape
    N = w.shape[1]
    tm = _tile(M, tm)
    tn = _tile(N, tn)
    norm = g is not None
    add2 = x2 is not None
    prologue = norm or add2 or x.dtype != BF16
    row = pl.BlockSpec((tm, K), lambda i, j: (i, 0))
    blk = pl.BlockSpec((tm, tn), lambda i, j: (i, j))
    ins, specs = [x], [row]
    if add2:
        ins.append(x2); specs.append(row)
    if norm:
        ins.append(g.reshape(1, K)); specs.append(pl.BlockSpec((1, K), lambda i, j: (0, 0)))
    ins.append(w); specs.append(pl.BlockSpec((K, tn), lambda i, j: (0, j)))
    if epi in ("res", "gate"):
        ins.append(res); specs.append(blk)
    if epi == "gate" and add2:
        ins.append(res2); specs.append(blk)
    if epi == "gate":
        ins.append(aux); specs.append(blk)
    out_shape = [jax.ShapeDtypeStruct((M, N), F32)]
    out_specs = [blk]
    if emit_xn:
        out_shape.append(jax.ShapeDtypeStruct((M, K), BF16))
        out_specs.append(row)
    scratch = [pltpu.VMEM((tm, K), BF16)] if prologue else []
    outs = pl.pallas_call(
        functools.partial(_mm_body, norm=norm, add2=add2, epi=epi, emit_xn=emit_xn, prologue=prologue),
        grid=(M // tm, N // tn),
        in_specs=specs, out_specs=out_specs, out_shape=out_shape, scratch_shapes=scratch,
        compiler_params=_cp(("parallel", "arbitrary")), name="mm_" + epi,
    )(*ins)
    return outs if emit_xn else outs[0]


def _mm2_res_body(x1_ref, x2_ref, w1_ref, w2_ref, res_ref, o_ref):
    o_ref[...] = res_ref[...] + (jnp.dot(x1_ref[...], w1_ref[...], preferred_element_type=F32)
                                 + jnp.dot(x2_ref[...], w2_ref[...], preferred_element_type=F32))


def _mm2_res(x1, x2, w, res, tm=512, tn=512):
    M, K1 = x1.shape
    N = w.shape[1]
    assert x2.shape == (M, K1) and w.shape[0] == 2 * K1
    tm = _tile(M, tm)
    tn = _tile(N, tn)
    row = pl.BlockSpec((tm, K1), lambda i, j: (i, 0))
    blk = pl.BlockSpec((tm, tn), lambda i, j: (i, j))
    return pl.pallas_call(
        _mm2_res_body,
        grid=(M // tm, N // tn),
        in_specs=[row, row, pl.BlockSpec((K1, tn), lambda i, j: (0, j)), pl.BlockSpec((K1, tn), lambda i, j: (1, j)), blk],
        out_specs=blk, out_shape=jax.ShapeDtypeStruct((M, N), F32),
        compiler_params=_cp(("parallel", "arbitrary")), name="mm2_res",
    )(x1, x2, w, w, res)


def _rownorm_mm_body(x_ref, w_ref, g_ref, o_ref):
    y = jnp.dot(x_ref[...].astype(BF16), w_ref[...], preferred_element_type=F32)
    ms = jnp.mean(y * y, axis=-1, keepdims=True)
    o_ref[...] = y * lax.rsqrt(ms + EPS) * g_ref[...]


def _rownorm_mm(x, w, g, tm=256):
    M, K = x.shape
    N = w.shape[1]
    tm = _tile(M, tm)
    return pl.pallas_call(
        _rownorm_mm_body,
        grid=(M // tm,),
        in_specs=[pl.BlockSpec((tm, K), lambda i: (i, 0)),
                  pl.BlockSpec((K, N), lambda i: (0, 0)),
                  pl.BlockSpec((1, N), lambda i: (0, 0))],
        out_specs=pl.BlockSpec((tm, N), lambda i: (i, 0)),
        out_shape=jax.ShapeDtypeStruct((M, N), F32),
        compiler_params=_cp(("parallel",)), name="rownorm_mm",
    )(x, w, g.reshape(1, N))


def _head_norm(x, g):
    ms = jnp.mean(x * x, axis=-1, keepdims=True)
    return x * lax.rsqrt(ms + EPS) * g


def _rope_full(x, cos, sin):
    return x * cos + pltpu.roll(x, HEAD_DIM // 2, 1) * sin


def _rope_pair(x, cos, sin, lane):
    half = IDX_DIM // 2
    rot = jnp.where((lane % IDX_DIM) < half, pltpu.roll(x, LANES - half, 1), pltpu.roll(x, half, 1))
    return x * cos + rot * sin


def _post_body(zqf, zkf, zvf, zqd, zqi, zkv, zs, cd, sd, ci, si, bf, gqf, gkf, gqd, gkd,
               qf_o, kf_o, kfb_o, vf_o, vfb_o, lf_o, qd_o, kd_o, kdb_o, vd_o, vdb_o,
               qi_o, ki_o, kia_o, kib_o, wi_o, vft_o, vdt_o, *, q_scale):
    tm = zqf.shape[0]
    cos_d, sin_d = cd[...], sd[...]
    cos_i, sin_i = ci[...], si[...]
    lane = lax.broadcasted_iota(I32, (tm, LANES), 1)
    for h in range(FOX_HEADS):
        sl = slice(h * HEAD_DIM, (h + 1) * HEAD_DIM)
        qf_o[:, sl] = (_head_norm(zqf[:, sl], gqf[...]) * q_scale).astype(BF16)
        k = _head_norm(zkf[:, sl], gkf[...])
        kf_o[:, sl] = k
        kfb_o[:, sl] = k.astype(BF16)
        v = zvf[:, sl]
        vf_o[:, sl] = v
        vfb_o[:, sl] = v.astype(BF16)
        vft_o[sl, :] = v.T.astype(BF16)
    for h in range(DSA_HEADS):
        sl = slice(h * HEAD_DIM, (h + 1) * HEAD_DIM)
        qd_o[:, sl] = (_rope_full(_head_norm(zqd[:, sl], gqd[...]), cos_d, sin_d) * q_scale).astype(BF16)
    for h in range(DSA_KV_HEADS):
        sl = slice(h * HEAD_DIM, (h + 1) * HEAD_DIM)
        k = _rope_full(_head_norm(zkv[:, sl], gkd[...]), cos_d, sin_d)
        kd_o[:, sl] = k
        kdb_o[:, sl] = k.astype(BF16)
        v = zkv[:, DSA_KW + h * HEAD_DIM: DSA_KW + (h + 1) * HEAD_DIM]
        vd_o[:, sl] = v
        vdb_o[:, sl] = v.astype(BF16)
        vdt_o[sl, :] = v.T.astype(BF16)
    for j in range(IDX_QW // LANES):
        sl = slice(j * LANES, (j + 1) * LANES)
        qi_o[:, sl] = _rope_pair(zqi[:, sl], cos_i, sin_i, lane).astype(BF16)
    x = zs[...]
    y = _rope_pair(x, cos_i, sin_i, lane)
    ki_o[...] = y[:, :IDX_DIM]
    ka = jnp.where(lane < IDX_DIM, y, 0.0)
    kia_o[...] = ka.astype(BF16)
    kib_o[...] = pltpu.roll(ka, IDX_DIM, 1).astype(BF16)
    wi_o[...] = x[:, IDX_DIM:IDX_DIM + IDX_HEADS] * (IDX_HEADS ** -0.5 * IDX_DIM ** -0.5)
    fl = x[:, IDX_DIM + IDX_HEADS:IDX_DIM + IDX_HEADS + FOX_HEADS] + bf[...]
    lf_o[...] = jnp.minimum(fl, 0.0) - jnp.log1p(jnp.exp(-jnp.abs(fl)))


def _post(z, pos, b_f, g_q_fox, g_k_fox, g_q_dsa, g_k_dsa, q_scale, tm=128):
    M = z.shape[0]
    tm = _tile(M, tm)
    posf = pos.astype(F32)[:, None]
    hd = HEAD_DIM // 2
    ang = posf * (ROPE_THETA ** (-jnp.arange(hd, dtype=F32) * 2.0 / HEAD_DIM))
    cd = jnp.concatenate([jnp.cos(ang), jnp.cos(ang)], axis=-1)
    sd = jnp.concatenate([-jnp.sin(ang), jnp.sin(ang)], axis=-1)
    hi = IDX_DIM // 2
    angi = posf * (ROPE_THETA ** (-jnp.arange(hi, dtype=F32) * 2.0 / IDX_DIM))
    ci = jnp.tile(jnp.cos(angi), (1, 4))
    si = jnp.tile(jnp.concatenate([-jnp.sin(angi), jnp.sin(angi)], axis=-1), (1, 2))

    def zspec(w, c):
        return pl.BlockSpec((tm, w), lambda i: (i, c))

    def rspec(w):
        return pl.BlockSpec((tm, w), lambda i: (i, 0))

    def cspec(w):
        return pl.BlockSpec((1, w), lambda i: (0, 0))

    small_col = (5 * FOX_W + 2 * DSA_KW) // LANES
    in_specs = [zspec(FOX_W, 0), zspec(FOX_W, 1), zspec(FOX_W, 2), zspec(DSA_QW, 3), zspec(IDX_QW, 4),
                zspec(2 * DSA_KW, 5 * FOX_W // (2 * DSA_KW)), zspec(LANES, small_col),
                rspec(LANES), rspec(LANES), rspec(LANES), rspec(LANES),
                cspec(FOX_HEADS), cspec(HEAD_DIM), cspec(HEAD_DIM), cspec(HEAD_DIM), cspec(HEAD_DIM)]
    outs = [(FOX_W, BF16), (FOX_W, F32), (FOX_W, BF16), (FOX_W, F32), (FOX_W, BF16), (FOX_HEADS, F32),
            (DSA_QW, BF16), (DSA_KW, F32), (DSA_KW, BF16), (DSA_KW, F32), (DSA_KW, BF16),
            (IDX_QW, BF16), (IDX_DIM, F32), (LANES, BF16), (LANES, BF16), (IDX_HEADS, F32)]
    tspec = lambda w: pl.BlockSpec((w, tm), lambda i: (0, i))
    res = pl.pallas_call(
        functools.partial(_post_body, q_scale=q_scale),
        grid=(M // tm,),
        in_specs=in_specs,
        out_specs=[rspec(w) for w, _ in outs] + [tspec(FOX_W), tspec(DSA_KW)],
        out_shape=[jax.ShapeDtypeStruct((M, w), d) for w, d in outs]
        + [jax.ShapeDtypeStruct((FOX_W, M), BF16), jax.ShapeDtypeStruct((DSA_KW, M), BF16)],
        compiler_params=_cp(("parallel",)), name="post_proj",
    )(z, z, z, z, z, z, z, cd, sd, ci, si, b_f.reshape(1, -1), g_q_fox.reshape(1, -1),
      g_k_fox.reshape(1, -1), g_q_dsa.reshape(1, -1), g_k_dsa.reshape(1, -1))
    names = ("qf", "kf", "kfb", "vf", "vfb", "lf", "qd", "kd", "kdb", "vd", "vdb", "qi", "ki", "kia",
             "kib", "wi", "vft", "vdt")
    return dict(zip(names, res))


def _split3(x):
    hi = x.astype(BF16)
    r1 = x - hi.astype(F32)
    mid = r1.astype(BF16)
    lo = (r1 - mid.astype(F32)).astype(BF16)
    return hi, mid, lo


def _tri_ones(n):
    r = lax.broadcasted_iota(I32, (n, n), 0)
    c = lax.broadcasted_iota(I32, (n, n), 1)
    return jnp.where(c <= r, 1.0, 0.0).astype(BF16)


def _tri_cumsum(tri, x):
    hi, mid, lo = _split3(x)
    dot = lambda a: jnp.dot(tri, a, preferred_element_type=F32)
    return dot(hi) + (dot(mid) + dot(lo))


def _page_group(n_pages, pref):
    g = min(n_pages, pref)
    while n_pages % g:
        g -= 1
    return g


def _paged_specs(G, n_pages, block):
    def spec(g):
        return pl.BlockSpec(block, lambda b, p, pt: (pt[b * n_pages + p * G + g], 0, 0))
    return [spec(g) for g in range(G)]


def _cumsum_body(pt_ref, *refs, G):
    lf_refs = refs[:G]
    new_ref, c_ref, cn_ref, carry_ref = refs[G:]
    p = pl.program_id(1)
    R = lf_refs[0].shape[0]

    @pl.when(p == 0)
    def _():
        carry_ref[...] = jnp.zeros_like(carry_ref)

    tri = _tri_ones(R)
    carry = carry_ref[...]
    for g in range(G):
        c = _tri_cumsum(tri, lf_refs[g][...]) + carry
        c_ref[g * R:(g + 1) * R, :] = c
        carry = c[-1:, :]
    carry_ref[...] = carry

    @pl.when(p == pl.num_programs(1) - 1)
    def _():
        nr = new_ref.shape[0]
        cn_ref[...] = _tri_cumsum(_tri_ones(nr), new_ref[...]) + carry


def _paged_cumsum(pool, page_table, new_rows, group=16):
    B, n_pages = page_table.shape
    R, H = pool.shape[1:]
    nr = new_rows.shape[1]
    G = _page_group(n_pages, group)
    grid_spec = pltpu.PrefetchScalarGridSpec(
        num_scalar_prefetch=1, grid=(B, n_pages // G),
        in_specs=_paged_specs(G, n_pages, (None, R, H)) + [
            pl.BlockSpec((None, nr, H), lambda b, p, pt: (b, 0, 0))],
        out_specs=[pl.BlockSpec((None, G * R, H), lambda b, p, pt: (b, p, 0)),
                   pl.BlockSpec((None, nr, H), lambda b, p, pt: (b, 0, 0))],
        scratch_shapes=[pltpu.VMEM((1, H), F32)])
    c, cn = pl.pallas_call(
        functools.partial(_cumsum_body, G=G), grid_spec=grid_spec,
        out_shape=[jax.ShapeDtypeStruct((B, n_pages * R, H), F32),
                   jax.ShapeDtypeStruct((B, nr, H), F32)],
        compiler_params=_cp(("parallel", "arbitrary")), name="paged_cumsum",
    )(page_table.reshape(-1).astype(I32), *([pool] * G), new_rows)
    return c, cn


def _softmax_step(s, vs, m_prev, l_prev, acc_prev):
    m_new = jnp.maximum(m_prev, jnp.max(s, axis=-1, keepdims=True))
    alpha = jnp.exp(m_prev - m_new)
    p = jnp.exp(s - m_new)
    l_new = alpha * l_prev + jnp.sum(p, axis=-1, keepdims=True)
    pb = p.astype(BF16)
    pv = None
    for g, v in enumerate(vs):
        d = jnp.dot(pb[:, g * LANES:(g + 1) * LANES], v, preferred_element_type=F32)
        pv = d if pv is None else pv + d
    return m_new, l_new, alpha * acc_prev + pv


def _causal_pairs(nq):
    qs, ks = [], []
    for qi in range(nq):
        for ki in range(qi + 1):
            qs.append(qi); ks.append(ki)
    return np.asarray(qs, np.int32), np.asarray(ks, np.int32)


def _init_softmax_state(m_ref, l_ref, acc_ref):
    m_ref[...] = jnp.full_like(m_ref, NEG)
    l_ref[...] = jnp.zeros_like(l_ref)
    acc_ref[...] = jnp.zeros_like(acc_ref)


def _tile_plan(t, diag):
    plan = []
    for r in range(t // QS):
        for c in range(t // KS):
            if diag and c * KS > r * QS + QS - 1:
                continue
            plan.append((r, c, diag and (c + 1) * KS - 1 > r * QS))
    return plan


def _attn_tiles_t(ks, qs, vts, ms, ls, accs, *, biases=None, masks=None, key_sub=None, qry_add=None):
    n = len(ks)
    ts = [_nt_dot(ks[i], qs[i]) for i in range(n)]
    if key_sub is not None:
        ts = [ts[i] - key_sub for i in range(n)]
    if biases is not None:
        ts = [ts[i] + biases[i] for i in range(n)]
    if masks is not None:
        ts = [ts[i] if masks[i] is None else jnp.where(masks[i], ts[i], NEG) for i in range(n)]
    mx = [jnp.max(ts[i], axis=0, keepdims=True) for i in range(n)]
    if qry_add is not None:
        mx = [mx[i] + qry_add[i] for i in range(n)]
    m_new = [jnp.maximum(ms[i], mx[i]) for i in range(n)]
    shift = [-m_new[i] if qry_add is None else qry_add[i] - m_new[i] for i in range(n)]
    ps = [jnp.exp2(ts[i] + shift[i]) for i in range(n)]
    alphas = [jnp.exp2(ms[i] - m_new[i]) for i in range(n)]
    l_new = [alphas[i] * ls[i] + jnp.sum(ps[i], axis=0, keepdims=True) for i in range(n)]
    pv = [jnp.dot(vts[i], ps[i].astype(BF16), preferred_element_type=F32) for i in range(n)]
    acc_new = [alphas[i] * accs[i] + pv[i] for i in range(n)]
    return m_new, l_new, acc_new


def _tile_causal_t(r, c):
    key = lax.broadcasted_iota(I32, (KS, QS), 0)
    qry = lax.broadcasted_iota(I32, (KS, QS), 1)
    return key + (c * KS - r * QS) <= qry


def _fox_prompt_body(qi_ref, ki_ref, q_ref, k_ref, vt_ref, cq_ref, ck_ref, o_ref, m_ref, l_ref, acc_ref, *, t):
    p = pl.program_id(0)
    qi, ki = qi_ref[p], ki_ref[p]

    @pl.when(ki == 0)
    def _():
        _init_softmax_state(m_ref, l_ref, acc_ref)

    def run(diag):
        plan = _tile_plan(t, diag)
        for h in range(FOX_HEADS):
            hd = slice(h * HEAD_DIM, (h + 1) * HEAD_DIM)
            qcs = [slice(r * QS, (r + 1) * QS) for r in range(t // QS)]
            qs = [q_ref[qc, hd] for qc in qcs]
            cq2 = [cq_ref[h:h + 1, qc] * LOG2E for qc in qcs]
            ms = [m_ref[h:h + 1, qc] for qc in qcs]
            ls = [l_ref[h:h + 1, qc] for qc in qcs]
            accs = [acc_ref[hd, qc] for qc in qcs]
            for c in range(t // KS):
                kc = slice(c * KS, (c + 1) * KS)
                act = [(r, msk) for r, cc, msk in plan if cc == c]
                ck2 = jnp.broadcast_to(ck_ref[kc, h:h + 1] * LOG2E, (KS, QS))
                for g0 in range(0, len(act), WIDTH):
                    grp = act[g0:g0 + WIDTH]
                    k = k_ref[kc, hd]
                    vt = vt_ref[hd, kc]
                    rs = [r for r, _ in grp]
                    mo, lo, ao = _attn_tiles_t(
                        [k] * len(grp), [qs[r] for r in rs], [vt] * len(grp),
                        [ms[r] for r in rs], [ls[r] for r in rs], [accs[r] for r in rs],
                        key_sub=ck2, qry_add=[cq2[r] for r in rs],
                        masks=[_tile_causal_t(r, c) if msk else None for r, msk in grp])
                    for i, r in enumerate(rs):
                        ms[r], ls[r], accs[r] = mo[i], lo[i], ao[i]
            for r, qc in enumerate(qcs):
                if diag:
                    o_ref[qc, hd] = (accs[r] / ls[r]).T.astype(BF16)
                else:
                    m_ref[h:h + 1, qc] = ms[r]
                    l_ref[h:h + 1, qc] = ls[r]
                    acc_ref[hd, qc] = accs[r]

    @pl.when(ki == qi)
    def _():
        run(True)

    @pl.when(ki != qi)
    def _():
        run(False)


def _fox_prompt(q, k, vt, c, t=512):
    T = q.shape[0]
    t = max(_tile(T, t), KS)
    qs, ks = _causal_pairs(T // t)
    grid_spec = pltpu.PrefetchScalarGridSpec(
        num_scalar_prefetch=2, grid=(len(qs),),
        in_specs=[pl.BlockSpec((t, FOX_W), lambda p, qi, ki: (qi[p], 0)),
                  pl.BlockSpec((t, FOX_W), lambda p, qi, ki: (ki[p], 0)),
                  pl.BlockSpec((FOX_W, t), lambda p, qi, ki: (0, ki[p])),
                  pl.BlockSpec((FOX_HEADS, t), lambda p, qi, ki: (0, qi[p])),
                  pl.BlockSpec((t, FOX_HEADS), lambda p, qi, ki: (ki[p], 0))],
        out_specs=pl.BlockSpec((t, FOX_W), lambda p, qi, ki: (qi[p], 0)),
        scratch_shapes=[pltpu.VMEM((FOX_HEADS, t), F32), pltpu.VMEM((FOX_HEADS, t), F32),
                        pltpu.VMEM((FOX_W, t), F32)])
    return pl.pallas_call(
        functools.partial(_fox_prompt_body, t=t), grid_spec=grid_spec,
        out_shape=jax.ShapeDtypeStruct((T, FOX_W), BF16),
        compiler_params=_cp(("arbitrary",)), name="fox_prompt",
    )(jnp.asarray(qs), jnp.asarray(ks), q, k, vt, c.T, c)


def _float_key(x):
    b = pltpu.bitcast(x, I32)
    return b ^ ((b >> 31) & jnp.int32(0x7FFFFFFF))


def _kth_key(keys_ref, nchunks, k, row0, rows):
    def count_ge(cand):
        def body(c, acc):
            kk = keys_ref[c, row0:row0 + rows, :]
            return acc + jnp.where(kk >= cand, 1.0, 0.0)
        acc = lax.fori_loop(0, nchunks, body, jnp.zeros((rows, LANES), F32))
        return jnp.sum(acc, axis=1, keepdims=True)

    def bit_body(i, t):
        cand = t + lax.shift_left(jnp.int32(1), 31 - i)
        return jnp.where(count_ge(cand) >= k, cand, t)

    return lax.fori_loop(0, 32, bit_body, jnp.full((rows, 1), INT_MIN, I32))


def _idx_prompt_body(q_ref, w_ref, ka_ref, kb_ref, keys_ref, thr_ref, *, tq, kc, n_sel, row_group):
    qi = pl.program_id(0)
    nck = keys_ref.shape[0]
    per = kc // LANES
    n_super = ((qi + 1) * tq + kc - 1) // kc

    def super_body(sc, carry):
        k0 = pl.multiple_of(sc * kc, kc)
        ka = ka_ref[pl.ds(k0, kc), :]
        kb = kb_ref[pl.ds(k0, kc), :]
        acc = jnp.zeros((tq, kc), F32)
        for j in range(IDX_QW // LANES):
            q2 = q_ref[:, j * LANES:(j + 1) * LANES]
            acc = acc + w_ref[:, 2 * j:2 * j + 1] * jnp.maximum(_nt_dot(q2, ka), 0.0)
            acc = acc + w_ref[:, 2 * j + 1:2 * j + 2] * jnp.maximum(_nt_dot(q2, kb), 0.0)
        qpos = qi * tq + lax.broadcasted_iota(I32, (tq, kc), 0)
        kpos = sc * kc + lax.broadcasted_iota(I32, (tq, kc), 1)
        key = jnp.where(kpos <= qpos, _float_key(acc), INT_MIN)
        for c in range(per):
            keys_ref[sc * per + c] = key[:, c * LANES:(c + 1) * LANES]
        return carry

    lax.fori_loop(0, n_super, super_body, 0)

    def fill_body(c, carry):
        keys_ref[c] = jnp.full((tq, LANES), INT_MIN, I32)
        return carry

    lax.fori_loop(n_super * per, nck, fill_body, 0)
    for rg in range(tq // row_group):
        thr_ref[rg * row_group:(rg + 1) * row_group, :] = _kth_key(
            keys_ref, n_super * per, n_sel, rg * row_group, row_group)


def _idx_prompt(qi_bf, wi, kia, kib, n_sel, tq=256, kc=512):
    T = qi_bf.shape[0]
    tq = _tile(T, tq)
    kc = _tile(T, kc)
    nck = T // LANES
    row_group = min(tq, 128)
    keys, thr = pl.pallas_call(
        functools.partial(_idx_prompt_body, tq=tq, kc=kc, n_sel=n_sel, row_group=row_group),
        grid=(T // tq,),
        in_specs=[pl.BlockSpec((tq, IDX_QW), lambda i: (i, 0)),
                  pl.BlockSpec((tq, IDX_HEADS), lambda i: (i, 0)),
                  pl.BlockSpec((T, LANES), lambda i: (0, 0)),
                  pl.BlockSpec((T, LANES), lambda i: (0, 0))],
        out_specs=[pl.BlockSpec((None, nck, tq, LANES), lambda i: (i, 0, 0, 0)),
                   pl.BlockSpec((tq, 1), lambda i: (i, 0))],
        out_shape=[jax.ShapeDtypeStruct((T // tq, nck, tq, LANES), I32),
                   jax.ShapeDtypeStruct((T, 1), I32)],
        compiler_params=_cp(("parallel",)), name="idx_prompt",
    )(qi_bf, wi, kia, kib)
    return keys, thr, tq


def _dsa_prompt_body(qi_ref, ki_ref, q_ref, k_ref, vt_ref, keys_ref, thr_ref, o_ref, m_ref, l_ref, acc_ref, *, t):
    p = pl.program_id(0)
    qi, ki = qi_ref[p], ki_ref[p]

    @pl.when(ki == 0)
    def _():
        _init_softmax_state(m_ref, l_ref, acc_ref)

    def run(diag):
        for r, c, msk in _tile_plan(t, diag):
            qc = slice(r * QS, (r + 1) * QS)
            kc = slice(c * KS, (c + 1) * KS)
            keys = jnp.concatenate([keys_ref[c * (KS // LANES) + j, qc, :] for j in range(KS // LANES)], axis=1)
            bias = jnp.where(keys >= thr_ref[qc, :], 0.0, NEG).T
            if msk:
                bias = jnp.where(_tile_causal_t(r, c), bias, NEG)
            for g in range(DSA_KV_HEADS):
                kv = slice(g * HEAD_DIM, (g + 1) * HEAD_DIM)
                hs = [g * DSA_GROUP + i for i in range(DSA_GROUP)]
                hds = [slice(h * HEAD_DIM, (h + 1) * HEAD_DIM) for h in hs]
                n = len(hs)
                mo, lo, ao = _attn_tiles_t(
                    [k_ref[kc, kv]] * n, [q_ref[qc, hd] for hd in hds], [vt_ref[kv, kc]] * n,
                    [m_ref[h:h + 1, qc] for h in hs], [l_ref[h:h + 1, qc] for h in hs],
                    [acc_ref[hd, qc] for hd in hds], biases=[bias] * n)
                for i, h in enumerate(hs):
                    m_ref[h:h + 1, qc] = mo[i]
                    l_ref[h:h + 1, qc] = lo[i]
                    acc_ref[hds[i], qc] = ao[i]

    @pl.when(ki == qi)
    def _():
        run(True)
        for h in range(DSA_HEADS):
            hd = slice(h * HEAD_DIM, (h + 1) * HEAD_DIM)
            o_ref[:, hd] = (acc_ref[hd, :] / l_ref[h:h + 1, :]).T.astype(BF16)

    @pl.when(ki != qi)
    def _():
        run(False)


def _dsa_prompt(q, k, vt, keys, thr, t):
    T = q.shape[0]
    qs, ks = _causal_pairs(T // t)
    per = t // LANES
    grid_spec = pltpu.PrefetchScalarGridSpec(
        num_scalar_prefetch=2, grid=(len(qs),),
        in_specs=[pl.BlockSpec((t, DSA_QW), lambda p, qi, ki: (qi[p], 0)),
                  pl.BlockSpec((t, DSA_KW), lambda p, qi, ki: (ki[p], 0)),
                  pl.BlockSpec((DSA_KW, t), lambda p, qi, ki: (0, ki[p])),
                  pl.BlockSpec((None, per, t, LANES), lambda p, qi, ki: (qi[p], ki[p], 0, 0)),
                  pl.BlockSpec((t, 1), lambda p, qi, ki: (qi[p], 0))],
        out_specs=pl.BlockSpec((t, DSA_QW), lambda p, qi, ki: (qi[p], 0)),
        scratch_shapes=[pltpu.VMEM((DSA_HEADS, t), F32), pltpu.VMEM((DSA_HEADS, t), F32),
                        pltpu.VMEM((DSA_QW, t), F32)])
    return pl.pallas_call(
        functools.partial(_dsa_prompt_body, t=t), grid_spec=grid_spec,
        out_shape=jax.ShapeDtypeStruct((T, DSA_QW), BF16),
        compiler_params=_cp(("arbitrary",)), name="dsa_prompt",
    )(jnp.asarray(qs), jnp.asarray(ks), q, k, vt, keys, thr)


TPAD = 8


def _head_rows(ref, h, n_heads):
    return ref[pl.ds(h, ref.shape[0] // n_heads, stride=n_heads), :].astype(BF16)


def _decode_step(qs, kget, vget, n_blocks, bias, state, rows_per_q):
    m_ref, l_ref, acc_ref = state
    n_q = len(qs)
    s = jnp.concatenate(
        [jnp.concatenate([_nt_dot(qs[j], kget(b, j)) for j in range(n_q)], axis=0) for b in range(n_blocks)],
        axis=1) * ATT_SCALE + bias
    m_prev = m_ref[...]
    m_new = jnp.maximum(m_prev, jnp.max(s, axis=-1, keepdims=True))
    alpha = jnp.exp(m_prev - m_new)
    p = jnp.exp(s - m_new)
    l_ref[...] = alpha * l_ref[...] + jnp.sum(p, axis=-1, keepdims=True)
    m_ref[...] = m_new
    pv = []
    for j in range(n_q):
        rows = slice(j * rows_per_q, (j + 1) * rows_per_q)
        d = None
        for b in range(n_blocks):
            t = jnp.dot(p[rows, b * LANES:(b + 1) * LANES].astype(BF16), vget(b, j), preferred_element_type=F32)
            d = t if d is None else d + t
        pv.append(d)
    acc_ref[...] = alpha * acc_ref[...] + jnp.concatenate(pv, axis=0)


def _fox_sample_body(pt_ref, q_ref, cq_ref, ck_ref, *refs, G):
    k_refs, v_refs = refs[:G], refs[G:2 * G]
    kn_ref, vn_ref, ckn_ref, o_ref, m_ref, l_ref, acc_ref = refs[2 * G:]
    p = pl.program_id(1)
    state = (m_ref, l_ref, acc_ref)

    @pl.when(p == 0)
    def _():
        _init_softmax_state(*state)

    qs = [q_ref[h].astype(BF16) for h in range(FOX_HEADS)]
    cq = cq_ref[...]

    def forget_bias(ck):
        return cq - jnp.concatenate(
            [jnp.broadcast_to(ck[h:h + 1, :], (TPAD, ck.shape[1])) for h in range(FOX_HEADS)], axis=0)

    _decode_step(qs, lambda b, h: _head_rows(k_refs[b], h, FOX_HEADS), lambda b, h: _head_rows(v_refs[b], h, FOX_HEADS),
                 G, forget_bias(ck_ref[...]), state, TPAD)

    @pl.when(p == pl.num_programs(1) - 1)
    def _():
        rows = FOX_HEADS * TPAD
        tok = lax.broadcasted_iota(I32, (rows, LANES), 0) % TPAD
        col = lax.broadcasted_iota(I32, (rows, LANES), 1)
        bias = jnp.where(col <= tok, forget_bias(ckn_ref[...]), NEG)
        _decode_step(qs, lambda b, h: kn_ref[h], lambda b, h: vn_ref[h], 1, bias, state, TPAD)
        o_ref[...] = acc_ref[...] / l_ref[...]


def _fox_sample(q, cq, ckT, cache_k, cache_v, page_table, knew, vnew, cknT, group=8):
    B, n_pages = page_table.shape
    R = cache_k.shape[1]
    page = R // FOX_HEADS
    rows = FOX_HEADS * TPAD
    G = _page_group(n_pages, group)
    per_b3 = lambda b, p, pt: (b, 0, 0)
    per_b4 = lambda b, p, pt: (b, 0, 0, 0)
    grid_spec = pltpu.PrefetchScalarGridSpec(
        num_scalar_prefetch=1, grid=(B, n_pages // G),
        in_specs=[pl.BlockSpec((None, FOX_HEADS, TPAD, HEAD_DIM), per_b4),
                  pl.BlockSpec((None, rows, 1), per_b3),
                  pl.BlockSpec((None, FOX_HEADS, G * page), lambda b, p, pt: (b, 0, p))]
        + _paged_specs(G, n_pages, (None, R, HEAD_DIM)) + _paged_specs(G, n_pages, (None, R, HEAD_DIM))
        + [pl.BlockSpec((None, FOX_HEADS, LANES, HEAD_DIM), per_b4),
           pl.BlockSpec((None, FOX_HEADS, LANES, HEAD_DIM), per_b4),
           pl.BlockSpec((None, FOX_HEADS, LANES), per_b3)],
        out_specs=pl.BlockSpec((None, rows, HEAD_DIM), per_b3),
        scratch_shapes=[pltpu.VMEM((rows, 1), F32), pltpu.VMEM((rows, 1), F32),
                        pltpu.VMEM((rows, HEAD_DIM), F32)])
    return pl.pallas_call(
        functools.partial(_fox_sample_body, G=G), grid_spec=grid_spec,
        out_shape=jax.ShapeDtypeStruct((B, rows, HEAD_DIM), F32),
        compiler_params=_cp(("parallel", "arbitrary")), name="fox_sample",
    )(page_table.reshape(-1).astype(I32), q, cq, ckT, *([cache_k] * G), *([cache_v] * G), knew, vnew, cknT)


def _idx_sample_body(pt_ref, q_ref, w_ref, *refs, ts, n_sel, G):
    k_refs = refs[:G]
    kn_ref, keys_ref, thr_ref = refs[G:]
    p = pl.program_id(1)
    n_pages = keys_ref.shape[0] - 1
    pad = jnp.full((TPAD - ts, LANES), INT_MIN, I32)

    def score(k):
        r = jnp.maximum(_nt_dot(q_ref[...], k), 0.0) * w_ref[...]
        return jnp.sum(r.reshape(ts, IDX_HEADS, LANES), axis=1)

    for g in range(G):
        keys_ref[p * G + g] = jnp.concatenate([_float_key(score(k_refs[g][...].astype(BF16))), pad], axis=0)

    @pl.when(p == pl.num_programs(1) - 1)
    def _():
        tok = lax.broadcasted_iota(I32, (ts, LANES), 0)
        col = lax.broadcasted_iota(I32, (ts, LANES), 1)
        kn = jnp.where(col <= tok, _float_key(score(kn_ref[...])), INT_MIN)
        keys_ref[n_pages] = jnp.concatenate([kn, pad], axis=0)
        thr_ref[...] = _kth_key(keys_ref, n_pages + 1, n_sel, 0, 8)


def _idx_sample(q, w, cache_k, page_table, knew, ts, n_sel, group=64):
    B, n_pages = page_table.shape
    R = cache_k.shape[1]
    rows = ts * IDX_HEADS
    G = _page_group(n_pages, group)
    per_b = lambda b, p, pt: (b, 0, 0)
    grid_spec = pltpu.PrefetchScalarGridSpec(
        num_scalar_prefetch=1, grid=(B, n_pages // G),
        in_specs=[pl.BlockSpec((None, rows, IDX_DIM), per_b),
                  pl.BlockSpec((None, rows, 1), per_b)]
        + _paged_specs(G, n_pages, (None, R, IDX_DIM))
        + [pl.BlockSpec((None, LANES, IDX_DIM), per_b)],
        out_specs=[pl.BlockSpec((None, n_pages + 1, 8, LANES), lambda b, p, pt: (b, 0, 0, 0)),
                   pl.BlockSpec((None, 8, 1), per_b)])
    return pl.pallas_call(
        functools.partial(_idx_sample_body, ts=ts, n_sel=n_sel, G=G), grid_spec=grid_spec,
        out_shape=[jax.ShapeDtypeStruct((B, n_pages + 1, 8, LANES), I32),
                   jax.ShapeDtypeStruct((B, 8, 1), I32)],
        compiler_params=_cp(("parallel", "arbitrary")), name="idx_sample",
    )(page_table.reshape(-1).astype(I32), q, w, *([cache_k] * G), knew)


def _dsa_sample_body(pt_ref, q_ref, keys_ref, keysn_ref, thr_ref, *refs, G):
    k_refs, v_refs = refs[:G], refs[G:2 * G]
    kn_ref, vn_ref, o_ref, m_ref, l_ref, acc_ref = refs[2 * G:]
    p = pl.program_id(1)
    state = (m_ref, l_ref, acc_ref)

    @pl.when(p == 0)
    def _():
        _init_softmax_state(*state)

    qs = [q_ref[g].astype(BF16) for g in range(DSA_KV_HEADS)]
    thr = thr_ref[...]

    def select_bias(keys):
        sel = jnp.concatenate([jnp.where(kk >= thr, 0.0, NEG) for kk in keys], axis=1)
        return jnp.concatenate([sel] * DSA_HEADS, axis=0)

    _decode_step(qs, lambda b, g: _head_rows(k_refs[b], g, DSA_KV_HEADS),
                 lambda b, g: _head_rows(v_refs[b], g, DSA_KV_HEADS),
                 G, select_bias([keys_ref[b] for b in range(G)]), state, DSA_GROUP * TPAD)

    @pl.when(p == pl.num_programs(1) - 1)
    def _():
        _decode_step(qs, lambda b, g: kn_ref[g], lambda b, g: vn_ref[g], 1, select_bias([keysn_ref[...]]),
                     state, DSA_GROUP * TPAD)
        o_ref[...] = acc_ref[...] / l_ref[...]


def _dsa_sample(q, keys, thr, cache_k, cache_v, page_table, knew, vnew, group=16):
    B, n_pages = page_table.shape
    R = cache_k.shape[1]
    rows = DSA_HEADS * TPAD
    G = _page_group(n_pages, group)
    per_b3 = lambda b, p, pt: (b, 0, 0)
    per_b4 = lambda b, p, pt: (b, 0, 0, 0)
    grid_spec = pltpu.PrefetchScalarGridSpec(
        num_scalar_prefetch=1, grid=(B, n_pages // G),
        in_specs=[pl.BlockSpec((None, DSA_KV_HEADS, DSA_GROUP * TPAD, HEAD_DIM), per_b4),
                  pl.BlockSpec((None, G, TPAD, LANES), lambda b, p, pt: (b, p, 0, 0)),
                  pl.BlockSpec((None, None, TPAD, LANES), lambda b, p, pt: (b, n_pages, 0, 0)),
                  pl.BlockSpec((None, TPAD, 1), per_b3)]
        + _paged_specs(G, n_pages, (None, R, HEAD_DIM)) + _paged_specs(G, n_pages, (None, R, HEAD_DIM))
        + [pl.BlockSpec((None, DSA_KV_HEADS, LANES, HEAD_DIM), per_b4),
           pl.BlockSpec((None, DSA_KV_HEADS, LANES, HEAD_DIM), per_b4)],
        out_specs=pl.BlockSpec((None, rows, HEAD_DIM), per_b3),
        scratch_shapes=[pltpu.VMEM((rows, 1), F32), pltpu.VMEM((rows, 1), F32),
                        pltpu.VMEM((rows, HEAD_DIM), F32)])
    return pl.pallas_call(
        functools.partial(_dsa_sample_body, G=G), grid_spec=grid_spec,
        out_shape=jax.ShapeDtypeStruct((B, rows, HEAD_DIM), F32),
        compiler_params=_cp(("parallel", "arbitrary")), name="dsa_sample",
    )(page_table.reshape(-1).astype(I32), q, keys, keys, thr, *([cache_k] * G), *([cache_v] * G), knew, vnew)


def _top_rows(x, n):
    out = []
    for _ in range(n):
        m = jnp.max(x, axis=0, keepdims=True)
        out.append(m)
        x = jnp.where(x == m, -jnp.inf, x)
    return out


def _peer_select_body(q_ref, keys_ref, s0_ref, s1_ref, st_ref, *, n_heads):
    nk = keys_ref.shape[2]
    thr, off = [], []
    for h in range(n_heads):
        tops = []
        for c, s_ref in enumerate((s0_ref, s1_ref)):
            col = (2 * h + c) * nk
            st = _nt_dot(q_ref[:, col:col + nk].astype(BF16), keys_ref[h, c]).T
            s_ref[h * nk:(h + 1) * nk, :] = st
            tops.append(jnp.concatenate(_top_rows(st, PEER_TOPK), axis=0))
        cand = jnp.concatenate([tops[0][r:r + 1, :] + tops[1] for r in range(PEER_TOPK)], axis=0)
        best = _top_rows(cand, PEER_TOPK)
        z = jnp.ones_like(best[0])
        for r in range(1, PEER_TOPK):
            z = z + jnp.exp(best[r] - best[0])
        thr.append(best[-1])
        off.append(-(best[0] + jnp.log(z)))
    st_ref[...] = jnp.concatenate(thr + off, axis=0)


def _peer_select(q, keys_bf, tm=128):
    M = q.shape[0]
    n_heads, _, nk, _ = keys_bf.shape
    tm = _tile(M, tm)
    col = lambda i: (0, i)
    return pl.pallas_call(
        functools.partial(_peer_select_body, n_heads=n_heads),
        grid=(M // tm,),
        in_specs=[pl.BlockSpec((tm, q.shape[1]), lambda i: (i, 0)),
                  pl.BlockSpec(keys_bf.shape, lambda i: (0, 0, 0, 0))],
        out_specs=[pl.BlockSpec((n_heads * nk, tm), col), pl.BlockSpec((n_heads * nk, tm), col),
                   pl.BlockSpec((2 * n_heads, tm), col)],
        out_shape=[jax.ShapeDtypeStruct((n_heads * nk, M), F32), jax.ShapeDtypeStruct((n_heads * nk, M), F32),
                   jax.ShapeDtypeStruct((2 * n_heads, M), F32)],
        compiler_params=_cp(("parallel",)), name="peer_select",
    )(q, keys_bf)


def _peer_expert_body(x_ref, u_ref, v_ref, s0_ref, s1_ref, st_ref, o_ref, *, n_heads, nk, ni):
    e = pl.program_id(1)

    @pl.when(e == 0)
    def _():
        o_ref[...] = jnp.zeros_like(o_ref)

    a = _nt_dot(u_ref[...], x_ref[...])
    act = 0.5 * a * (1.0 + lax.erf(a * (2.0 ** -0.5)))
    gates = []
    for il in range(ni):
        w = None
        for h in range(n_heads):
            tot = s0_ref[il * n_heads + h:il * n_heads + h + 1, :] + s1_ref[h * nk:(h + 1) * nk, :]
            val = jnp.where(tot >= st_ref[h:h + 1, :], jnp.exp(tot + st_ref[n_heads + h:n_heads + h + 1, :]), 0.0)
            w = val if w is None else w + val
        gates.append(w)
    ga = (jnp.concatenate(gates, axis=0) * act).astype(BF16)
    o_ref[...] += lax.dot_general(ga, v_ref[...], (((0,), (0,)), ((), ())), preferred_element_type=F32)


def _peer_expert(hn_bf, u_bf, v_bf, s0t, s1t, st, n_heads, nk, tm=512, te=256):
    M, D = hn_bf.shape
    E = u_bf.shape[0]
    tm = _tile(M, tm)
    te = _tile(E, te)
    ni = te // nk
    s0g = jnp.transpose(s0t.reshape(n_heads, E // te, ni, M), (1, 2, 0, 3)).reshape(E // te, ni * n_heads, M)
    return pl.pallas_call(
        functools.partial(_peer_expert_body, n_heads=n_heads, nk=nk, ni=ni),
        grid=(M // tm, E // te),
        in_specs=[pl.BlockSpec((tm, D), lambda i, e: (i, 0)),
                  pl.BlockSpec((te, D), lambda i, e: (e, 0)),
                  pl.BlockSpec((te, D), lambda i, e: (e, 0)),
                  pl.BlockSpec((None, ni * n_heads, tm), lambda i, e: (e, 0, i)),
                  pl.BlockSpec((n_heads * nk, tm), lambda i, e: (0, i)),
                  pl.BlockSpec((2 * n_heads, tm), lambda i, e: (0, i))],
        out_specs=pl.BlockSpec((tm, D), lambda i, e: (i, 0)),
        out_shape=jax.ShapeDtypeStruct((M, D), F32),
        compiler_params=_cp(("parallel", "arbitrary")), name="peer_expert",
    )(hn_bf, u_bf, v_bf, s0g, s1t, st)


def _prep_w_in(w_in):
    o = np.cumsum([0, FOX_W, FOX_W, FOX_W, FOX_HEADS, DSA_QW, DSA_KW, DSA_KW, IDX_QW, IDX_DIM, IDX_HEADS])
    seg = lambda i: w_in[:, o[i]:o[i + 1]]
    cols = [seg(0), seg(1), seg(2), seg(4), seg(7), seg(5), seg(6), seg(8), seg(9), seg(3)]
    used = int(o[-1])
    total = -(-used // 768) * 768
    cols.append(jnp.zeros((w_in.shape[0], total - used), w_in.dtype))
    return jnp.concatenate(cols, axis=1).astype(BF16)


def _channel(x, o_fox, o_dsa, p_l, w_o_bf, g_norm2, w_pq_bf, keys_bf, u_bf, v_bf, g_norm3, w_gate_bf, w_proj_bf,
             g_ple):
    n_heads, _, nk, _ = keys_bf.shape
    h1 = _mm2_res(o_fox, o_dsa, w_o_bf, x)
    q, hn_bf = _mm(h1, w_pq_bf, g=g_norm2, emit_xn=True)
    s0t, s1t, st = _peer_select(q, keys_bf)
    moe = _peer_expert(hn_bf, u_bf, v_bf, s0t, s1t, st, n_heads, nk)
    pn = _rownorm_mm(p_l, w_proj_bf, g_ple)
    return _mm(h1, w_gate_bf, g=g_norm3, x2=moe, res=h1, res2=moe, aux=pn, epi="gate", tm=256)


def _head_major(x, B, ts, n_heads, n_rows):
    x4 = jnp.transpose(x.reshape(B, ts, n_heads, HEAD_DIM), (0, 2, 1, 3))
    return jnp.pad(x4, ((0, 0), (0, 0), (0, n_rows - ts), (0, 0)))


def _token_major(o, B, ts, n_heads):
    o4 = o.reshape(B, n_heads, TPAD, HEAD_DIM)[:, :, :ts]
    return jnp.transpose(o4, (0, 2, 1, 3)).reshape(B * ts, n_heads * HEAD_DIM)


def _pad_rows(x, n):
    return jnp.pad(x, ((0, 0), (0, n - x.shape[1]), (0, 0)))


def kernel(x_prompt, x_sample, cache_fox_k, cache_fox_v, cache_fox_logf, cache_dsa_k, cache_dsa_v, cache_idx_k, page_table, p_prompt, p_sample, g_norm1, w_in, b_f, g_q_fox, g_k_fox, g_q_dsa, g_k_dsa, w_o, g_norm2, w_peer_q, peer_keys, peer_u, peer_v, g_norm3, w_ple_gate, w_ple_proj, g_ple):
    Bp, Tp, D = x_prompt.shape
    Bs, Ts = x_sample.shape[:2]
    depth = w_in.shape[0]
    n_pages = page_table.shape[1]
    page = cache_fox_k.shape[2]
    past = n_pages * page
    n_phys = cache_fox_k.shape[1]
    assert Bp == 1 and Ts <= TPAD and page == LANES

    def pool(cache):
        return cache.reshape((depth * n_phys, -1) + cache.shape[-1:])

    fox_k_pool, fox_v_pool, dsa_k_pool, dsa_v_pool = pool(cache_fox_k), pool(cache_fox_v), pool(cache_dsa_k), pool(cache_dsa_v)
    fox_lf_pool = cache_fox_logf.reshape(depth * n_phys, page, FOX_HEADS)
    idx_k_pool = cache_idx_k.reshape(depth * n_phys, page, IDX_DIM)

    h_p = x_prompt.reshape(Tp, D)
    h_s = x_sample.reshape(Bs * Ts, D)
    pos_p = jnp.arange(Tp, dtype=I32)
    pos_s = jnp.tile(past + jnp.arange(Ts, dtype=I32), Bs)
    outs = [[] for _ in range(12)]
    for l in range(depth):
        w_in_bf = _prep_w_in(w_in[l])
        chan_w = (w_o[l].astype(BF16), g_norm2[l], w_peer_q[l].astype(BF16), peer_keys[l].astype(BF16),
                  peer_u[l].astype(BF16), peer_v[l].astype(BF16), g_norm3[l], w_ple_gate[l].astype(BF16),
                  w_ple_proj[l].astype(BF16), g_ple[l])
        norm_w = (b_f[l], g_q_fox[l], g_k_fox[l], g_q_dsa[l], g_k_dsa[l])
        pt_l = page_table + l * n_phys

        z = _mm(h_p, w_in_bf, g=g_norm1[l], tn=768)
        pr = _post(z, pos_p, *norm_w, ATT_SCALE * LOG2E)
        ident = jnp.arange(Tp // page, dtype=I32).reshape(1, -1)
        c, _ = _paged_cumsum(pr["lf"].reshape(Tp // page, page, FOX_HEADS), ident,
                             jnp.zeros((1, 8, FOX_HEADS), F32))
        o_fox = _fox_prompt(pr["qf"], pr["kfb"], pr["vft"], c[0])
        keys, thr, tq = _idx_prompt(pr["qi"], pr["wi"], pr["kia"], pr["kib"], min(TOPK_MAX, Tp // 4))
        o_dsa = _dsa_prompt(pr["qd"], pr["kdb"], pr["vdt"], keys, thr, tq)
        h_p = _channel(h_p, o_fox, o_dsa, p_prompt[l].reshape(Tp, -1), *chan_w)
        for i, (name, shape) in enumerate((("kf", (Bp, Tp, FOX_HEADS, HEAD_DIM)), ("vf", (Bp, Tp, FOX_HEADS, HEAD_DIM)),
                                           ("lf", (Bp, Tp, FOX_HEADS)), ("kd", (Bp, Tp, DSA_KV_HEADS, HEAD_DIM)),
                                           ("vd", (Bp, Tp, DSA_KV_HEADS, HEAD_DIM)), ("ki", (Bp, Tp, IDX_DIM)))):
            outs[i].append(pr[name].reshape(shape))

        zs = _mm(h_s, w_in_bf, g=g_norm1[l], tn=768)
        sr = _post(zs, pos_s, *norm_w, 1.0)
        lf_new = _pad_rows(sr["lf"].reshape(Bs, Ts, FOX_HEADS), TPAD)
        c_past, c_new = _paged_cumsum(fox_lf_pool, pt_l, lf_new)
        cq = jnp.transpose(c_new, (0, 2, 1)).reshape(Bs, FOX_HEADS * TPAD, 1)
        ckT = jnp.transpose(c_past, (0, 2, 1))
        cknT = jnp.transpose(_pad_rows(c_new, LANES), (0, 2, 1))
        o = _fox_sample(_head_major(sr["qf"], Bs, Ts, FOX_HEADS, TPAD).astype(F32), cq, ckT, fox_k_pool, fox_v_pool, pt_l,
                        _head_major(sr["kfb"], Bs, Ts, FOX_HEADS, LANES), _head_major(sr["vfb"], Bs, Ts, FOX_HEADS, LANES),
                        cknT)
        o_fox_s = _token_major(o, Bs, Ts, FOX_HEADS)
        n_sel = min(TOPK_MAX, (past + Ts) // 4)
        keys_s, thr_s = _idx_sample(sr["qi"].reshape(Bs, Ts * IDX_HEADS, IDX_DIM),
                                    sr["wi"].reshape(Bs, Ts * IDX_HEADS, 1), idx_k_pool, pt_l,
                                    _pad_rows(sr["kia"][:, :IDX_DIM].reshape(Bs, Ts, IDX_DIM), LANES), Ts, n_sel)
        qd = _head_major(sr["qd"], Bs, Ts, DSA_HEADS, TPAD).astype(F32)
        od = _dsa_sample(qd.reshape(Bs, DSA_KV_HEADS, DSA_GROUP * TPAD, HEAD_DIM), keys_s, thr_s,
                         dsa_k_pool, dsa_v_pool, pt_l, _head_major(sr["kdb"], Bs, Ts, DSA_KV_HEADS, LANES),
                         _head_major(sr["vdb"], Bs, Ts, DSA_KV_HEADS, LANES))
        o_dsa_s = _token_major(od, Bs, Ts, DSA_HEADS)
        h_s = _channel(h_s, o_fox_s.astype(BF16), o_dsa_s.astype(BF16), p_sample[l].reshape(Bs * Ts, -1), *chan_w)
        for i, (name, shape) in enumerate((("kf", (Bs, Ts, FOX_HEADS, HEAD_DIM)), ("vf", (Bs, Ts, FOX_HEADS, HEAD_DIM)),
                                           ("lf", (Bs, Ts, FOX_HEADS)), ("kd", (Bs, Ts, DSA_KV_HEADS, HEAD_DIM)),
                                           ("vd", (Bs, Ts, DSA_KV_HEADS, HEAD_DIM)), ("ki", (Bs, Ts, IDX_DIM)))):
            outs[6 + i].append(sr[name].reshape(shape))
    return (h_p.reshape(Bp, Tp, D), h_s.reshape(Bs, Ts, D)) + tuple(jnp.stack(o) for o in outs)
```

```python
import functools
import math

import numpy as np
import jax
import jax.numpy as jnp
from jax import lax
from jax.experimental import pallas as pl
from jax.experimental.pallas import tpu as pltpu

F32 = jnp.float32
BF16 = jnp.bfloat16
I32 = jnp.int32

HEAD_DIM = 128
FOX_HEADS = 16
DSA_HEADS = 16
DSA_KV_HEADS = 4
DSA_GROUP = DSA_HEADS // DSA_KV_HEADS
IDX_HEADS = 32
IDX_DIM = 64
TOPK_MAX = 256
ROPE_THETA = 10000.0
PEER_TOPK = 16
EPS = 1e-6

FOX_W = FOX_HEADS * HEAD_DIM
DSA_QW = DSA_HEADS * HEAD_DIM
DSA_KW = DSA_KV_HEADS * HEAD_DIM
IDX_QW = IDX_HEADS * IDX_DIM

LANES = 128
NEG = -1e30
INT_MIN = -(2 ** 31)
VMEM_LIMIT = 56 * 1024 * 1024
ATT_SCALE = HEAD_DIM ** -0.5
LOG2E = math.log2(math.e)
QS = 128
KS = 256
WIDTH = 4


def _cp(sem):
    return pltpu.CompilerParams(dimension_semantics=sem, vmem_limit_bytes=VMEM_LIMIT)


def _tile(n, pref):
    t = min(n, pref)
    while n % t:
        t //= 2
    return t


def _nt_dot(a, b):
    return lax.dot_general(a, b, (((1,), (1,)), ((), ())), preferred_element_type=F32)


def _mm_body(*refs, norm, add2, epi, emit_xn, prologue):
    it = iter(refs)
    x_ref = next(it)
    x2_ref = next(it) if add2 else None
    g_ref = next(it) if norm else None
    w_ref = next(it)
    res_ref = next(it) if epi in ("res", "gate") else None
    res2_ref = next(it) if (epi == "gate" and add2) else None
    aux_ref = next(it) if epi == "gate" else None
    o_ref = next(it)
    xo_ref = next(it) if emit_xn else None
    xn_ref = next(it) if prologue else None

    if prologue:
        @pl.when(pl.program_id(1) == 0)
        def _():
            x = x_ref[...].astype(F32)
            if add2:
                x = x + x2_ref[...]
            if norm:
                ms = jnp.mean(x * x, axis=-1, keepdims=True)
                x = x * lax.rsqrt(ms + EPS) * g_ref[...]
            xb = x.astype(BF16)
            xn_ref[...] = xb
            if emit_xn:
                xo_ref[...] = xb
        lhs = xn_ref[...]
    else:
        lhs = x_ref[...]
    acc = jnp.dot(lhs, w_ref[...], preferred_element_type=F32)
    if epi == "res":
        acc = res_ref[...] + acc
    elif epi == "gate":
        r = res_ref[...]
        if add2:
            r = r + res2_ref[...]
        acc = r + jax.nn.sigmoid(acc) * aux_ref[...]
    o_ref[...] = acc


def _mm(x, w, *, g=None, x2=None, res=None, res2=None, aux=None, epi="none", emit_xn=False,
        tm=512, tn=512):
    M, K = x.shape
    N = w.shape[1]
    tm = _tile(M, tm)
    tn = _tile(N, tn)
    norm = g is not None
    add2 = x2 is not None
    prologue = norm or add2 or x.dtype != BF16
    row = pl.BlockSpec((tm, K), lambda i, j: (i, 0))
    blk = pl.BlockSpec((tm, tn), lambda i, j: (i, j))
    ins, specs = [x], [row]
    if add2:
        ins.append(x2); specs.append(row)
    if norm:
        ins.append(g.reshape(1, K)); specs.append(pl.BlockSpec((1, K), lambda i, j: (0, 0)))
    ins.append(w); specs.append(pl.BlockSpec((K, tn), lambda i, j: (0, j)))
    if epi in ("res", "gate"):
        ins.append(res); specs.append(blk)
    if epi == "gate" and add2:
        ins.append(res2); specs.append(blk)
    if epi == "gate":
        ins.append(aux); specs.append(blk)
    out_shape = [jax.ShapeDtypeStruct((M, N), F32)]
    out_specs = [blk]
    if emit_xn:
        out_shape.append(jax.ShapeDtypeStruct((M, K), BF16))
        out_specs.append(row)
    scratch = [pltpu.VMEM((tm, K), BF16)] if prologue else []
    outs = pl.pallas_call(
        functools.partial(_mm_body, norm=norm, add2=add2, epi=epi, emit_xn=emit_xn, prologue=prologue),
        grid=(M // tm, N // tn),
        in_specs=specs, out_specs=out_specs, out_shape=out_shape, scratch_shapes=scratch,
        compiler_params=_cp(("parallel", "arbitrary")), name="mm_" + epi,
    )(*ins)
    return outs if emit_xn else outs[0]


def _mm2_res_body(x1_ref, x2_ref, w1_ref, w2_ref, res_ref, o_ref):
    o_ref[...] = res_ref[...] + (jnp.dot(x1_ref[...], w1_ref[...], preferred_element_type=F32)
                                 + jnp.dot(x2_ref[...], w2_ref[...], preferred_element_type=F32))


def _mm2_res(x1, x2, w, res, tm=512, tn=512):
    M, K1 = x1.shape
    N = w.shape[1]
    assert x2.shape == (M, K1) and w.shape[0] == 2 * K1
    tm = _tile(M, tm)
    tn = _tile(N, tn)
    row = pl.BlockSpec((tm, K1), lambda i, j: (i, 0))
    blk = pl.BlockSpec((tm, tn), lambda i, j: (i, j))
    return pl.pallas_call(
        _mm2_res_body,
        grid=(M // tm, N // tn),
        in_specs=[row, row, pl.BlockSpec((K1, tn), lambda i, j: (0, j)), pl.BlockSpec((K1, tn), lambda i, j: (1, j)), blk],
        out_specs=blk, out_shape=jax.ShapeDtypeStruct((M, N), F32),
        compiler_params=_cp(("parallel", "arbitrary")), name="mm2_res",
    )(x1, x2, w, w, res)


def _rownorm_mm_body(x_ref, w_ref, g_ref, o_ref):
    y = jnp.dot(x_ref[...].astype(BF16), w_ref[...], preferred_element_type=F32)
    ms = jnp.mean(y * y, axis=-1, keepdims=True)
    o_ref[...] = y * lax.rsqrt(ms + EPS) * g_ref[...]


def _rownorm_mm(x, w, g, tm=256):
    M, K = x.shape
    N = w.shape[1]
    tm = _tile(M, tm)
    return pl.pallas_call(
        _rownorm_mm_body,
        grid=(M // tm,),
        in_specs=[pl.BlockSpec((tm, K), lambda i: (i, 0)),
                  pl.BlockSpec((K, N), lambda i: (0, 0)),
                  pl.BlockSpec((1, N), lambda i: (0, 0))],
        out_specs=pl.BlockSpec((tm, N), lambda i: (i, 0)),
        out_shape=jax.ShapeDtypeStruct((M, N), F32),
        compiler_params=_cp(("parallel",)), name="rownorm_mm",
    )(x, w, g.reshape(1, N))


def _head_norm(x, g):
    ms = jnp.mean(x * x, axis=-1, keepdims=True)
    return x * lax.rsqrt(ms + EPS) * g


def _rope_full(x, cos, sin):
    return x * cos + pltpu.roll(x, HEAD_DIM // 2, 1) * sin


def _rope_pair(x, cos, sin, lane):
    half = IDX_DIM // 2
    rot = jnp.where((lane % IDX_DIM) < half, pltpu.roll(x, LANES - half, 1), pltpu.roll(x, half, 1))
    return x * cos + rot * sin


def _post_body(zqf, zkf, zvf, zqd, zqi, zkv, zs, cd, sd, ci, si, bf, gqf, gkf, gqd, gkd,
               qf_o, kf_o, kfb_o, vf_o, vfb_o, lf_o, qd_o, kd_o, kdb_o, vd_o, vdb_o,
               qi_o, ki_o, kia_o, kib_o, wi_o, vft_o, vdt_o, *, q_scale):
    tm = zqf.shape[0]
    cos_d, sin_d = cd[...], sd[...]
    cos_i, sin_i = ci[...], si[...]
    lane = lax.broadcasted_iota(I32, (tm, LANES), 1)
    for h in range(FOX_HEADS):
        sl = slice(h * HEAD_DIM, (h + 1) * HEAD_DIM)
        qf_o[:, sl] = (_head_norm(zqf[:, sl], gqf[...]) * q_scale).astype(BF16)
        k = _head_norm(zkf[:, sl], gkf[...])
        kf_o[:, sl] = k
        kfb_o[:, sl] = k.astype(BF16)
        v = zvf[:, sl]
        vf_o[:, sl] = v
        vfb_o[:, sl] = v.astype(BF16)
        vft_o[sl, :] = v.T.astype(BF16)
    for h in range(DSA_HEADS):
        sl = slice(h * HEAD_DIM, (h + 1) * HEAD_DIM)
        qd_o[:, sl] = (_rope_full(_head_norm(zqd[:, sl], gqd[...]), cos_d, sin_d) * q_scale).astype(BF16)
    for h in range(DSA_KV_HEADS):
        sl = slice(h * HEAD_DIM, (h + 1) * HEAD_DIM)
        k = _rope_full(_head_norm(zkv[:, sl], gkd[...]), cos_d, sin_d)
        kd_o[:, sl] = k
        kdb_o[:, sl] = k.astype(BF16)
        v = zkv[:, DSA_KW + h * HEAD_DIM: DSA_KW + (h + 1) * HEAD_DIM]
        vd_o[:, sl] = v
        vdb_o[:, sl] = v.astype(BF16)
        vdt_o[sl, :] = v.T.astype(BF16)
    for j in range(IDX_QW // LANES):
        sl = slice(j * LANES, (j + 1) * LANES)
        qi_o[:, sl] = _rope_pair(zqi[:, sl], cos_i, sin_i, lane).astype(BF16)
    x = zs[...]
    y = _rope_pair(x, cos_i, sin_i, lane)
    ki_o[...] = y[:, :IDX_DIM]
    ka = jnp.where(lane < IDX_DIM, y, 0.0)
    kia_o[...] = ka.astype(BF16)
    kib_o[...] = pltpu.roll(ka, IDX_DIM, 1).astype(BF16)
    wi_o[...] = x[:, IDX_DIM:IDX_DIM + IDX_HEADS] * (IDX_HEADS ** -0.5 * IDX_DIM ** -0.5)
    fl = x[:, IDX_DIM + IDX_HEADS:IDX_DIM + IDX_HEADS + FOX_HEADS] + bf[...]
    lf_o[...] = jnp.minimum(fl, 0.0) - jnp.log1p(jnp.exp(-jnp.abs(fl)))


def _post(z, pos, b_f, g_q_fox, g_k_fox, g_q_dsa, g_k_dsa, q_scale, tm=128):
    M = z.shape[0]
    tm = _tile(M, tm)
    posf = pos.astype(F32)[:, None]
    hd = HEAD_DIM // 2
    ang = posf * (ROPE_THETA ** (-jnp.arange(hd, dtype=F32) * 2.0 / HEAD_DIM))
    cd = jnp.concatenate([jnp.cos(ang), jnp.cos(ang)], axis=-1)
    sd = jnp.concatenate([-jnp.sin(ang), jnp.sin(ang)], axis=-1)
    hi = IDX_DIM // 2
    angi = posf * (ROPE_THETA ** (-jnp.arange(hi, dtype=F32) * 2.0 / IDX_DIM))
    ci = jnp.tile(jnp.cos(angi), (1, 4))
    si = jnp.tile(jnp.concatenate([-jnp.sin(angi), jnp.sin(angi)], axis=-1), (1, 2))

    def zspec(w, c):
        return pl.BlockSpec((tm, w), lambda i: (i, c))

    def rspec(w):
        return pl.BlockSpec((tm, w), lambda i: (i, 0))

    def cspec(w):
        return pl.BlockSpec((1, w), lambda i: (0, 0))

    small_col = (5 * FOX_W + 2 * DSA_KW) // LANES
    in_specs = [zspec(FOX_W, 0), zspec(FOX_W, 1), zspec(FOX_W, 2), zspec(DSA_QW, 3), zspec(IDX_QW, 4),
                zspec(2 * DSA_KW, 5 * FOX_W // (2 * DSA_KW)), zspec(LANES, small_col),
                rspec(LANES), rspec(LANES), rspec(LANES), rspec(LANES),
                cspec(FOX_HEADS), cspec(HEAD_DIM), cspec(HEAD_DIM), cspec(HEAD_DIM), cspec(HEAD_DIM)]
    outs = [(FOX_W, BF16), (FOX_W, F32), (FOX_W, BF16), (FOX_W, F32), (FOX_W, BF16), (FOX_HEADS, F32),
            (DSA_QW, BF16), (DSA_KW, F32), (DSA_KW, BF16), (DSA_KW, F32), (DSA_KW, BF16),
            (IDX_QW, BF16), (IDX_DIM, F32), (LANES, BF16), (LANES, BF16), (IDX_HEADS, F32)]
    tspec = lambda w: pl.BlockSpec((w, tm), lambda i: (0, i))
    res = pl.pallas_call(
        functools.partial(_post_body, q_scale=q_scale),
        grid=(M // tm,),
        in_specs=in_specs,
        out_specs=[rspec(w) for w, _ in outs] + [tspec(FOX_W), tspec(DSA_KW)],
        out_shape=[jax.ShapeDtypeStruct((M, w), d) for w, d in outs]
        + [jax.ShapeDtypeStruct((FOX_W, M), BF16), jax.ShapeDtypeStruct((DSA_KW, M), BF16)],
        compiler_params=_cp(("parallel",)), name="post_proj",
    )(z, z, z, z, z, z, z, cd, sd, ci, si, b_f.reshape(1, -1), g_q_fox.reshape(1, -1),
      g_k_fox.reshape(1, -1), g_q_dsa.reshape(1, -1), g_k_dsa.reshape(1, -1))
    names = ("qf", "kf", "kfb", "vf", "vfb", "lf", "qd", "kd", "kdb", "vd", "vdb", "qi", "ki", "kia",
             "kib", "wi", "vft", "vdt")
    return dict(zip(names, res))


def _split3(x):
    hi = x.astype(BF16)
    r1 = x - hi.astype(F32)
    mid = r1.astype(BF16)
    lo = (r1 - mid.astype(F32)).astype(BF16)
    return hi, mid, lo


def _tri_ones(n):
    r = lax.broadcasted_iota(I32, (n, n), 0)
    c = lax.broadcasted_iota(I32, (n, n), 1)
    return jnp.where(c <= r, 1.0, 0.0).astype(BF16)


def _tri_cumsum(tri, x):
    hi, mid, lo = _split3(x)
    dot = lambda a: jnp.dot(tri, a, preferred_element_type=F32)
    return dot(hi) + (dot(mid) + dot(lo))


def _page_group(n_pages, pref):
    g = min(n_pages, pref)
    while n_pages % g:
        g -= 1
    return g


def _paged_specs(G, n_pages, block):
    def spec(g):
        return pl.BlockSpec(block, lambda b, p, pt: (pt[b * n_pages + p * G + g], 0, 0))
    return [spec(g) for g in range(G)]


def _cumsum_body(pt_ref, *refs, G):
    lf_refs = refs[:G]
    new_ref, c_ref, cn_ref, carry_ref = refs[G:]
    p = pl.program_id(1)
    R = lf_refs[0].shape[0]

    @pl.when(p == 0)
    def _():
        carry_ref[...] = jnp.zeros_like(carry_ref)

    tri = _tri_ones(R)
    carry = carry_ref[...]
    for g in range(G):
        c = _tri_cumsum(tri, lf_refs[g][...]) + carry
        c_ref[g * R:(g + 1) * R, :] = c
        carry = c[-1:, :]
    carry_ref[...] = carry

    @pl.when(p == pl.num_programs(1) - 1)
    def _():
        nr = new_ref.shape[0]
        cn_ref[...] = _tri_cumsum(_tri_ones(nr), new_ref[...]) + carry


def _paged_cumsum(pool, page_table, new_rows, group=16):
    B, n_pages = page_table.shape
    R, H = pool.shape[1:]
    nr = new_rows.shape[1]
    G = _page_group(n_pages, group)
    grid_spec = pltpu.PrefetchScalarGridSpec(
        num_scalar_prefetch=1, grid=(B, n_pages // G),
        in_specs=_paged_specs(G, n_pages, (None, R, H)) + [
            pl.BlockSpec((None, nr, H), lambda b, p, pt: (b, 0, 0))],
        out_specs=[pl.BlockSpec((None, G * R, H), lambda b, p, pt: (b, p, 0)),
                   pl.BlockSpec((None, nr, H), lambda b, p, pt: (b, 0, 0))],
        scratch_shapes=[pltpu.VMEM((1, H), F32)])
    c, cn = pl.pallas_call(
        functools.partial(_cumsum_body, G=G), grid_spec=grid_spec,
        out_shape=[jax.ShapeDtypeStruct((B, n_pages * R, H), F32),
                   jax.ShapeDtypeStruct((B, nr, H), F32)],
        compiler_params=_cp(("parallel", "arbitrary")), name="paged_cumsum",
    )(page_table.reshape(-1).astype(I32), *([pool] * G), new_rows)
    return c, cn


def _softmax_step(s, vs, m_prev, l_prev, acc_prev):
    m_new = jnp.maximum(m_prev, jnp.max(s, axis=-1, keepdims=True))
    alpha = jnp.exp(m_prev - m_new)
    p = jnp.exp(s - m_new)
    l_new = alpha * l_prev + jnp.sum(p, axis=-1, keepdims=True)
    pb = p.astype(BF16)
    pv = None
    for g, v in enumerate(vs):
        d = jnp.dot(pb[:, g * LANES:(g + 1) * LANES], v, preferred_element_type=F32)
        pv = d if pv is None else pv + d
    return m_new, l_new, alpha * acc_prev + pv


def _causal_pairs(nq):
    qs, ks = [], []
    for qi in range(nq):
        for ki in range(qi + 1):
            qs.append(qi); ks.append(ki)
    return np.asarray(qs, np.int32), np.asarray(ks, np.int32)


def _init_softmax_state(m_ref, l_ref, acc_ref):
    m_ref[...] = jnp.full_like(m_ref, NEG)
    l_ref[...] = jnp.zeros_like(l_ref)
    acc_ref[...] = jnp.zeros_like(acc_ref)


def _tile_plan(t, diag):
    plan = []
    for r in range(t // QS):
        for c in range(t // KS):
            if diag and c * KS > r * QS + QS - 1:
                continue
            plan.append((r, c, diag and (c + 1) * KS - 1 > r * QS))
    return plan


def _attn_tiles_t(ks, qs, vts, ms, ls, accs, *, biases=None, masks=None, key_sub=None, qry_add=None):
    n = len(ks)
    ts = [_nt_dot(ks[i], qs[i]) for i in range(n)]
    if key_sub is not None:
        ts = [ts[i] - key_sub for i in range(n)]
    if biases is not None:
        ts = [ts[i] + biases[i] for i in range(n)]
    if masks is not None:
        ts = [ts[i] if masks[i] is None else jnp.where(masks[i], ts[i], NEG) for i in range(n)]
    mx = [jnp.max(ts[i], axis=0, keepdims=True) for i in range(n)]
    if qry_add is not None:
        mx = [mx[i] + qry_add[i] for i in range(n)]
    m_new = [jnp.maximum(ms[i], mx[i]) for i in range(n)]
    shift = [-m_new[i] if qry_add is None else qry_add[i] - m_new[i] for i in range(n)]
    ps = [jnp.exp2(ts[i] + shift[i]) for i in range(n)]
    alphas = [jnp.exp2(ms[i] - m_new[i]) for i in range(n)]
    l_new = [alphas[i] * ls[i] + jnp.sum(ps[i], axis=0, keepdims=True) for i in range(n)]
    pv = [jnp.dot(vts[i], ps[i].astype(BF16), preferred_element_type=F32) for i in range(n)]
    acc_new = [alphas[i] * accs[i] + pv[i] for i in range(n)]
    return m_new, l_new, acc_new


def _tile_causal_t(r, c):
    key = lax.broadcasted_iota(I32, (KS, QS), 0)
    qry = lax.broadcasted_iota(I32, (KS, QS), 1)
    return key + (c * KS - r * QS) <= qry


def _fox_prompt_body(qi_ref, ki_ref, q_ref, k_ref, vt_ref, cq_ref, ck_ref, o_ref, m_ref, l_ref, acc_ref, *, t):
    p = pl.program_id(0)
    qi, ki = qi_ref[p], ki_ref[p]

    @pl.when(ki == 0)
    def _():
        _init_softmax_state(m_ref, l_ref, acc_ref)

    def run(diag):
        plan = _tile_plan(t, diag)
        for h in range(FOX_HEADS):
            hd = slice(h * HEAD_DIM, (h + 1) * HEAD_DIM)
            qcs = [slice(r * QS, (r + 1) * QS) for r in range(t // QS)]
            qs = [q_ref[qc, hd] for qc in qcs]
            cq2 = [cq_ref[h:h + 1, qc] * LOG2E for qc in qcs]
            ms = [m_ref[h:h + 1, qc] for qc in qcs]
            ls = [l_ref[h:h + 1, qc] for qc in qcs]
            accs = [acc_ref[hd, qc] for qc in qcs]
            for c in range(t // KS):
                kc = slice(c * KS, (c + 1) * KS)
                act = [(r, msk) for r, cc, msk in plan if cc == c]
                ck2 = jnp.broadcast_to(ck_ref[kc, h:h + 1] * LOG2E, (KS, QS))
                for g0 in range(0, len(act), WIDTH):
                    grp = act[g0:g0 + WIDTH]
                    k = k_ref[kc, hd]
                    vt = vt_ref[hd, kc]
                    rs = [r for r, _ in grp]
                    mo, lo, ao = _attn_tiles_t(
                        [k] * len(grp), [qs[r] for r in rs], [vt] * len(grp),
                        [ms[r] for r in rs], [ls[r] for r in rs], [accs[r] for r in rs],
                        key_sub=ck2, qry_add=[cq2[r] for r in rs],
                        masks=[_tile_causal_t(r, c) if msk else None for r, msk in grp])
                    for i, r in enumerate(rs):
                        ms[r], ls[r], accs[r] = mo[i], lo[i], ao[i]
            for r, qc in enumerate(qcs):
                if diag:
                    o_ref[qc, hd] = (accs[r] / ls[r]).T.astype(BF16)
                else:
                    m_ref[h:h + 1, qc] = ms[r]
                    l_ref[h:h + 1, qc] = ls[r]
                    acc_ref[hd, qc] = accs[r]

    @pl.when(ki == qi)
    def _():
        run(True)

    @pl.when(ki != qi)
    def _():
        run(False)


def _fox_prompt(q, k, vt, c, t=512):
    T = q.shape[0]
    t = max(_tile(T, t), KS)
    qs, ks = _causal_pairs(T // t)
    grid_spec = pltpu.PrefetchScalarGridSpec(
        num_scalar_prefetch=2, grid=(len(qs),),
        in_specs=[pl.BlockSpec((t, FOX_W), lambda p, qi, ki: (qi[p], 0)),
                  pl.BlockSpec((t, FOX_W), lambda p, qi, ki: (ki[p], 0)),
                  pl.BlockSpec((FOX_W, t), lambda p, qi, ki: (0, ki[p])),
                  pl.BlockSpec((FOX_HEADS, t), lambda p, qi, ki: (0, qi[p])),
                  pl.BlockSpec((t, FOX_HEADS), lambda p, qi, ki: (ki[p], 0))],
        out_specs=pl.BlockSpec((t, FOX_W), lambda p, qi, ki: (qi[p], 0)),
        scratch_shapes=[pltpu.VMEM((FOX_HEADS, t), F32), pltpu.VMEM((FOX_HEADS, t), F32),
                        pltpu.VMEM((FOX_W, t), F32)])
    return pl.pallas_call(
        functools.partial(_fox_prompt_body, t=t), grid_spec=grid_spec,
        out_shape=jax.ShapeDtypeStruct((T, FOX_W), BF16),
        compiler_params=_cp(("arbitrary",)), name="fox_prompt",
    )(jnp.asarray(qs), jnp.asarray(ks), q, k, vt, c.T, c)


def _float_key(x):
    b = pltpu.bitcast(x, I32)
    return b ^ ((b >> 31) & jnp.int32(0x7FFFFFFF))


def _kth_key(keys_ref, nchunks, k, row0, rows, unroll=1):
    def count_ge(cand):
        def body(c, acc):
            for u in range(unroll):
                kk = keys_ref[c * unroll + u, row0:row0 + rows, :]
                acc = acc + jnp.where(kk >= cand, 1.0, 0.0)
            return acc
        acc = lax.fori_loop(0, nchunks // unroll, body, jnp.zeros((rows, LANES), F32))
        return jnp.sum(acc, axis=1, keepdims=True)

    def bit_body(i, t):
        cand = t + lax.shift_left(jnp.int32(1), 31 - i)
        return jnp.where(count_ge(cand) >= k, cand, t)

    return lax.fori_loop(0, 32, bit_body, jnp.full((rows, 1), INT_MIN, I32))


def _idx_prompt_body(q_ref, w_ref, ka_ref, kb_ref, keys_ref, thr_ref, *, tq, kc, n_sel, row_group):
    qi = pl.program_id(0)
    nck = keys_ref.shape[0]
    per = kc // LANES
    n_super = ((qi + 1) * tq + kc - 1) // kc

    def super_body(sc, carry):
        k0 = pl.multiple_of(sc * kc, kc)
        ka = ka_ref[pl.ds(k0, kc), :]
        kb = kb_ref[pl.ds(k0, kc), :]
        acc = jnp.zeros((tq, kc), F32)
        for j in range(IDX_QW // LANES):
            q2 = q_ref[:, j * LANES:(j + 1) * LANES]
            acc = acc + w_ref[:, 2 * j:2 * j + 1] * jnp.maximum(_nt_dot(q2, ka), 0.0)
            acc = acc + w_ref[:, 2 * j + 1:2 * j + 2] * jnp.maximum(_nt_dot(q2, kb), 0.0)
        qpos = qi * tq + lax.broadcasted_iota(I32, (tq, kc), 0)
        kpos = sc * kc + lax.broadcasted_iota(I32, (tq, kc), 1)
        key = jnp.where(kpos <= qpos, _float_key(acc), INT_MIN)
        for c in range(per):
            keys_ref[sc * per + c] = key[:, c * LANES:(c + 1) * LANES]
        return carry

    lax.fori_loop(0, n_super, super_body, 0)

    def fill_body(c, carry):
        keys_ref[c] = jnp.full((tq, LANES), INT_MIN, I32)
        return carry

    lax.fori_loop(n_super * per, nck, fill_body, 0)
    for rg in range(tq // row_group):
        thr_ref[rg * row_group:(rg + 1) * row_group, :] = _kth_key(
            keys_ref, n_super * per, n_sel, rg * row_group, row_group, unroll=per)


def _idx_prompt(qi_bf, wi, kia, kib, n_sel, tq=256, kc=512):
    T = qi_bf.shape[0]
    tq = _tile(T, tq)
    kc = _tile(T, kc)
    nck = T // LANES
    row_group = min(tq, 128)
    keys, thr = pl.pallas_call(
        functools.partial(_idx_prompt_body, tq=tq, kc=kc, n_sel=n_sel, row_group=row_group),
        grid=(T // tq,),
        in_specs=[pl.BlockSpec((tq, IDX_QW), lambda i: (i, 0)),
                  pl.BlockSpec((tq, IDX_HEADS), lambda i: (i, 0)),
                  pl.BlockSpec((T, LANES), lambda i: (0, 0)),
                  pl.BlockSpec((T, LANES), lambda i: (0, 0))],
        out_specs=[pl.BlockSpec((None, nck, tq, LANES), lambda i: (i, 0, 0, 0)),
                   pl.BlockSpec((tq, 1), lambda i: (i, 0))],
        out_shape=[jax.ShapeDtypeStruct((T // tq, nck, tq, LANES), I32),
                   jax.ShapeDtypeStruct((T, 1), I32)],
        compiler_params=_cp(("parallel",)), name="idx_prompt",
    )(qi_bf, wi, kia, kib)
    return keys, thr, tq


def _dsa_prompt_body(qi_ref, ki_ref, q_ref, k_ref, vt_ref, keys_ref, thr_ref, o_ref, m_ref, l_ref, acc_ref, *, t):
    p = pl.program_id(0)
    qi, ki = qi_ref[p], ki_ref[p]

    @pl.when(ki == 0)
    def _():
        _init_softmax_state(m_ref, l_ref, acc_ref)

    def run(diag):
        for r, c, msk in _tile_plan(t, diag):
            qc = slice(r * QS, (r + 1) * QS)
            kc = slice(c * KS, (c + 1) * KS)
            keys = jnp.concatenate([keys_ref[c * (KS // LANES) + j, qc, :] for j in range(KS // LANES)], axis=1)
            bias = jnp.where(keys >= thr_ref[qc, :], 0.0, NEG).T
            if msk:
                bias = jnp.where(_tile_causal_t(r, c), bias, NEG)
            for g in range(DSA_KV_HEADS):
                kv = slice(g * HEAD_DIM, (g + 1) * HEAD_DIM)
                hs = [g * DSA_GROUP + i for i in range(DSA_GROUP)]
                hds = [slice(h * HEAD_DIM, (h + 1) * HEAD_DIM) for h in hs]
                n = len(hs)
                mo, lo, ao = _attn_tiles_t(
                    [k_ref[kc, kv]] * n, [q_ref[qc, hd] for hd in hds], [vt_ref[kv, kc]] * n,
                    [m_ref[h:h + 1, qc] for h in hs], [l_ref[h:h + 1, qc] for h in hs],
                    [acc_ref[hd, qc] for hd in hds], biases=[bias] * n)
                for i, h in enumerate(hs):
                    m_ref[h:h + 1, qc] = mo[i]
                    l_ref[h:h + 1, qc] = lo[i]
                    acc_ref[hds[i], qc] = ao[i]

    @pl.when(ki == qi)
    def _():
        run(True)
        for h in range(DSA_HEADS):
            hd = slice(h * HEAD_DIM, (h + 1) * HEAD_DIM)
            o_ref[:, hd] = (acc_ref[hd, :] / l_ref[h:h + 1, :]).T.astype(BF16)

    @pl.when(ki != qi)
    def _():
        run(False)


def _dsa_prompt(q, k, vt, keys, thr, t):
    T = q.shape[0]
    qs, ks = _causal_pairs(T // t)
    per = t // LANES
    grid_spec = pltpu.PrefetchScalarGridSpec(
        num_scalar_prefetch=2, grid=(len(qs),),
        in_specs=[pl.BlockSpec((t, DSA_QW), lambda p, qi, ki: (qi[p], 0)),
                  pl.BlockSpec((t, DSA_KW), lambda p, qi, ki: (ki[p], 0)),
                  pl.BlockSpec((DSA_KW, t), lambda p, qi, ki: (0, ki[p])),
                  pl.BlockSpec((None, per, t, LANES), lambda p, qi, ki: (qi[p], ki[p], 0, 0)),
                  pl.BlockSpec((t, 1), lambda p, qi, ki: (qi[p], 0))],
        out_specs=pl.BlockSpec((t, DSA_QW), lambda p, qi, ki: (qi[p], 0)),
        scratch_shapes=[pltpu.VMEM((DSA_HEADS, t), F32), pltpu.VMEM((DSA_HEADS, t), F32),
                        pltpu.VMEM((DSA_QW, t), F32)])
    return pl.pallas_call(
        functools.partial(_dsa_prompt_body, t=t), grid_spec=grid_spec,
        out_shape=jax.ShapeDtypeStruct((T, DSA_QW), BF16),
        compiler_params=_cp(("arbitrary",)), name="dsa_prompt",
    )(jnp.asarray(qs), jnp.asarray(ks), q, k, vt, keys, thr)


TPAD = 8


def _head_rows(ref, h, n_heads):
    return ref[pl.ds(h, ref.shape[0] // n_heads, stride=n_heads), :].astype(BF16)


def _decode_step(qs, kget, vget, n_blocks, bias, state, rows_per_q):
    m_ref, l_ref, acc_ref = state
    n_q = len(qs)
    s = jnp.concatenate(
        [jnp.concatenate([_nt_dot(qs[j], kget(b, j)) for j in range(n_q)], axis=0) for b in range(n_blocks)],
        axis=1) * ATT_SCALE + bias
    m_prev = m_ref[...]
    m_new = jnp.maximum(m_prev, jnp.max(s, axis=-1, keepdims=True))
    alpha = jnp.exp(m_prev - m_new)
    p = jnp.exp(s - m_new)
    l_ref[...] = alpha * l_ref[...] + jnp.sum(p, axis=-1, keepdims=True)
    m_ref[...] = m_new
    pv = []
    for j in range(n_q):
        rows = slice(j * rows_per_q, (j + 1) * rows_per_q)
        d = None
        for b in range(n_blocks):
            t = jnp.dot(p[rows, b * LANES:(b + 1) * LANES].astype(BF16), vget(b, j), preferred_element_type=F32)
            d = t if d is None else d + t
        pv.append(d)
    acc_ref[...] = alpha * acc_ref[...] + jnp.concatenate(pv, axis=0)


def _fox_sample_body(pt_ref, q_ref, cq_ref, ck_ref, *refs, G):
    k_refs, v_refs = refs[:G], refs[G:2 * G]
    kn_ref, vn_ref, ckn_ref, o_ref, m_ref, l_ref, acc_ref = refs[2 * G:]
    p = pl.program_id(1)
    state = (m_ref, l_ref, acc_ref)

    @pl.when(p == 0)
    def _():
        _init_softmax_state(*state)

    qs = [q_ref[h].astype(BF16) for h in range(FOX_HEADS)]
    cq = cq_ref[...]

    def forget_bias(ck):
        return cq - jnp.concatenate(
            [jnp.broadcast_to(ck[h:h + 1, :], (TPAD, ck.shape[1])) for h in range(FOX_HEADS)], axis=0)

    _decode_step(qs, lambda b, h: _head_rows(k_refs[b], h, FOX_HEADS), lambda b, h: _head_rows(v_refs[b], h, FOX_HEADS),
                 G, forget_bias(ck_ref[...]), state, TPAD)

    @pl.when(p == pl.num_programs(1) - 1)
    def _():
        rows = FOX_HEADS * TPAD
        tok = lax.broadcasted_iota(I32, (rows, LANES), 0) % TPAD
        col = lax.broadcasted_iota(I32, (rows, LANES), 1)
        bias = jnp.where(col <= tok, forget_bias(ckn_ref[...]), NEG)
        _decode_step(qs, lambda b, h: kn_ref[h], lambda b, h: vn_ref[h], 1, bias, state, TPAD)
        o_ref[...] = acc_ref[...] / l_ref[...]


def _fox_sample(q, cq, ckT, cache_k, cache_v, page_table, knew, vnew, cknT, group=8):
    B, n_pages = page_table.shape
    R = cache_k.shape[1]
    page = R // FOX_HEADS
    rows = FOX_HEADS * TPAD
    G = _page_group(n_pages, group)
    per_b3 = lambda b, p, pt: (b, 0, 0)
    per_b4 = lambda b, p, pt: (b, 0, 0, 0)
    grid_spec = pltpu.PrefetchScalarGridSpec(
        num_scalar_prefetch=1, grid=(B, n_pages // G),
        in_specs=[pl.BlockSpec((None, FOX_HEADS, TPAD, HEAD_DIM), per_b4),
                  pl.BlockSpec((None, rows, 1), per_b3),
                  pl.BlockSpec((None, FOX_HEADS, G * page), lambda b, p, pt: (b, 0, p))]
        + _paged_specs(G, n_pages, (None, R, HEAD_DIM)) + _paged_specs(G, n_pages, (None, R, HEAD_DIM))
        + [pl.BlockSpec((None, FOX_HEADS, LANES, HEAD_DIM), per_b4),
           pl.BlockSpec((None, FOX_HEADS, LANES, HEAD_DIM), per_b4),
           pl.BlockSpec((None, FOX_HEADS, LANES), per_b3)],
        out_specs=pl.BlockSpec((None, rows, HEAD_DIM), per_b3),
        scratch_shapes=[pltpu.VMEM((rows, 1), F32), pltpu.VMEM((rows, 1), F32),
                        pltpu.VMEM((rows, HEAD_DIM), F32)])
    return pl.pallas_call(
        functools.partial(_fox_sample_body, G=G), grid_spec=grid_spec,
        out_shape=jax.ShapeDtypeStruct((B, rows, HEAD_DIM), F32),
        compiler_params=_cp(("parallel", "arbitrary")), name="fox_sample",
    )(page_table.reshape(-1).astype(I32), q, cq, ckT, *([cache_k] * G), *([cache_v] * G), knew, vnew, cknT)


def _idx_sample_body(pt_ref, q_ref, w_ref, *refs, ts, n_sel, G):
    k_refs = refs[:G]
    kn_ref, keys_ref, thr_ref = refs[G:]
    p = pl.program_id(1)
    n_pages = keys_ref.shape[0] - 1
    pad = jnp.full((TPAD - ts, LANES), INT_MIN, I32)

    def score(k):
        r = jnp.maximum(_nt_dot(q_ref[...], k), 0.0) * w_ref[...]
        return jnp.sum(r.reshape(ts, IDX_HEADS, LANES), axis=1)

    for g in range(G):
        keys_ref[p * G + g] = jnp.concatenate([_float_key(score(k_refs[g][...].astype(BF16))), pad], axis=0)

    @pl.when(p == pl.num_programs(1) - 1)
    def _():
        tok = lax.broadcasted_iota(I32, (ts, LANES), 0)
        col = lax.broadcasted_iota(I32, (ts, LANES), 1)
        kn = jnp.where(col <= tok, _float_key(score(kn_ref[...])), INT_MIN)
        keys_ref[n_pages] = jnp.concatenate([kn, pad], axis=0)
        thr_ref[...] = _kth_key(keys_ref, n_pages + 1, n_sel, 0, 8)


def _idx_sample(q, w, cache_k, page_table, knew, ts, n_sel, group=64):
    B, n_pages = page_table.shape
    R = cache_k.shape[1]
    rows = ts * IDX_HEADS
    G = _page_group(n_pages, group)
    per_b = lambda b, p, pt: (b, 0, 0)
    grid_spec = pltpu.PrefetchScalarGridSpec(
        num_scalar_prefetch=1, grid=(B, n_pages // G),
        in_specs=[pl.BlockSpec((None, rows, IDX_DIM), per_b),
                  pl.BlockSpec((None, rows, 1), per_b)]
        + _paged_specs(G, n_pages, (None, R, IDX_DIM))
        + [pl.BlockSpec((None, LANES, IDX_DIM), per_b)],
        out_specs=[pl.BlockSpec((None, n_pages + 1, 8, LANES), lambda b, p, pt: (b, 0, 0, 0)),
                   pl.BlockSpec((None, 8, 1), per_b)])
    return pl.pallas_call(
        functools.partial(_idx_sample_body, ts=ts, n_sel=n_sel, G=G), grid_spec=grid_spec,
        out_shape=[jax.ShapeDtypeStruct((B, n_pages + 1, 8, LANES), I32),
                   jax.ShapeDtypeStruct((B, 8, 1), I32)],
        compiler_params=_cp(("parallel", "arbitrary")), name="idx_sample",
    )(page_table.reshape(-1).astype(I32), q, w, *([cache_k] * G), knew)


def _dsa_sample_body(pt_ref, q_ref, keys_ref, keysn_ref, thr_ref, *refs, G):
    k_refs, v_refs = refs[:G], refs[G:2 * G]
    kn_ref, vn_ref, o_ref, m_ref, l_ref, acc_ref = refs[2 * G:]
    p = pl.program_id(1)
    state = (m_ref, l_ref, acc_ref)

    @pl.when(p == 0)
    def _():
        _init_softmax_state(*state)

    qs = [q_ref[g].astype(BF16) for g in range(DSA_KV_HEADS)]
    thr = thr_ref[...]

    def select_bias(keys):
        sel = jnp.concatenate([jnp.where(kk >= thr, 0.0, NEG) for kk in keys], axis=1)
        return jnp.concatenate([sel] * DSA_HEADS, axis=0)

    _decode_step(qs, lambda b, g: _head_rows(k_refs[b], g, DSA_KV_HEADS),
                 lambda b, g: _head_rows(v_refs[b], g, DSA_KV_HEADS),
                 G, select_bias([keys_ref[b] for b in range(G)]), state, DSA_GROUP * TPAD)

    @pl.when(p == pl.num_programs(1) - 1)
    def _():
        _decode_step(qs, lambda b, g: kn_ref[g], lambda b, g: vn_ref[g], 1, select_bias([keysn_ref[...]]),
                     state, DSA_GROUP * TPAD)
        o_ref[...] = acc_ref[...] / l_ref[...]


def _dsa_sample(q, keys, thr, cache_k, cache_v, page_table, knew, vnew, group=16):
    B, n_pages = page_table.shape
    R = cache_k.shape[1]
    rows = DSA_HEADS * TPAD
    G = _page_group(n_pages, group)
    per_b3 = lambda b, p, pt: (b, 0, 0)
    per_b4 = lambda b, p, pt: (b, 0, 0, 0)
    grid_spec = pltpu.PrefetchScalarGridSpec(
        num_scalar_prefetch=1, grid=(B, n_pages // G),
        in_specs=[pl.BlockSpec((None, DSA_KV_HEADS, DSA_GROUP * TPAD, HEAD_DIM), per_b4),
                  pl.BlockSpec((None, G, TPAD, LANES), lambda b, p, pt: (b, p, 0, 0)),
                  pl.BlockSpec((None, None, TPAD, LANES), lambda b, p, pt: (b, n_pages, 0, 0)),
                  pl.BlockSpec((None, TPAD, 1), per_b3)]
        + _paged_specs(G, n_pages, (None, R, HEAD_DIM)) + _paged_specs(G, n_pages, (None, R, HEAD_DIM))
        + [pl.BlockSpec((None, DSA_KV_HEADS, LANES, HEAD_DIM), per_b4),
           pl.BlockSpec((None, DSA_KV_HEADS, LANES, HEAD_DIM), per_b4)],
        out_specs=pl.BlockSpec((None, rows, HEAD_DIM), per_b3),
        scratch_shapes=[pltpu.VMEM((rows, 1), F32), pltpu.VMEM((rows, 1), F32),
                        pltpu.VMEM((rows, HEAD_DIM), F32)])
    return pl.pallas_call(
        functools.partial(_dsa_sample_body, G=G), grid_spec=grid_spec,
        out_shape=jax.ShapeDtypeStruct((B, rows, HEAD_DIM), F32),
        compiler_params=_cp(("parallel", "arbitrary")), name="dsa_sample",
    )(page_table.reshape(-1).astype(I32), q, keys, keys, thr, *([cache_k] * G), *([cache_v] * G), knew, vnew)


def _top_rows(x, n):
    out = []
    for _ in range(n):
        m = jnp.max(x, axis=0, keepdims=True)
        out.append(m)
        x = jnp.where(x == m, -jnp.inf, x)
    return out


def _peer_select_body(q_ref, keys_ref, s0_ref, s1_ref, st_ref, *, n_heads):
    nk = keys_ref.shape[2]
    thr, off = [], []
    for h in range(n_heads):
        tops = []
        for c, s_ref in enumerate((s0_ref, s1_ref)):
            col = (2 * h + c) * nk
            st = _nt_dot(q_ref[:, col:col + nk].astype(BF16), keys_ref[h, c]).T
            s_ref[h * nk:(h + 1) * nk, :] = st
            tops.append(jnp.concatenate(_top_rows(st, PEER_TOPK), axis=0))
        cand = jnp.concatenate([tops[0][r:r + 1, :] + tops[1] for r in range(PEER_TOPK)], axis=0)
        best = _top_rows(cand, PEER_TOPK)
        z = jnp.ones_like(best[0])
        for r in range(1, PEER_TOPK):
            z = z + jnp.exp(best[r] - best[0])
        thr.append(best[-1])
        off.append(-(best[0] + jnp.log(z)))
    st_ref[...] = jnp.concatenate(thr + off, axis=0)


def _peer_select(q, keys_bf, tm=128):
    M = q.shape[0]
    n_heads, _, nk, _ = keys_bf.shape
    tm = _tile(M, tm)
    col = lambda i: (0, i)
    return pl.pallas_call(
        functools.partial(_peer_select_body, n_heads=n_heads),
        grid=(M // tm,),
        in_specs=[pl.BlockSpec((tm, q.shape[1]), lambda i: (i, 0)),
                  pl.BlockSpec(keys_bf.shape, lambda i: (0, 0, 0, 0))],
        out_specs=[pl.BlockSpec((n_heads * nk, tm), col), pl.BlockSpec((n_heads * nk, tm), col),
                   pl.BlockSpec((2 * n_heads, tm), col)],
        out_shape=[jax.ShapeDtypeStruct((n_heads * nk, M), F32), jax.ShapeDtypeStruct((n_heads * nk, M), F32),
                   jax.ShapeDtypeStruct((2 * n_heads, M), F32)],
        compiler_params=_cp(("parallel",)), name="peer_select",
    )(q, keys_bf)


def _peer_expert_body(x_ref, u_ref, v_ref, s0_ref, s1_ref, st_ref, o_ref, *, n_heads, nk, ni):
    e = pl.program_id(1)

    @pl.when(e == 0)
    def _():
        o_ref[...] = jnp.zeros_like(o_ref)

    a = _nt_dot(u_ref[...], x_ref[...])
    act = 0.5 * a * (1.0 + lax.erf(a * (2.0 ** -0.5)))
    gates = []
    for il in range(ni):
        w = None
        for h in range(n_heads):
            tot = s0_ref[il * n_heads + h:il * n_heads + h + 1, :] + s1_ref[h * nk:(h + 1) * nk, :]
            val = jnp.where(tot >= st_ref[h:h + 1, :], jnp.exp(tot + st_ref[n_heads + h:n_heads + h + 1, :]), 0.0)
            w = val if w is None else w + val
        gates.append(w)
    ga = (jnp.concatenate(gates, axis=0) * act).astype(BF16)
    o_ref[...] += lax.dot_general(ga, v_ref[...], (((0,), (0,)), ((), ())), preferred_element_type=F32)


def _peer_expert(hn_bf, u_bf, v_bf, s0t, s1t, st, n_heads, nk, tm=512, te=512):
    M, D = hn_bf.shape
    E = u_bf.shape[0]
    tm = _tile(M, tm)
    te = _tile(E, te)
    ni = te // nk
    s0g = jnp.transpose(s0t.reshape(n_heads, E // te, ni, M), (1, 2, 0, 3)).reshape(E // te, ni * n_heads, M)
    return pl.pallas_call(
        functools.partial(_peer_expert_body, n_heads=n_heads, nk=nk, ni=ni),
        grid=(M // tm, E // te),
        in_specs=[pl.BlockSpec((tm, D), lambda i, e: (i, 0)),
                  pl.BlockSpec((te, D), lambda i, e: (e, 0)),
                  pl.BlockSpec((te, D), lambda i, e: (e, 0)),
                  pl.BlockSpec((None, ni * n_heads, tm), lambda i, e: (e, 0, i)),
                  pl.BlockSpec((n_heads * nk, tm), lambda i, e: (0, i)),
                  pl.BlockSpec((2 * n_heads, tm), lambda i, e: (0, i))],
        out_specs=pl.BlockSpec((tm, D), lambda i, e: (i, 0)),
        out_shape=jax.ShapeDtypeStruct((M, D), F32),
        compiler_params=_cp(("parallel", "arbitrary")), name="peer_expert",
    )(hn_bf, u_bf, v_bf, s0g, s1t, st)


def _prep_w_in(w_in):
    o = np.cumsum([0, FOX_W, FOX_W, FOX_W, FOX_HEADS, DSA_QW, DSA_KW, DSA_KW, IDX_QW, IDX_DIM, IDX_HEADS])
    seg = lambda i: w_in[:, o[i]:o[i + 1]]
    cols = [seg(i).astype(BF16) for i in (0, 1, 2, 4, 7, 5, 6, 8, 9, 3)]
    used = int(o[-1])
    total = -(-used // 768) * 768
    cols.append(jnp.zeros((w_in.shape[0], total - used), BF16))
    return jnp.concatenate(cols, axis=1)


def _channel(x, o_fox, o_dsa, p_l, w_o_bf, g_norm2, w_pq_bf, keys_bf, u_bf, v_bf, g_norm3, w_gate_bf, w_proj_bf,
             g_ple):
    n_heads, _, nk, _ = keys_bf.shape
    h1 = _mm2_res(o_fox, o_dsa, w_o_bf, x)
    q, hn_bf = _mm(h1, w_pq_bf, g=g_norm2, emit_xn=True)
    s0t, s1t, st = _peer_select(q, keys_bf)
    moe = _peer_expert(hn_bf, u_bf, v_bf, s0t, s1t, st, n_heads, nk)
    pn = _rownorm_mm(p_l, w_proj_bf, g_ple)
    return _mm(h1, w_gate_bf, g=g_norm3, x2=moe, res=h1, res2=moe, aux=pn, epi="gate", tm=256)


def _head_major(x, B, ts, n_heads, n_rows):
    x4 = jnp.transpose(x.reshape(B, ts, n_heads, HEAD_DIM), (0, 2, 1, 3))
    return jnp.pad(x4, ((0, 0), (0, 0), (0, n_rows - ts), (0, 0)))


def _token_major(o, B, ts, n_heads):
    o4 = o.reshape(B, n_heads, TPAD, HEAD_DIM)[:, :, :ts]
    return jnp.transpose(o4, (0, 2, 1, 3)).reshape(B * ts, n_heads * HEAD_DIM)


def _pad_rows(x, n):
    return jnp.pad(x, ((0, 0), (0, n - x.shape[1]), (0, 0)))


def kernel(x_prompt, x_sample, cache_fox_k, cache_fox_v, cache_fox_logf, cache_dsa_k, cache_dsa_v, cache_idx_k, page_table, p_prompt, p_sample, g_norm1, w_in, b_f, g_q_fox, g_k_fox, g_q_dsa, g_k_dsa, w_o, g_norm2, w_peer_q, peer_keys, peer_u, peer_v, g_norm3, w_ple_gate, w_ple_proj, g_ple):
    Bp, Tp, D = x_prompt.shape
    Bs, Ts = x_sample.shape[:2]
    depth = w_in.shape[0]
    n_pages = page_table.shape[1]
    page = cache_fox_k.shape[2]
    past = n_pages * page
    n_phys = cache_fox_k.shape[1]
    assert Bp == 1 and Ts <= TPAD and page == LANES

    def pool(cache):
        return cache.reshape((depth * n_phys, -1) + cache.shape[-1:])

    fox_k_pool, fox_v_pool, dsa_k_pool, dsa_v_pool = pool(cache_fox_k), pool(cache_fox_v), pool(cache_dsa_k), pool(cache_dsa_v)
    fox_lf_pool = cache_fox_logf.reshape(depth * n_phys, page, FOX_HEADS)
    idx_k_pool = cache_idx_k.reshape(depth * n_phys, page, IDX_DIM)

    h_p = x_prompt.reshape(Tp, D)
    h_s = x_sample.reshape(Bs * Ts, D)
    pos_p = jnp.arange(Tp, dtype=I32)
    pos_s = jnp.tile(past + jnp.arange(Ts, dtype=I32), Bs)
    outs = [[] for _ in range(12)]
    for l in range(depth):
        w_in_bf = _prep_w_in(w_in[l])
        chan_w = (w_o[l].astype(BF16), g_norm2[l], w_peer_q[l].astype(BF16), peer_keys[l].astype(BF16),
                  peer_u[l].astype(BF16), peer_v[l].astype(BF16), g_norm3[l], w_ple_gate[l].astype(BF16),
                  w_ple_proj[l].astype(BF16), g_ple[l])
        norm_w = (b_f[l], g_q_fox[l], g_k_fox[l], g_q_dsa[l], g_k_dsa[l])
        pt_l = page_table + l * n_phys

        z = _mm(h_p, w_in_bf, g=g_norm1[l], tn=768)
        pr = _post(z, pos_p, *norm_w, ATT_SCALE * LOG2E)
        ident = jnp.arange(Tp // page, dtype=I32).reshape(1, -1)
        c, _ = _paged_cumsum(pr["lf"].reshape(Tp // page, page, FOX_HEADS), ident,
                             jnp.zeros((1, 8, FOX_HEADS), F32))
        o_fox = _fox_prompt(pr["qf"], pr["kfb"], pr["vft"], c[0])
        keys, thr, tq = _idx_prompt(pr["qi"], pr["wi"], pr["kia"], pr["kib"], min(TOPK_MAX, Tp // 4))
        o_dsa = _dsa_prompt(pr["qd"], pr["kdb"], pr["vdt"], keys, thr, tq)
        h_p = _channel(h_p, o_fox, o_dsa, p_prompt[l].reshape(Tp, -1), *chan_w)
        for i, (name, shape) in enumerate((("kf", (Bp, Tp, FOX_HEADS, HEAD_DIM)), ("vf", (Bp, Tp, FOX_HEADS, HEAD_DIM)),
                                           ("lf", (Bp, Tp, FOX_HEADS)), ("kd", (Bp, Tp, DSA_KV_HEADS, HEAD_DIM)),
                                           ("vd", (Bp, Tp, DSA_KV_HEADS, HEAD_DIM)), ("ki", (Bp, Tp, IDX_DIM)))):
            outs[i].append(pr[name].reshape(shape))

        zs = _mm(h_s, w_in_bf, g=g_norm1[l], tn=768)
        sr = _post(zs, pos_s, *norm_w, 1.0)
        lf_new = _pad_rows(sr["lf"].reshape(Bs, Ts, FOX_HEADS), TPAD)
        c_past, c_new = _paged_cumsum(fox_lf_pool, pt_l, lf_new)
        cq = jnp.transpose(c_new, (0, 2, 1)).reshape(Bs, FOX_HEADS * TPAD, 1)
        ckT = jnp.transpose(c_past, (0, 2, 1))
        cknT = jnp.transpose(_pad_rows(c_new, LANES), (0, 2, 1))
        o = _fox_sample(_head_major(sr["qf"], Bs, Ts, FOX_HEADS, TPAD).astype(F32), cq, ckT, fox_k_pool, fox_v_pool, pt_l,
                        _head_major(sr["kfb"], Bs, Ts, FOX_HEADS, LANES), _head_major(sr["vfb"], Bs, Ts, FOX_HEADS, LANES),
                        cknT)
        o_fox_s = _token_major(o, Bs, Ts, FOX_HEADS)
        n_sel = min(TOPK_MAX, (past + Ts) // 4)
        keys_s, thr_s = _idx_sample(sr["qi"].reshape(Bs, Ts * IDX_HEADS, IDX_DIM),
                                    sr["wi"].reshape(Bs, Ts * IDX_HEADS, 1), idx_k_pool, pt_l,
                                    _pad_rows(sr["kia"][:, :IDX_DIM].reshape(Bs, Ts, IDX_DIM), LANES), Ts, n_sel)
        qd = _head_major(sr["qd"], Bs, Ts, DSA_HEADS, TPAD).astype(F32)
        od = _dsa_sample(qd.reshape(Bs, DSA_KV_HEADS, DSA_GROUP * TPAD, HEAD_DIM), keys_s, thr_s,
                         dsa_k_pool, dsa_v_pool, pt_l, _head_major(sr["kdb"], Bs, Ts, DSA_KV_HEADS, LANES),
                         _head_major(sr["vdb"], Bs, Ts, DSA_KV_HEADS, LANES))
        o_dsa_s = _token_major(od, Bs, Ts, DSA_HEADS)
        h_s = _channel(h_s, o_fox_s.astype(BF16), o_dsa_s.astype(BF16), p_sample[l].reshape(Bs * Ts, -1), *chan_w)
        for i, (name, shape) in enumerate((("kf", (Bs, Ts, FOX_HEADS, HEAD_DIM)), ("vf", (Bs, Ts, FOX_HEADS, HEAD_DIM)),
                                           ("lf", (Bs, Ts, FOX_HEADS)), ("kd", (Bs, Ts, DSA_KV_HEADS, HEAD_DIM)),
                                           ("vd", (Bs, Ts, DSA_KV_HEADS, HEAD_DIM)), ("ki", (Bs, Ts, IDX_DIM)))):
            outs[6 + i].append(sr[name].reshape(shape))
    return (h_p.reshape(Bp, Tp, D), h_s.reshape(Bs, Ts, D)) + tuple(jnp.stack(o) for o in outs)
```

```python
import functools
import math

import numpy as np
import jax
import jax.numpy as jnp
from jax import lax
from jax.experimental import pallas as pl
from jax.experimental.pallas import tpu as pltpu

F32 = jnp.float32
BF16 = jnp.bfloat16
I32 = jnp.int32

HEAD_DIM = 128
FOX_HEADS = 16
DSA_HEADS = 16
DSA_KV_HEADS = 4
DSA_GROUP = DSA_HEADS // DSA_KV_HEADS
IDX_HEADS = 32
IDX_DIM = 64
TOPK_MAX = 256
ROPE_THETA = 10000.0
PEER_TOPK = 16
EPS = 1e-6

FOX_W = FOX_HEADS * HEAD_DIM
DSA_QW = DSA_HEADS * HEAD_DIM
DSA_KW = DSA_KV_HEADS * HEAD_DIM
IDX_QW = IDX_HEADS * IDX_DIM

LANES = 128
NEG = -1e30
INT_MIN = -(2 ** 31)
VMEM_LIMIT = 56 * 1024 * 1024
ATT_SCALE = HEAD_DIM ** -0.5
LOG2E = math.log2(math.e)
QS = 128
KS = 256
WIDTH = 4
GROUPS = 2
FOX_TOGETHER = 2


def _cp(sem):
    return pltpu.CompilerParams(dimension_semantics=sem, vmem_limit_bytes=VMEM_LIMIT)


def _tile(n, pref):
    t = min(n, pref)
    while n % t:
        t //= 2
    return t


def _nt_dot(a, b):
    return lax.dot_general(a, b, (((1,), (1,)), ((), ())), preferred_element_type=F32)


def _mm_body(*refs, norm, add2, epi, emit_xn, prologue):
    it = iter(refs)
    x_ref = next(it)
    x2_ref = next(it) if add2 else None
    g_ref = next(it) if norm else None
    w_ref = next(it)
    res_ref = next(it) if epi in ("res", "gate") else None
    res2_ref = next(it) if (epi == "gate" and add2) else None
    aux_ref = next(it) if epi == "gate" else None
    o_ref = next(it)
    xo_ref = next(it) if emit_xn else None
    xn_ref = next(it) if prologue else None

    if prologue:
        @pl.when(pl.program_id(1) == 0)
        def _():
            x = x_ref[...].astype(F32)
            if add2:
                x = x + x2_ref[...]
            if norm:
                ms = jnp.mean(x * x, axis=-1, keepdims=True)
                x = x * lax.rsqrt(ms + EPS) * g_ref[...]
            xb = x.astype(BF16)
            xn_ref[...] = xb
            if emit_xn:
                xo_ref[...] = xb
        lhs = xn_ref[...]
    else:
        lhs = x_ref[...]
    acc = jnp.dot(lhs, w_ref[...], preferred_element_type=F32)
    if epi == "res":
        acc = res_ref[...] + acc
    elif epi == "gate":
        r = res_ref[...]
        if add2:
            r = r + res2_ref[...]
        acc = r + jax.nn.sigmoid(acc) * aux_ref[...]
    o_ref[...] = acc


def _mm(x, w, *, g=None, x2=None, res=None, res2=None, aux=None, epi="none", emit_xn=False,
        tm=512, tn=512):
    M, K = x.shape
    N = w.shape[1]
    tm = _tile(M, tm)
    tn = _tile(N, tn)
    norm = g is not None
    add2 = x2 is not None
    prologue = norm or add2 or x.dtype != BF16
    row = pl.BlockSpec((tm, K), lambda i, j: (i, 0))
    blk = pl.BlockSpec((tm, tn), lambda i, j: (i, j))
    ins, specs = [x], [row]
    if add2:
        ins.append(x2); specs.append(row)
    if norm:
        ins.append(g.reshape(1, K)); specs.append(pl.BlockSpec((1, K), lambda i, j: (0, 0)))
    ins.append(w); specs.append(pl.BlockSpec((K, tn), lambda i, j: (0, j)))
    if epi in ("res", "gate"):
        ins.append(res); specs.append(blk)
    if epi == "gate" and add2:
        ins.append(res2); specs.append(blk)
    if epi == "gate":
        ins.append(aux); specs.append(blk)
    out_shape = [jax.ShapeDtypeStruct((M, N), F32)]
    out_specs = [blk]
    if emit_xn:
        out_shape.append(jax.ShapeDtypeStruct((M, K), BF16))
        out_specs.append(row)
    scratch = [pltpu.VMEM((tm, K), BF16)] if prologue else []
    outs = pl.pallas_call(
        functools.partial(_mm_body, norm=norm, add2=add2, epi=epi, emit_xn=emit_xn, prologue=prologue),
        grid=(M // tm, N // tn),
        in_specs=specs, out_specs=out_specs, out_shape=out_shape, scratch_shapes=scratch,
        compiler_params=_cp(("parallel", "arbitrary")), name="mm_" + epi,
    )(*ins)
    return outs if emit_xn else outs[0]


def _mm2_res_body(x1_ref, x2_ref, w1_ref, w2_ref, res_ref, o_ref):
    o_ref[...] = res_ref[...] + (jnp.dot(x1_ref[...], w1_ref[...], preferred_element_type=F32)
                                 + jnp.dot(x2_ref[...], w2_ref[...], preferred_element_type=F32))


def _mm2_res(x1, x2, w, res, tm=512, tn=512):
    M, K1 = x1.shape
    N = w.shape[1]
    assert x2.shape == (M, K1) and w.shape[0] == 2 * K1
    tm = _tile(M, tm)
    tn = _tile(N, tn)
    row = pl.BlockSpec((tm, K1), lambda i, j: (i, 0))
    blk = pl.BlockSpec((tm, tn), lambda i, j: (i, j))
    return pl.pallas_call(
        _mm2_res_body,
        grid=(M // tm, N // tn),
        in_specs=[row, row, pl.BlockSpec((K1, tn), lambda i, j: (0, j)), pl.BlockSpec((K1, tn), lambda i, j: (1, j)), blk],
        out_specs=blk, out_shape=jax.ShapeDtypeStruct((M, N), F32),
        compiler_params=_cp(("parallel", "arbitrary")), name="mm2_res",
    )(x1, x2, w, w, res)


def _rownorm_mm_body(x_ref, w_ref, g_ref, o_ref):
    y = jnp.dot(x_ref[...].astype(BF16), w_ref[...], preferred_element_type=F32)
    ms = jnp.mean(y * y, axis=-1, keepdims=True)
    o_ref[...] = y * lax.rsqrt(ms + EPS) * g_ref[...]


def _rownorm_mm(x, w, g, tm=256):
    M, K = x.shape
    N = w.shape[1]
    tm = _tile(M, tm)
    return pl.pallas_call(
        _rownorm_mm_body,
        grid=(M // tm,),
        in_specs=[pl.BlockSpec((tm, K), lambda i: (i, 0)),
                  pl.BlockSpec((K, N), lambda i: (0, 0)),
                  pl.BlockSpec((1, N), lambda i: (0, 0))],
        out_specs=pl.BlockSpec((tm, N), lambda i: (i, 0)),
        out_shape=jax.ShapeDtypeStruct((M, N), F32),
        compiler_params=_cp(("parallel",)), name="rownorm_mm",
    )(x, w, g.reshape(1, N))


def _head_norm(x, g):
    ms = jnp.mean(x * x, axis=-1, keepdims=True)
    return x * lax.rsqrt(ms + EPS) * g


def _rope_full(x, cos, sin):
    return x * cos + pltpu.roll(x, HEAD_DIM // 2, 1) * sin


def _rope_pair(x, cos, sin, lane):
    half = IDX_DIM // 2
    rot = jnp.where((lane % IDX_DIM) < half, pltpu.roll(x, LANES - half, 1), pltpu.roll(x, half, 1))
    return x * cos + rot * sin


def _post_body(zqf, zkf, zvf, zqd, zqi, zkv, zs, cd, sd, ci, si, bf, gqf, gkf, gqd, gkd,
               qf_o, kf_o, kfb_o, vf_o, vfb_o, lf_o, qd_o, kd_o, kdb_o, vd_o, vdb_o,
               qi_o, ki_o, kia_o, kib_o, wi_o, vft_o, vdt_o, *, q_scale):
    tm = zqf.shape[0]
    cos_d, sin_d = cd[...], sd[...]
    cos_i, sin_i = ci[...], si[...]
    lane = lax.broadcasted_iota(I32, (tm, LANES), 1)
    for h in range(FOX_HEADS):
        sl = slice(h * HEAD_DIM, (h + 1) * HEAD_DIM)
        qf_o[:, sl] = (_head_norm(zqf[:, sl], gqf[...]) * q_scale).astype(BF16)
        k = _head_norm(zkf[:, sl], gkf[...])
        kf_o[:, sl] = k
        kfb_o[:, sl] = k.astype(BF16)
        v = zvf[:, sl]
        vf_o[:, sl] = v
        vfb_o[:, sl] = v.astype(BF16)
        vft_o[sl, :] = v.T.astype(BF16)
    for h in range(DSA_HEADS):
        sl = slice(h * HEAD_DIM, (h + 1) * HEAD_DIM)
        qd_o[:, sl] = (_rope_full(_head_norm(zqd[:, sl], gqd[...]), cos_d, sin_d) * q_scale).astype(BF16)
    for h in range(DSA_KV_HEADS):
        sl = slice(h * HEAD_DIM, (h + 1) * HEAD_DIM)
        k = _rope_full(_head_norm(zkv[:, sl], gkd[...]), cos_d, sin_d)
        kd_o[:, sl] = k
        kdb_o[:, sl] = k.astype(BF16)
        v = zkv[:, DSA_KW + h * HEAD_DIM: DSA_KW + (h + 1) * HEAD_DIM]
        vd_o[:, sl] = v
        vdb_o[:, sl] = v.astype(BF16)
        vdt_o[sl, :] = v.T.astype(BF16)
    for j in range(IDX_QW // LANES):
        sl = slice(j * LANES, (j + 1) * LANES)
        qi_o[:, sl] = _rope_pair(zqi[:, sl], cos_i, sin_i, lane).astype(BF16)
    x = zs[...]
    y = _rope_pair(x, cos_i, sin_i, lane)
    ki_o[...] = y[:, :IDX_DIM]
    ka = jnp.where(lane < IDX_DIM, y, 0.0)
    kia_o[...] = ka.astype(BF16)
    kib_o[...] = pltpu.roll(ka, IDX_DIM, 1).astype(BF16)
    wi_o[...] = x[:, IDX_DIM:IDX_DIM + IDX_HEADS] * (IDX_HEADS ** -0.5 * IDX_DIM ** -0.5)
    fl = x[:, IDX_DIM + IDX_HEADS:IDX_DIM + IDX_HEADS + FOX_HEADS] + bf[...]
    lf_o[...] = jnp.minimum(fl, 0.0) - jnp.log1p(jnp.exp(-jnp.abs(fl)))


def _post(z, pos, b_f, g_q_fox, g_k_fox, g_q_dsa, g_k_dsa, q_scale, tm=128):
    M = z.shape[0]
    tm = _tile(M, tm)
    posf = pos.astype(F32)[:, None]
    hd = HEAD_DIM // 2
    ang = posf * (ROPE_THETA ** (-jnp.arange(hd, dtype=F32) * 2.0 / HEAD_DIM))
    cd = jnp.concatenate([jnp.cos(ang), jnp.cos(ang)], axis=-1)
    sd = jnp.concatenate([-jnp.sin(ang), jnp.sin(ang)], axis=-1)
    hi = IDX_DIM // 2
    angi = posf * (ROPE_THETA ** (-jnp.arange(hi, dtype=F32) * 2.0 / IDX_DIM))
    ci = jnp.tile(jnp.cos(angi), (1, 4))
    si = jnp.tile(jnp.concatenate([-jnp.sin(angi), jnp.sin(angi)], axis=-1), (1, 2))

    def zspec(w, c):
        return pl.BlockSpec((tm, w), lambda i: (i, c))

    def rspec(w):
        return pl.BlockSpec((tm, w), lambda i: (i, 0))

    def cspec(w):
        return pl.BlockSpec((1, w), lambda i: (0, 0))

    small_col = (5 * FOX_W + 2 * DSA_KW) // LANES
    in_specs = [zspec(FOX_W, 0), zspec(FOX_W, 1), zspec(FOX_W, 2), zspec(DSA_QW, 3), zspec(IDX_QW, 4),
                zspec(2 * DSA_KW, 5 * FOX_W // (2 * DSA_KW)), zspec(LANES, small_col),
                rspec(LANES), rspec(LANES), rspec(LANES), rspec(LANES),
                cspec(FOX_HEADS), cspec(HEAD_DIM), cspec(HEAD_DIM), cspec(HEAD_DIM), cspec(HEAD_DIM)]
    outs = [(FOX_W, BF16), (FOX_W, F32), (FOX_W, BF16), (FOX_W, F32), (FOX_W, BF16), (FOX_HEADS, F32),
            (DSA_QW, BF16), (DSA_KW, F32), (DSA_KW, BF16), (DSA_KW, F32), (DSA_KW, BF16),
            (IDX_QW, BF16), (IDX_DIM, F32), (LANES, BF16), (LANES, BF16), (IDX_HEADS, F32)]
    tspec = lambda w: pl.BlockSpec((w, tm), lambda i: (0, i))
    res = pl.pallas_call(
        functools.partial(_post_body, q_scale=q_scale),
        grid=(M // tm,),
        in_specs=in_specs,
        out_specs=[rspec(w) for w, _ in outs] + [tspec(FOX_W), tspec(DSA_KW)],
        out_shape=[jax.ShapeDtypeStruct((M, w), d) for w, d in outs]
        + [jax.ShapeDtypeStruct((FOX_W, M), BF16), jax.ShapeDtypeStruct((DSA_KW, M), BF16)],
        compiler_params=_cp(("parallel",)), name="post_proj",
    )(z, z, z, z, z, z, z, cd, sd, ci, si, b_f.reshape(1, -1), g_q_fox.reshape(1, -1),
      g_k_fox.reshape(1, -1), g_q_dsa.reshape(1, -1), g_k_dsa.reshape(1, -1))
    names = ("qf", "kf", "kfb", "vf", "vfb", "lf", "qd", "kd", "kdb", "vd", "vdb", "qi", "ki", "kia",
             "kib", "wi", "vft", "vdt")
    return dict(zip(names, res))


def _split3(x):
    hi = x.astype(BF16)
    r1 = x - hi.astype(F32)
    mid = r1.astype(BF16)
    lo = (r1 - mid.astype(F32)).astype(BF16)
    return hi, mid, lo


def _tri_ones(n):
    r = lax.broadcasted_iota(I32, (n, n), 0)
    c = lax.broadcasted_iota(I32, (n, n), 1)
    return jnp.where(c <= r, 1.0, 0.0).astype(BF16)


def _tri_cumsum(tri, x):
    hi, mid, lo = _split3(x)
    dot = lambda a: jnp.dot(tri, a, preferred_element_type=F32)
    return dot(hi) + (dot(mid) + dot(lo))


def _page_group(n_pages, pref):
    g = min(n_pages, pref)
    while n_pages % g:
        g -= 1
    return g


def _paged_specs(G, n_pages, block):
    def spec(g):
        return pl.BlockSpec(block, lambda b, p, pt: (pt[b * n_pages + p * G + g], 0, 0))
    return [spec(g) for g in range(G)]


def _cumsum_body(pt_ref, *refs, G):
    lf_refs = refs[:G]
    new_ref, c_ref, cn_ref, carry_ref = refs[G:]
    p = pl.program_id(1)
    R = lf_refs[0].shape[0]

    @pl.when(p == 0)
    def _():
        carry_ref[...] = jnp.zeros_like(carry_ref)

    tri = _tri_ones(R)
    carry = carry_ref[...]
    for g in range(G):
        c = _tri_cumsum(tri, lf_refs[g][...]) + carry
        c_ref[g * R:(g + 1) * R, :] = c
        carry = c[-1:, :]
    carry_ref[...] = carry

    @pl.when(p == pl.num_programs(1) - 1)
    def _():
        nr = new_ref.shape[0]
        cn_ref[...] = _tri_cumsum(_tri_ones(nr), new_ref[...]) + carry


def _paged_cumsum(pool, page_table, new_rows, group=16):
    B, n_pages = page_table.shape
    R, H = pool.shape[1:]
    nr = new_rows.shape[1]
    G = _page_group(n_pages, group)
    grid_spec = pltpu.PrefetchScalarGridSpec(
        num_scalar_prefetch=1, grid=(B, n_pages // G),
        in_specs=_paged_specs(G, n_pages, (None, R, H)) + [
            pl.BlockSpec((None, nr, H), lambda b, p, pt: (b, 0, 0))],
        out_specs=[pl.BlockSpec((None, G * R, H), lambda b, p, pt: (b, p, 0)),
                   pl.BlockSpec((None, nr, H), lambda b, p, pt: (b, 0, 0))],
        scratch_shapes=[pltpu.VMEM((1, H), F32)])
    c, cn = pl.pallas_call(
        functools.partial(_cumsum_body, G=G), grid_spec=grid_spec,
        out_shape=[jax.ShapeDtypeStruct((B, n_pages * R, H), F32),
                   jax.ShapeDtypeStruct((B, nr, H), F32)],
        compiler_params=_cp(("parallel", "arbitrary")), name="paged_cumsum",
    )(page_table.reshape(-1).astype(I32), *([pool] * G), new_rows)
    return c, cn


def _softmax_step(s, vs, m_prev, l_prev, acc_prev):
    m_new = jnp.maximum(m_prev, jnp.max(s, axis=-1, keepdims=True))
    alpha = jnp.exp(m_prev - m_new)
    p = jnp.exp(s - m_new)
    l_new = alpha * l_prev + jnp.sum(p, axis=-1, keepdims=True)
    pb = p.astype(BF16)
    pv = None
    for g, v in enumerate(vs):
        d = jnp.dot(pb[:, g * LANES:(g + 1) * LANES], v, preferred_element_type=F32)
        pv = d if pv is None else pv + d
    return m_new, l_new, alpha * acc_prev + pv


def _causal_pairs(nq):
    qs, ks = [], []
    for qi in range(nq):
        for ki in range(qi + 1):
            qs.append(qi); ks.append(ki)
    return np.asarray(qs, np.int32), np.asarray(ks, np.int32)


def _init_softmax_state(m_ref, l_ref, acc_ref):
    m_ref[...] = jnp.full_like(m_ref, NEG)
    l_ref[...] = jnp.zeros_like(l_ref)
    acc_ref[...] = jnp.zeros_like(acc_ref)


def _tile_plan(t, diag):
    plan = []
    for r in range(t // QS):
        for c in range(t // KS):
            if diag and c * KS > r * QS + QS - 1:
                continue
            plan.append((r, c, diag and (c + 1) * KS - 1 > r * QS))
    return plan


def _attn_tiles_t(ks, qs, vts, ms, ls, accs, *, biases=None, masks=None, key_sub=None, qry_add=None):
    n = len(ks)
    ts = [_nt_dot(ks[i], qs[i]) for i in range(n)]
    if key_sub is not None:
        ts = [ts[i] - key_sub[i] for i in range(n)]
    if biases is not None:
        ts = [ts[i] + biases[i] for i in range(n)]
    if masks is not None:
        ts = [ts[i] if masks[i] is None else jnp.where(masks[i], ts[i], NEG) for i in range(n)]
    mx = [jnp.max(ts[i], axis=0, keepdims=True) for i in range(n)]
    if qry_add is not None:
        mx = [mx[i] + qry_add[i] for i in range(n)]
    m_new = [jnp.maximum(ms[i], mx[i]) for i in range(n)]
    shift = [-m_new[i] if qry_add is None else qry_add[i] - m_new[i] for i in range(n)]
    ps = [jnp.exp2(ts[i] + shift[i]) for i in range(n)]
    alphas = [jnp.exp2(ms[i] - m_new[i]) for i in range(n)]
    l_new = [alphas[i] * ls[i] + jnp.sum(ps[i], axis=0, keepdims=True) for i in range(n)]
    pv = [jnp.dot(vts[i], ps[i].astype(BF16), preferred_element_type=F32) for i in range(n)]
    acc_new = [alphas[i] * accs[i] + pv[i] for i in range(n)]
    return m_new, l_new, acc_new


def _tile_causal_t(r, c):
    key = lax.broadcasted_iota(I32, (KS, QS), 0)
    qry = lax.broadcasted_iota(I32, (KS, QS), 1)
    return key + (c * KS - r * QS) <= qry


def _fox_prompt_body(qi_ref, ki_ref, q_ref, k_ref, vt_ref, cq_ref, ck_ref, o_ref, m_ref, l_ref, acc_ref, *, t):
    p = pl.program_id(0)
    qi, ki = qi_ref[p], ki_ref[p]

    @pl.when(ki == 0)
    def _():
        _init_softmax_state(m_ref, l_ref, acc_ref)

    def run(diag):
        plan = _tile_plan(t, diag)
        qcs = [slice(r * QS, (r + 1) * QS) for r in range(t // QS)]
        for h0 in range(0, FOX_HEADS, FOX_TOGETHER):
            heads = list(range(h0, h0 + FOX_TOGETHER))
            hds = {h: slice(h * HEAD_DIM, (h + 1) * HEAD_DIM) for h in heads}
            qs = {(h, r): q_ref[qc, hds[h]] for h in heads for r, qc in enumerate(qcs)}
            cq2 = {(h, r): cq_ref[h:h + 1, qc] * LOG2E for h in heads for r, qc in enumerate(qcs)}
            ms = {(h, r): m_ref[h:h + 1, qc] for h in heads for r, qc in enumerate(qcs)}
            ls = {(h, r): l_ref[h:h + 1, qc] for h in heads for r, qc in enumerate(qcs)}
            accs = {(h, r): acc_ref[hds[h], qc] for h in heads for r, qc in enumerate(qcs)}
            for c in range(t // KS):
                kc = slice(c * KS, (c + 1) * KS)
                act = [(r, msk) for r, cc, msk in plan if cc == c]
                ck2 = {h: jnp.broadcast_to(ck_ref[kc, h:h + 1] * LOG2E, (KS, QS)) for h in heads}
                k = {h: k_ref[kc, hds[h]] for h in heads}
                vt = {h: vt_ref[hds[h], kc] for h in heads}
                for g0 in range(0, len(act), WIDTH):
                    tiles = [(h, r, msk) for h in heads for r, msk in act[g0:g0 + WIDTH]]
                    mo, lo, ao = _attn_tiles_t(
                        [k[h] for h, _, _ in tiles], [qs[h, r] for h, r, _ in tiles], [vt[h] for h, _, _ in tiles],
                        [ms[h, r] for h, r, _ in tiles], [ls[h, r] for h, r, _ in tiles],
                        [accs[h, r] for h, r, _ in tiles],
                        key_sub=[ck2[h] for h, _, _ in tiles], qry_add=[cq2[h, r] for h, r, _ in tiles],
                        masks=[_tile_causal_t(r, c) if msk else None for _, r, msk in tiles])
                    for i, (h, r, _) in enumerate(tiles):
                        ms[h, r], ls[h, r], accs[h, r] = mo[i], lo[i], ao[i]
            for h in heads:
                for r, qc in enumerate(qcs):
                    if diag:
                        o_ref[qc, hds[h]] = (accs[h, r] / ls[h, r]).T.astype(BF16)
                    else:
                        m_ref[h:h + 1, qc] = ms[h, r]
                        l_ref[h:h + 1, qc] = ls[h, r]
                        acc_ref[hds[h], qc] = accs[h, r]

    @pl.when(ki == qi)
    def _():
        run(True)

    @pl.when(ki != qi)
    def _():
        run(False)


def _fox_prompt(q, k, vt, c, t=512):
    T = q.shape[0]
    t = max(_tile(T, t), KS)
    qs, ks = _causal_pairs(T // t)
    grid_spec = pltpu.PrefetchScalarGridSpec(
        num_scalar_prefetch=2, grid=(len(qs),),
        in_specs=[pl.BlockSpec((t, FOX_W), lambda p, qi, ki: (qi[p], 0)),
                  pl.BlockSpec((t, FOX_W), lambda p, qi, ki: (ki[p], 0)),
                  pl.BlockSpec((FOX_W, t), lambda p, qi, ki: (0, ki[p])),
                  pl.BlockSpec((FOX_HEADS, t), lambda p, qi, ki: (0, qi[p])),
                  pl.BlockSpec((t, FOX_HEADS), lambda p, qi, ki: (ki[p], 0))],
        out_specs=pl.BlockSpec((t, FOX_W), lambda p, qi, ki: (qi[p], 0)),
        scratch_shapes=[pltpu.VMEM((FOX_HEADS, t), F32), pltpu.VMEM((FOX_HEADS, t), F32),
                        pltpu.VMEM((FOX_W, t), F32)])
    return pl.pallas_call(
        functools.partial(_fox_prompt_body, t=t), grid_spec=grid_spec,
        out_shape=jax.ShapeDtypeStruct((T, FOX_W), BF16),
        compiler_params=_cp(("arbitrary",)), name="fox_prompt",
    )(jnp.asarray(qs), jnp.asarray(ks), q, k, vt, c.T, c)


def _float_key(x):
    b = pltpu.bitcast(x, I32)
    return b ^ ((b >> 31) & jnp.int32(0x7FFFFFFF))


def _kth_key(keys_ref, nchunks, k, row0, rows, unroll=1):
    def count_ge(cand):
        def body(c, acc):
            for u in range(unroll):
                kk = keys_ref[c * unroll + u, row0:row0 + rows, :]
                acc = acc + jnp.where(kk >= cand, 1.0, 0.0)
            return acc
        acc = lax.fori_loop(0, nchunks // unroll, body, jnp.zeros((rows, LANES), F32))
        return jnp.sum(acc, axis=1, keepdims=True)

    def bit_body(i, t):
        cand = t + lax.shift_left(jnp.int32(1), 31 - i)
        return jnp.where(count_ge(cand) >= k, cand, t)

    return lax.fori_loop(0, 32, bit_body, jnp.full((rows, 1), INT_MIN, I32))


def _idx_prompt_body(q_ref, w_ref, ka_ref, kb_ref, keys_ref, thr_ref, *, tq, kc, n_sel, row_group):
    qi = pl.program_id(0)
    nck = keys_ref.shape[0]
    per = kc // LANES
    n_super = ((qi + 1) * tq + kc - 1) // kc

    def super_body(sc, carry):
        k0 = pl.multiple_of(sc * kc, kc)
        ka = ka_ref[pl.ds(k0, kc), :]
        kb = kb_ref[pl.ds(k0, kc), :]
        acc = jnp.zeros((tq, kc), F32)
        for j in range(IDX_QW // LANES):
            q2 = q_ref[:, j * LANES:(j + 1) * LANES]
            acc = acc + w_ref[:, 2 * j:2 * j + 1] * jnp.maximum(_nt_dot(q2, ka), 0.0)
            acc = acc + w_ref[:, 2 * j + 1:2 * j + 2] * jnp.maximum(_nt_dot(q2, kb), 0.0)
        qpos = qi * tq + lax.broadcasted_iota(I32, (tq, kc), 0)
        kpos = sc * kc + lax.broadcasted_iota(I32, (tq, kc), 1)
        key = jnp.where(kpos <= qpos, _float_key(acc), INT_MIN)
        for c in range(per):
            keys_ref[sc * per + c] = key[:, c * LANES:(c + 1) * LANES]
        return carry

    lax.fori_loop(0, n_super, super_body, 0)

    def fill_body(c, carry):
        keys_ref[c] = jnp.full((tq, LANES), INT_MIN, I32)
        return carry

    lax.fori_loop(n_super * per, nck, fill_body, 0)
    for rg in range(tq // row_group):
        thr_ref[rg * row_group:(rg + 1) * row_group, :] = _kth_key(
            keys_ref, n_super * per, n_sel, rg * row_group, row_group, unroll=per)


def _idx_prompt(qi_bf, wi, kia, kib, n_sel, tq=256, kc=512):
    T = qi_bf.shape[0]
    tq = _tile(T, tq)
    kc = _tile(T, kc)
    nck = T // LANES
    row_group = min(tq, 128)
    keys, thr = pl.pallas_call(
        functools.partial(_idx_prompt_body, tq=tq, kc=kc, n_sel=n_sel, row_group=row_group),
        grid=(T // tq,),
        in_specs=[pl.BlockSpec((tq, IDX_QW), lambda i: (i, 0)),
                  pl.BlockSpec((tq, IDX_HEADS), lambda i: (i, 0)),
                  pl.BlockSpec((T, LANES), lambda i: (0, 0)),
                  pl.BlockSpec((T, LANES), lambda i: (0, 0))],
        out_specs=[pl.BlockSpec((None, nck, tq, LANES), lambda i: (i, 0, 0, 0)),
                   pl.BlockSpec((tq, 1), lambda i: (i, 0))],
        out_shape=[jax.ShapeDtypeStruct((T // tq, nck, tq, LANES), I32),
                   jax.ShapeDtypeStruct((T, 1), I32)],
        compiler_params=_cp(("parallel",)), name="idx_prompt",
    )(qi_bf, wi, kia, kib)
    return keys, thr, tq


def _dsa_prompt_body(qi_ref, ki_ref, q_ref, k_ref, vt_ref, keys_ref, thr_ref, o_ref, m_ref, l_ref, acc_ref, *, t):
    p = pl.program_id(0)
    qi, ki = qi_ref[p], ki_ref[p]

    @pl.when(ki == 0)
    def _():
        _init_softmax_state(m_ref, l_ref, acc_ref)

    def run(diag):
        for r, c, msk in _tile_plan(t, diag):
            qc = slice(r * QS, (r + 1) * QS)
            kc = slice(c * KS, (c + 1) * KS)
            keys = jnp.concatenate([keys_ref[c * (KS // LANES) + j, qc, :] for j in range(KS // LANES)], axis=1)
            bias = jnp.where(keys >= thr_ref[qc, :], 0.0, NEG).T
            if msk:
                bias = jnp.where(_tile_causal_t(r, c), bias, NEG)
            for g0 in range(0, DSA_KV_HEADS, GROUPS):
                hs = [g * DSA_GROUP + i for g in range(g0, g0 + GROUPS) for i in range(DSA_GROUP)]
                kvs = [slice((h // DSA_GROUP) * HEAD_DIM, (h // DSA_GROUP + 1) * HEAD_DIM) for h in hs]
                hds = [slice(h * HEAD_DIM, (h + 1) * HEAD_DIM) for h in hs]
                n = len(hs)
                mo, lo, ao = _attn_tiles_t(
                    [k_ref[kc, kv] for kv in kvs], [q_ref[qc, hd] for hd in hds], [vt_ref[kv, kc] for kv in kvs],
                    [m_ref[h:h + 1, qc] for h in hs], [l_ref[h:h + 1, qc] for h in hs],
                    [acc_ref[hd, qc] for hd in hds], biases=[bias] * n)
                for i, h in enumerate(hs):
                    m_ref[h:h + 1, qc] = mo[i]
                    l_ref[h:h + 1, qc] = lo[i]
                    acc_ref[hds[i], qc] = ao[i]

    @pl.when(ki == qi)
    def _():
        run(True)
        for h in range(DSA_HEADS):
            hd = slice(h * HEAD_DIM, (h + 1) * HEAD_DIM)
            o_ref[:, hd] = (acc_ref[hd, :] / l_ref[h:h + 1, :]).T.astype(BF16)

    @pl.when(ki != qi)
    def _():
        run(False)


def _dsa_prompt(q, k, vt, keys, thr, t):
    T = q.shape[0]
    qs, ks = _causal_pairs(T // t)
    per = t // LANES
    grid_spec = pltpu.PrefetchScalarGridSpec(
        num_scalar_prefetch=2, grid=(len(qs),),
        in_specs=[pl.BlockSpec((t, DSA_QW), lambda p, qi, ki: (qi[p], 0)),
                  pl.BlockSpec((t, DSA_KW), lambda p, qi, ki: (ki[p], 0)),
                  pl.BlockSpec((DSA_KW, t), lambda p, qi, ki: (0, ki[p])),
                  pl.BlockSpec((None, per, t, LANES), lambda p, qi, ki: (qi[p], ki[p], 0, 0)),
                  pl.BlockSpec((t, 1), lambda p, qi, ki: (qi[p], 0))],
        out_specs=pl.BlockSpec((t, DSA_QW), lambda p, qi, ki: (qi[p], 0)),
        scratch_shapes=[pltpu.VMEM((DSA_HEADS, t), F32), pltpu.VMEM((DSA_HEADS, t), F32),
                        pltpu.VMEM((DSA_QW, t), F32)])
    return pl.pallas_call(
        functools.partial(_dsa_prompt_body, t=t), grid_spec=grid_spec,
        out_shape=jax.ShapeDtypeStruct((T, DSA_QW), BF16),
        compiler_params=_cp(("arbitrary",)), name="dsa_prompt",
    )(jnp.asarray(qs), jnp.asarray(ks), q, k, vt, keys, thr)


TPAD = 8


def _head_rows(ref, h, n_heads):
    return ref[pl.ds(h, ref.shape[0] // n_heads, stride=n_heads), :].astype(BF16)


def _decode_step(qs, kget, vget, n_blocks, bias, state, rows_per_q):
    m_ref, l_ref, acc_ref = state
    n_q = len(qs)
    s = jnp.concatenate(
        [jnp.concatenate([_nt_dot(qs[j], kget(b, j)) for j in range(n_q)], axis=0) for b in range(n_blocks)],
        axis=1) * ATT_SCALE + bias
    m_prev = m_ref[...]
    m_new = jnp.maximum(m_prev, jnp.max(s, axis=-1, keepdims=True))
    alpha = jnp.exp(m_prev - m_new)
    p = jnp.exp(s - m_new)
    l_ref[...] = alpha * l_ref[...] + jnp.sum(p, axis=-1, keepdims=True)
    m_ref[...] = m_new
    pv = []
    for j in range(n_q):
        rows = slice(j * rows_per_q, (j + 1) * rows_per_q)
        d = None
        for b in range(n_blocks):
            t = jnp.dot(p[rows, b * LANES:(b + 1) * LANES].astype(BF16), vget(b, j), preferred_element_type=F32)
            d = t if d is None else d + t
        pv.append(d)
    acc_ref[...] = alpha * acc_ref[...] + jnp.concatenate(pv, axis=0)


def _fox_sample_body(pt_ref, q_ref, cq_ref, ck_ref, *refs, G):
    k_refs, v_refs = refs[:G], refs[G:2 * G]
    kn_ref, vn_ref, ckn_ref, o_ref, m_ref, l_ref, acc_ref = refs[2 * G:]
    p = pl.program_id(1)
    state = (m_ref, l_ref, acc_ref)

    @pl.when(p == 0)
    def _():
        _init_softmax_state(*state)

    qs = [q_ref[h].astype(BF16) for h in range(FOX_HEADS)]
    cq = cq_ref[...]

    def forget_bias(ck):
        return cq - jnp.concatenate(
            [jnp.broadcast_to(ck[h:h + 1, :], (TPAD, ck.shape[1])) for h in range(FOX_HEADS)], axis=0)

    _decode_step(qs, lambda b, h: _head_rows(k_refs[b], h, FOX_HEADS), lambda b, h: _head_rows(v_refs[b], h, FOX_HEADS),
                 G, forget_bias(ck_ref[...]), state, TPAD)

    @pl.when(p == pl.num_programs(1) - 1)
    def _():
        rows = FOX_HEADS * TPAD
        tok = lax.broadcasted_iota(I32, (rows, LANES), 0) % TPAD
        col = lax.broadcasted_iota(I32, (rows, LANES), 1)
        bias = jnp.where(col <= tok, forget_bias(ckn_ref[...]), NEG)
        _decode_step(qs, lambda b, h: kn_ref[h], lambda b, h: vn_ref[h], 1, bias, state, TPAD)
        o_ref[...] = acc_ref[...] / l_ref[...]


def _fox_sample(q, cq, ckT, cache_k, cache_v, page_table, knew, vnew, cknT, group=8):
    B, n_pages = page_table.shape
    R = cache_k.shape[1]
    page = R // FOX_HEADS
    rows = FOX_HEADS * TPAD
    G = _page_group(n_pages, group)
    per_b3 = lambda b, p, pt: (b, 0, 0)
    per_b4 = lambda b, p, pt: (b, 0, 0, 0)
    grid_spec = pltpu.PrefetchScalarGridSpec(
        num_scalar_prefetch=1, grid=(B, n_pages // G),
        in_specs=[pl.BlockSpec((None, FOX_HEADS, TPAD, HEAD_DIM), per_b4),
                  pl.BlockSpec((None, rows, 1), per_b3),
                  pl.BlockSpec((None, FOX_HEADS, G * page), lambda b, p, pt: (b, 0, p))]
        + _paged_specs(G, n_pages, (None, R, HEAD_DIM)) + _paged_specs(G, n_pages, (None, R, HEAD_DIM))
        + [pl.BlockSpec((None, FOX_HEADS, LANES, HEAD_DIM), per_b4),
           pl.BlockSpec((None, FOX_HEADS, LANES, HEAD_DIM), per_b4),
           pl.BlockSpec((None, FOX_HEADS, LANES), per_b3)],
        out_specs=pl.BlockSpec((None, rows, HEAD_DIM), per_b3),
        scratch_shapes=[pltpu.VMEM((rows, 1), F32), pltpu.VMEM((rows, 1), F32),
                        pltpu.VMEM((rows, HEAD_DIM), F32)])
    return pl.pallas_call(
        functools.partial(_fox_sample_body, G=G), grid_spec=grid_spec,
        out_shape=jax.ShapeDtypeStruct((B, rows, HEAD_DIM), F32),
        compiler_params=_cp(("parallel", "arbitrary")), name="fox_sample",
    )(page_table.reshape(-1).astype(I32), q, cq, ckT, *([cache_k] * G), *([cache_v] * G), knew, vnew, cknT)


def _idx_sample_body(pt_ref, q_ref, w_ref, *refs, ts, n_sel, G):
    k_refs = refs[:G]
    kn_ref, keys_ref, thr_ref = refs[G:]
    p = pl.program_id(1)
    n_pages = keys_ref.shape[0] - 1
    pad = jnp.full((TPAD - ts, LANES), INT_MIN, I32)

    def score(k):
        r = jnp.maximum(_nt_dot(q_ref[...], k), 0.0) * w_ref[...]
        return jnp.sum(r.reshape(ts, IDX_HEADS, LANES), axis=1)

    for g in range(G):
        keys_ref[p * G + g] = jnp.concatenate([_float_key(score(k_refs[g][...].astype(BF16))), pad], axis=0)

    @pl.when(p == pl.num_programs(1) - 1)
    def _():
        tok = lax.broadcasted_iota(I32, (ts, LANES), 0)
        col = lax.broadcasted_iota(I32, (ts, LANES), 1)
        kn = jnp.where(col <= tok, _float_key(score(kn_ref[...])), INT_MIN)
        keys_ref[n_pages] = jnp.concatenate([kn, pad], axis=0)
        thr_ref[...] = _kth_key(keys_ref, n_pages + 1, n_sel, 0, 8)


def _idx_sample(q, w, cache_k, page_table, knew, ts, n_sel, group=64):
    B, n_pages = page_table.shape
    R = cache_k.shape[1]
    rows = ts * IDX_HEADS
    G = _page_group(n_pages, group)
    per_b = lambda b, p, pt: (b, 0, 0)
    grid_spec = pltpu.PrefetchScalarGridSpec(
        num_scalar_prefetch=1, grid=(B, n_pages // G),
        in_specs=[pl.BlockSpec((None, rows, IDX_DIM), per_b),
                  pl.BlockSpec((None, rows, 1), per_b)]
        + _paged_specs(G, n_pages, (None, R, IDX_DIM))
        + [pl.BlockSpec((None, LANES, IDX_DIM), per_b)],
        out_specs=[pl.BlockSpec((None, n_pages + 1, 8, LANES), lambda b, p, pt: (b, 0, 0, 0)),
                   pl.BlockSpec((None, 8, 1), per_b)])
    return pl.pallas_call(
        functools.partial(_idx_sample_body, ts=ts, n_sel=n_sel, G=G), grid_spec=grid_spec,
        out_shape=[jax.ShapeDtypeStruct((B, n_pages + 1, 8, LANES), I32),
                   jax.ShapeDtypeStruct((B, 8, 1), I32)],
        compiler_params=_cp(("parallel", "arbitrary")), name="idx_sample",
    )(page_table.reshape(-1).astype(I32), q, w, *([cache_k] * G), knew)


def _dsa_sample_body(pt_ref, q_ref, keys_ref, keysn_ref, thr_ref, *refs, G):
    k_refs, v_refs = refs[:G], refs[G:2 * G]
    kn_ref, vn_ref, o_ref, m_ref, l_ref, acc_ref = refs[2 * G:]
    p = pl.program_id(1)
    state = (m_ref, l_ref, acc_ref)

    @pl.when(p == 0)
    def _():
        _init_softmax_state(*state)

    qs = [q_ref[g].astype(BF16) for g in range(DSA_KV_HEADS)]
    thr = thr_ref[...]

    def select_bias(keys):
        sel = jnp.concatenate([jnp.where(kk >= thr, 0.0, NEG) for kk in keys], axis=1)
        return jnp.concatenate([sel] * DSA_HEADS, axis=0)

    _decode_step(qs, lambda b, g: _head_rows(k_refs[b], g, DSA_KV_HEADS),
                 lambda b, g: _head_rows(v_refs[b], g, DSA_KV_HEADS),
                 G, select_bias([keys_ref[b] for b in range(G)]), state, DSA_GROUP * TPAD)

    @pl.when(p == pl.num_programs(1) - 1)
    def _():
        _decode_step(qs, lambda b, g: kn_ref[g], lambda b, g: vn_ref[g], 1, select_bias([keysn_ref[...]]),
                     state, DSA_GROUP * TPAD)
        o_ref[...] = acc_ref[...] / l_ref[...]


def _dsa_sample(q, keys, thr, cache_k, cache_v, page_table, knew, vnew, group=16):
    B, n_pages = page_table.shape
    R = cache_k.shape[1]
    rows = DSA_HEADS * TPAD
    G = _page_group(n_pages, group)
    per_b3 = lambda b, p, pt: (b, 0, 0)
    per_b4 = lambda b, p, pt: (b, 0, 0, 0)
    grid_spec = pltpu.PrefetchScalarGridSpec(
        num_scalar_prefetch=1, grid=(B, n_pages // G),
        in_specs=[pl.BlockSpec((None, DSA_KV_HEADS, DSA_GROUP * TPAD, HEAD_DIM), per_b4),
                  pl.BlockSpec((None, G, TPAD, LANES), lambda b, p, pt: (b, p, 0, 0)),
                  pl.BlockSpec((None, None, TPAD, LANES), lambda b, p, pt: (b, n_pages, 0, 0)),
                  pl.BlockSpec((None, TPAD, 1), per_b3)]
        + _paged_specs(G, n_pages, (None, R, HEAD_DIM)) + _paged_specs(G, n_pages, (None, R, HEAD_DIM))
        + [pl.BlockSpec((None, DSA_KV_HEADS, LANES, HEAD_DIM), per_b4),
           pl.BlockSpec((None, DSA_KV_HEADS, LANES, HEAD_DIM), per_b4)],
        out_specs=pl.BlockSpec((None, rows, HEAD_DIM), per_b3),
        scratch_shapes=[pltpu.VMEM((rows, 1), F32), pltpu.VMEM((rows, 1), F32),
                        pltpu.VMEM((rows, HEAD_DIM), F32)])
    return pl.pallas_call(
        functools.partial(_dsa_sample_body, G=G), grid_spec=grid_spec,
        out_shape=jax.ShapeDtypeStruct((B, rows, HEAD_DIM), F32),
        compiler_params=_cp(("parallel", "arbitrary")), name="dsa_sample",
    )(page_table.reshape(-1).astype(I32), q, keys, keys, thr, *([cache_k] * G), *([cache_v] * G), knew, vnew)


def _top_rows(x, n):
    out = []
    for _ in range(n):
        m = jnp.max(x, axis=0, keepdims=True)
        out.append(m)
        x = jnp.where(x == m, -jnp.inf, x)
    return out


def _peer_select_body(q_ref, keys_ref, s0_ref, s1_ref, st_ref, *, n_heads):
    nk = keys_ref.shape[2]
    thr, off = [], []
    for h in range(n_heads):
        tops = []
        for c, s_ref in enumerate((s0_ref, s1_ref)):
            col = (2 * h + c) * nk
            st = _nt_dot(q_ref[:, col:col + nk].astype(BF16), keys_ref[h, c]).T
            s_ref[h * nk:(h + 1) * nk, :] = st
            tops.append(jnp.concatenate(_top_rows(st, PEER_TOPK), axis=0))
        cand = jnp.concatenate([tops[0][r:r + 1, :] + tops[1] for r in range(PEER_TOPK)], axis=0)
        best = _top_rows(cand, PEER_TOPK)
        z = jnp.ones_like(best[0])
        for r in range(1, PEER_TOPK):
            z = z + jnp.exp(best[r] - best[0])
        thr.append(best[-1])
        off.append(-(best[0] + jnp.log(z)))
    st_ref[...] = jnp.concatenate(thr + off, axis=0)


def _peer_select(q, keys_bf, tm=128):
    M = q.shape[0]
    n_heads, _, nk, _ = keys_bf.shape
    tm = _tile(M, tm)
    col = lambda i: (0, i)
    return pl.pallas_call(
        functools.partial(_peer_select_body, n_heads=n_heads),
        grid=(M // tm,),
        in_specs=[pl.BlockSpec((tm, q.shape[1]), lambda i: (i, 0)),
                  pl.BlockSpec(keys_bf.shape, lambda i: (0, 0, 0, 0))],
        out_specs=[pl.BlockSpec((n_heads * nk, tm), col), pl.BlockSpec((n_heads * nk, tm), col),
                   pl.BlockSpec((2 * n_heads, tm), col)],
        out_shape=[jax.ShapeDtypeStruct((n_heads * nk, M), F32), jax.ShapeDtypeStruct((n_heads * nk, M), F32),
                   jax.ShapeDtypeStruct((2 * n_heads, M), F32)],
        compiler_params=_cp(("parallel",)), name="peer_select",
    )(q, keys_bf)


def _peer_expert_body(x_ref, u_ref, v_ref, s0_ref, s1_ref, st_ref, o_ref, *, n_heads, nk, ni):
    e = pl.program_id(1)

    @pl.when(e == 0)
    def _():
        o_ref[...] = jnp.zeros_like(o_ref)

    a = _nt_dot(u_ref[...], x_ref[...])
    act = 0.5 * a * (1.0 + lax.erf(a * (2.0 ** -0.5)))
    gates = []
    for il in range(ni):
        w = None
        for h in range(n_heads):
            tot = s0_ref[il * n_heads + h:il * n_heads + h + 1, :] + s1_ref[h * nk:(h + 1) * nk, :]
            val = jnp.where(tot >= st_ref[h:h + 1, :], jnp.exp(tot + st_ref[n_heads + h:n_heads + h + 1, :]), 0.0)
            w = val if w is None else w + val
        gates.append(w)
    ga = (jnp.concatenate(gates, axis=0) * act).astype(BF16)
    o_ref[...] += lax.dot_general(ga, v_ref[...], (((0,), (0,)), ((), ())), preferred_element_type=F32)


def _peer_expert(hn_bf, u_bf, v_bf, s0t, s1t, st, n_heads, nk, tm=512, te=512):
    M, D = hn_bf.shape
    E = u_bf.shape[0]
    tm = _tile(M, tm)
    te = _tile(E, te)
    ni = te // nk
    s0g = jnp.transpose(s0t.reshape(n_heads, E // te, ni, M), (1, 2, 0, 3)).reshape(E // te, ni * n_heads, M)
    return pl.pallas_call(
        functools.partial(_peer_expert_body, n_heads=n_heads, nk=nk, ni=ni),
        grid=(M // tm, E // te),
        in_specs=[pl.BlockSpec((tm, D), lambda i, e: (i, 0)),
                  pl.BlockSpec((te, D), lambda i, e: (e, 0)),
                  pl.BlockSpec((te, D), lambda i, e: (e, 0)),
                  pl.BlockSpec((None, ni * n_heads, tm), lambda i, e: (e, 0, i)),
                  pl.BlockSpec((n_heads * nk, tm), lambda i, e: (0, i)),
                  pl.BlockSpec((2 * n_heads, tm), lambda i, e: (0, i))],
        out_specs=pl.BlockSpec((tm, D), lambda i, e: (i, 0)),
        out_shape=jax.ShapeDtypeStruct((M, D), F32),
        compiler_params=_cp(("parallel", "arbitrary")), name="peer_expert",
    )(hn_bf, u_bf, v_bf, s0g, s1t, st)


def _prep_w_in(w_in):
    o = np.cumsum([0, FOX_W, FOX_W, FOX_W, FOX_HEADS, DSA_QW, DSA_KW, DSA_KW, IDX_QW, IDX_DIM, IDX_HEADS])
    seg = lambda i: w_in[:, o[i]:o[i + 1]]
    cols = [seg(i).astype(BF16) for i in (0, 1, 2, 4, 7, 5, 6, 8, 9, 3)]
    used = int(o[-1])
    total = -(-used // 768) * 768
    cols.append(jnp.zeros((w_in.shape[0], total - used), BF16))
    return jnp.concatenate(cols, axis=1)


def _channel(x, o_fox, o_dsa, p_l, w_o_bf, g_norm2, w_pq_bf, keys_bf, u_bf, v_bf, g_norm3, w_gate_bf, w_proj_bf,
             g_ple):
    n_heads, _, nk, _ = keys_bf.shape
    h1 = _mm2_res(o_fox, o_dsa, w_o_bf, x)
    q, hn_bf = _mm(h1, w_pq_bf, g=g_norm2, emit_xn=True)
    s0t, s1t, st = _peer_select(q, keys_bf)
    moe = _peer_expert(hn_bf, u_bf, v_bf, s0t, s1t, st, n_heads, nk)
    pn = _rownorm_mm(p_l, w_proj_bf, g_ple)
    return _mm(h1, w_gate_bf, g=g_norm3, x2=moe, res=h1, res2=moe, aux=pn, epi="gate", tm=256)


def _head_major(x, B, ts, n_heads, n_rows):
    x4 = jnp.transpose(x.reshape(B, ts, n_heads, HEAD_DIM), (0, 2, 1, 3))
    return jnp.pad(x4, ((0, 0), (0, 0), (0, n_rows - ts), (0, 0)))


def _token_major(o, B, ts, n_heads):
    o4 = o.reshape(B, n_heads, TPAD, HEAD_DIM)[:, :, :ts]
    return jnp.transpose(o4, (0, 2, 1, 3)).reshape(B * ts, n_heads * HEAD_DIM)


def _pad_rows(x, n):
    return jnp.pad(x, ((0, 0), (0, n - x.shape[1]), (0, 0)))


def kernel(x_prompt, x_sample, cache_fox_k, cache_fox_v, cache_fox_logf, cache_dsa_k, cache_dsa_v, cache_idx_k, page_table, p_prompt, p_sample, g_norm1, w_in, b_f, g_q_fox, g_k_fox, g_q_dsa, g_k_dsa, w_o, g_norm2, w_peer_q, peer_keys, peer_u, peer_v, g_norm3, w_ple_gate, w_ple_proj, g_ple):
    Bp, Tp, D = x_prompt.shape
    Bs, Ts = x_sample.shape[:2]
    depth = w_in.shape[0]
    n_pages = page_table.shape[1]
    page = cache_fox_k.shape[2]
    past = n_pages * page
    n_phys = cache_fox_k.shape[1]
    assert Bp == 1 and Ts <= TPAD and page == LANES

    def pool(cache):
        return cache.reshape((depth * n_phys, -1) + cache.shape[-1:])

    fox_k_pool, fox_v_pool, dsa_k_pool, dsa_v_pool = pool(cache_fox_k), pool(cache_fox_v), pool(cache_dsa_k), pool(cache_dsa_v)
    fox_lf_pool = cache_fox_logf.reshape(depth * n_phys, page, FOX_HEADS)
    idx_k_pool = cache_idx_k.reshape(depth * n_phys, page, IDX_DIM)

    h_p = x_prompt.reshape(Tp, D)
    h_s = x_sample.reshape(Bs * Ts, D)
    pos_p = jnp.arange(Tp, dtype=I32)
    pos_s = jnp.tile(past + jnp.arange(Ts, dtype=I32), Bs)
    outs = [[] for _ in range(12)]
    for l in range(depth):
        w_in_bf = _prep_w_in(w_in[l])
        chan_w = (w_o[l].astype(BF16), g_norm2[l], w_peer_q[l].astype(BF16), peer_keys[l].astype(BF16),
                  peer_u[l].astype(BF16), peer_v[l].astype(BF16), g_norm3[l], w_ple_gate[l].astype(BF16),
                  w_ple_proj[l].astype(BF16), g_ple[l])
        norm_w = (b_f[l], g_q_fox[l], g_k_fox[l], g_q_dsa[l], g_k_dsa[l])
        pt_l = page_table + l * n_phys

        z = _mm(h_p, w_in_bf, g=g_norm1[l], tn=768)
        pr = _post(z, pos_p, *norm_w, ATT_SCALE * LOG2E)
        ident = jnp.arange(Tp // page, dtype=I32).reshape(1, -1)
        c, _ = _paged_cumsum(pr["lf"].reshape(Tp // page, page, FOX_HEADS), ident,
                             jnp.zeros((1, 8, FOX_HEADS), F32))
        o_fox = _fox_prompt(pr["qf"], pr["kfb"], pr["vft"], c[0])
        keys, thr, tq = _idx_prompt(pr["qi"], pr["wi"], pr["kia"], pr["kib"], min(TOPK_MAX, Tp // 4))
        o_dsa = _dsa_prompt(pr["qd"], pr["kdb"], pr["vdt"], keys, thr, tq)
        h_p = _channel(h_p, o_fox, o_dsa, p_prompt[l].reshape(Tp, -1), *chan_w)
        for i, (name, shape) in enumerate((("kf", (Bp, Tp, FOX_HEADS, HEAD_DIM)), ("vf", (Bp, Tp, FOX_HEADS, HEAD_DIM)),
                                           ("lf", (Bp, Tp, FOX_HEADS)), ("kd", (Bp, Tp, DSA_KV_HEADS, HEAD_DIM)),
                                           ("vd", (Bp, Tp, DSA_KV_HEADS, HEAD_DIM)), ("ki", (Bp, Tp, IDX_DIM)))):
            outs[i].append(pr[name].reshape(shape))

        zs = _mm(h_s, w_in_bf, g=g_norm1[l], tn=768)
        sr = _post(zs, pos_s, *norm_w, 1.0)
        lf_new = _pad_rows(sr["lf"].reshape(Bs, Ts, FOX_HEADS), TPAD)
        c_past, c_new = _paged_cumsum(fox_lf_pool, pt_l, lf_new)
        cq = jnp.transpose(c_new, (0, 2, 1)).reshape(Bs, FOX_HEADS * TPAD, 1)
        ckT = jnp.transpose(c_past, (0, 2, 1))
        cknT = jnp.transpose(_pad_rows(c_new, LANES), (0, 2, 1))
        o = _fox_sample(_head_major(sr["qf"], Bs, Ts, FOX_HEADS, TPAD).astype(F32), cq, ckT, fox_k_pool, fox_v_pool, pt_l,
                        _head_major(sr["kfb"], Bs, Ts, FOX_HEADS, LANES), _head_major(sr["vfb"], Bs, Ts, FOX_HEADS, LANES),
                        cknT)
        o_fox_s = _token_major(o, Bs, Ts, FOX_HEADS)
        n_sel = min(TOPK_MAX, (past + Ts) // 4)
        keys_s, thr_s = _idx_sample(sr["qi"].reshape(Bs, Ts * IDX_HEADS, IDX_DIM),
                                    sr["wi"].reshape(Bs, Ts * IDX_HEADS, 1), idx_k_pool, pt_l,
                                    _pad_rows(sr["kia"][:, :IDX_DIM].reshape(Bs, Ts, IDX_DIM), LANES), Ts, n_sel)
        qd = _head_major(sr["qd"], Bs, Ts, DSA_HEADS, TPAD).astype(F32)
        od = _dsa_sample(qd.reshape(Bs, DSA_KV_HEADS, DSA_GROUP * TPAD, HEAD_DIM), keys_s, thr_s,
                         dsa_k_pool, dsa_v_pool, pt_l, _head_major(sr["kdb"], Bs, Ts, DSA_KV_HEADS, LANES),
                         _head_major(sr["vdb"], Bs, Ts, DSA_KV_HEADS, LANES))
        o_dsa_s = _token_major(od, Bs, Ts, DSA_HEADS)
        h_s = _channel(h_s, o_fox_s.astype(BF16), o_dsa_s.astype(BF16), p_sample[l].reshape(Bs * Ts, -1), *chan_w)
        for i, (name, shape) in enumerate((("kf", (Bs, Ts, FOX_HEADS, HEAD_DIM)), ("vf", (Bs, Ts, FOX_HEADS, HEAD_DIM)),
                                           ("lf", (Bs, Ts, FOX_HEADS)), ("kd", (Bs, Ts, DSA_KV_HEADS, HEAD_DIM)),
                                           ("vd", (Bs, Ts, DSA_KV_HEADS, HEAD_DIM)), ("ki", (Bs, Ts, IDX_DIM)))):
            outs[6 + i].append(sr[name].reshape(shape))
    return (h_p.reshape(Bp, Tp, D), h_s.reshape(Bs, Ts, D)) + tuple(jnp.stack(o) for o in outs)
```

```python
import functools
import math

import numpy as np
import jax
import jax.numpy as jnp
from jax import lax
from jax.experimental import pallas as pl
from jax.experimental.pallas import tpu as pltpu

F32 = jnp.float32
BF16 = jnp.bfloat16
I32 = jnp.int32

HEAD_DIM = 128
FOX_HEADS = 16
DSA_HEADS = 16
DSA_KV_HEADS = 4
DSA_GROUP = DSA_HEADS // DSA_KV_HEADS
IDX_HEADS = 32
IDX_DIM = 64
TOPK_MAX = 256
ROPE_THETA = 10000.0
PEER_TOPK = 16
EPS = 1e-6

FOX_W = FOX_HEADS * HEAD_DIM
DSA_QW = DSA_HEADS * HEAD_DIM
DSA_KW = DSA_KV_HEADS * HEAD_DIM
IDX_QW = IDX_HEADS * IDX_DIM

LANES = 128
NEG = -1e30
INT_MIN = -(2 ** 31)
VMEM_LIMIT = 56 * 1024 * 1024
ATT_SCALE = HEAD_DIM ** -0.5
LOG2E = math.log2(math.e)
QS = 128
KS = 256
WIDTH = 4
GROUPS = 2
FOX_TOGETHER = 2


def _cp(sem):
    return pltpu.CompilerParams(dimension_semantics=sem, vmem_limit_bytes=VMEM_LIMIT)


def _tile(n, pref):
    t = min(n, pref)
    while n % t:
        t //= 2
    return t


def _nt_dot(a, b):
    return lax.dot_general(a, b, (((1,), (1,)), ((), ())), preferred_element_type=F32)


def _mm_body(*refs, norm, add2, epi, emit_xn, prologue):
    it = iter(refs)
    x_ref = next(it)
    x2_ref = next(it) if add2 else None
    g_ref = next(it) if norm else None
    w_ref = next(it)
    res_ref = next(it) if epi in ("res", "gate") else None
    res2_ref = next(it) if (epi == "gate" and add2) else None
    aux_ref = next(it) if epi == "gate" else None
    o_ref = next(it)
    xo_ref = next(it) if emit_xn else None
    xn_ref = next(it) if prologue else None

    if prologue:
        @pl.when(pl.program_id(1) == 0)
        def _():
            x = x_ref[...].astype(F32)
            if add2:
                x = x + x2_ref[...]
            if norm:
                ms = jnp.mean(x * x, axis=-1, keepdims=True)
                x = x * lax.rsqrt(ms + EPS) * g_ref[...]
            xb = x.astype(BF16)
            xn_ref[...] = xb
            if emit_xn:
                xo_ref[...] = xb
        lhs = xn_ref[...]
    else:
        lhs = x_ref[...]
    acc = jnp.dot(lhs, w_ref[...], preferred_element_type=F32)
    if epi == "res":
        acc = res_ref[...] + acc
    elif epi == "gate":
        r = res_ref[...]
        if add2:
            r = r + res2_ref[...]
        acc = r + jax.nn.sigmoid(acc) * aux_ref[...]
    o_ref[...] = acc


def _mm(x, w, *, g=None, x2=None, res=None, res2=None, aux=None, epi="none", emit_xn=False,
        tm=512, tn=512):
    M, K = x.shape
    N = w.shape[1]
    tm = _tile(M, tm)
    tn = _tile(N, tn)
    norm = g is not None
    add2 = x2 is not None
    prologue = norm or add2 or x.dtype != BF16
    row = pl.BlockSpec((tm, K), lambda i, j: (i, 0))
    blk = pl.BlockSpec((tm, tn), lambda i, j: (i, j))
    ins, specs = [x], [row]
    if add2:
        ins.append(x2); specs.append(row)
    if norm:
        ins.append(g.reshape(1, K)); specs.append(pl.BlockSpec((1, K), lambda i, j: (0, 0)))
    ins.append(w); specs.append(pl.BlockSpec((K, tn), lambda i, j: (0, j)))
    if epi in ("res", "gate"):
        ins.append(res); specs.append(blk)
    if epi == "gate" and add2:
        ins.append(res2); specs.append(blk)
    if epi == "gate":
        ins.append(aux); specs.append(blk)
    out_shape = [jax.ShapeDtypeStruct((M, N), F32)]
    out_specs = [blk]
    if emit_xn:
        out_shape.append(jax.ShapeDtypeStruct((M, K), BF16))
        out_specs.append(row)
    scratch = [pltpu.VMEM((tm, K), BF16)] if prologue else []
    outs = pl.pallas_call(
        functools.partial(_mm_body, norm=norm, add2=add2, epi=epi, emit_xn=emit_xn, prologue=prologue),
        grid=(M // tm, N // tn),
        in_specs=specs, out_specs=out_specs, out_shape=out_shape, scratch_shapes=scratch,
        compiler_params=_cp(("parallel", "arbitrary")), name="mm_" + epi,
    )(*ins)
    return outs if emit_xn else outs[0]


def _mm2_res_body(x1_ref, x2_ref, w1_ref, w2_ref, res_ref, o_ref):
    o_ref[...] = res_ref[...] + (jnp.dot(x1_ref[...], w1_ref[...], preferred_element_type=F32)
                                 + jnp.dot(x2_ref[...], w2_ref[...], preferred_element_type=F32))


def _mm2_res(x1, x2, w, res, tm=512, tn=512):
    M, K1 = x1.shape
    N = w.shape[1]
    assert x2.shape == (M, K1) and w.shape[0] == 2 * K1
    tm = _tile(M, tm)
    tn = _tile(N, tn)
    row = pl.BlockSpec((tm, K1), lambda i, j: (i, 0))
    blk = pl.BlockSpec((tm, tn), lambda i, j: (i, j))
    return pl.pallas_call(
        _mm2_res_body,
        grid=(M // tm, N // tn),
        in_specs=[row, row, pl.BlockSpec((K1, tn), lambda i, j: (0, j)), pl.BlockSpec((K1, tn), lambda i, j: (1, j)), blk],
        out_specs=blk, out_shape=jax.ShapeDtypeStruct((M, N), F32),
        compiler_params=_cp(("parallel", "arbitrary")), name="mm2_res",
    )(x1, x2, w, w, res)


def _rownorm_mm_body(x_ref, w_ref, g_ref, o_ref):
    y = jnp.dot(x_ref[...].astype(BF16), w_ref[...], preferred_element_type=F32)
    ms = jnp.mean(y * y, axis=-1, keepdims=True)
    o_ref[...] = y * lax.rsqrt(ms + EPS) * g_ref[...]


def _rownorm_mm(x, w, g, tm=256):
    M, K = x.shape
    N = w.shape[1]
    tm = _tile(M, tm)
    return pl.pallas_call(
        _rownorm_mm_body,
        grid=(M // tm,),
        in_specs=[pl.BlockSpec((tm, K), lambda i: (i, 0)),
                  pl.BlockSpec((K, N), lambda i: (0, 0)),
                  pl.BlockSpec((1, N), lambda i: (0, 0))],
        out_specs=pl.BlockSpec((tm, N), lambda i: (i, 0)),
        out_shape=jax.ShapeDtypeStruct((M, N), F32),
        compiler_params=_cp(("parallel",)), name="rownorm_mm",
    )(x, w, g.reshape(1, N))


def _head_norm(x, g):
    ms = jnp.mean(x * x, axis=-1, keepdims=True)
    return x * lax.rsqrt(ms + EPS) * g


def _rope_full(x, cos, sin):
    return x * cos + pltpu.roll(x, HEAD_DIM // 2, 1) * sin


def _rope_pair(x, cos, sin, lane):
    half = IDX_DIM // 2
    rot = jnp.where((lane % IDX_DIM) < half, pltpu.roll(x, LANES - half, 1), pltpu.roll(x, half, 1))
    return x * cos + rot * sin


def _post_body(zqf, zkf, zvf, zqd, zqi, zkv, zs, cd, sd, ci, si, bf, gqf, gkf, gqd, gkd,
               qf_o, kf_o, kfb_o, vf_o, vfb_o, lf_o, qd_o, kd_o, kdb_o, vd_o, vdb_o,
               qi_o, ki_o, kia_o, kib_o, wi_o, vft_o, vdt_o, *, q_scale):
    tm = zqf.shape[0]
    cos_d, sin_d = cd[...], sd[...]
    cos_i, sin_i = ci[...], si[...]
    lane = lax.broadcasted_iota(I32, (tm, LANES), 1)
    for h in range(FOX_HEADS):
        sl = slice(h * HEAD_DIM, (h + 1) * HEAD_DIM)
        qf_o[:, sl] = (_head_norm(zqf[:, sl], gqf[...]) * q_scale).astype(BF16)
        k = _head_norm(zkf[:, sl], gkf[...])
        kf_o[:, sl] = k
        kfb_o[:, sl] = k.astype(BF16)
        v = zvf[:, sl]
        vf_o[:, sl] = v
        vfb_o[:, sl] = v.astype(BF16)
        vft_o[sl, :] = v.T.astype(BF16)
    for h in range(DSA_HEADS):
        sl = slice(h * HEAD_DIM, (h + 1) * HEAD_DIM)
        qd_o[:, sl] = (_rope_full(_head_norm(zqd[:, sl], gqd[...]), cos_d, sin_d) * q_scale).astype(BF16)
    for h in range(DSA_KV_HEADS):
        sl = slice(h * HEAD_DIM, (h + 1) * HEAD_DIM)
        k = _rope_full(_head_norm(zkv[:, sl], gkd[...]), cos_d, sin_d)
        kd_o[:, sl] = k
        kdb_o[:, sl] = k.astype(BF16)
        v = zkv[:, DSA_KW + h * HEAD_DIM: DSA_KW + (h + 1) * HEAD_DIM]
        vd_o[:, sl] = v
        vdb_o[:, sl] = v.astype(BF16)
        vdt_o[sl, :] = v.T.astype(BF16)
    for j in range(IDX_QW // LANES):
        sl = slice(j * LANES, (j + 1) * LANES)
        qi_o[:, sl] = _rope_pair(zqi[:, sl], cos_i, sin_i, lane).astype(BF16)
    x = zs[...]
    y = _rope_pair(x, cos_i, sin_i, lane)
    ki_o[...] = y[:, :IDX_DIM]
    ka = jnp.where(lane < IDX_DIM, y, 0.0)
    kia_o[...] = ka.astype(BF16)
    kib_o[...] = pltpu.roll(ka, IDX_DIM, 1).astype(BF16)
    wi_o[...] = x[:, IDX_DIM:IDX_DIM + IDX_HEADS] * (IDX_HEADS ** -0.5 * IDX_DIM ** -0.5)
    fl = x[:, IDX_DIM + IDX_HEADS:IDX_DIM + IDX_HEADS + FOX_HEADS] + bf[...]
    lf_o[...] = jnp.minimum(fl, 0.0) - jnp.log1p(jnp.exp(-jnp.abs(fl)))


def _post(z, pos, b_f, g_q_fox, g_k_fox, g_q_dsa, g_k_dsa, q_scale, tm=128):
    M = z.shape[0]
    tm = _tile(M, tm)
    posf = pos.astype(F32)[:, None]
    hd = HEAD_DIM // 2
    ang = posf * (ROPE_THETA ** (-jnp.arange(hd, dtype=F32) * 2.0 / HEAD_DIM))
    cd = jnp.concatenate([jnp.cos(ang), jnp.cos(ang)], axis=-1)
    sd = jnp.concatenate([-jnp.sin(ang), jnp.sin(ang)], axis=-1)
    hi = IDX_DIM // 2
    angi = posf * (ROPE_THETA ** (-jnp.arange(hi, dtype=F32) * 2.0 / IDX_DIM))
    ci = jnp.tile(jnp.cos(angi), (1, 4))
    si = jnp.tile(jnp.concatenate([-jnp.sin(angi), jnp.sin(angi)], axis=-1), (1, 2))

    def zspec(w, c):
        return pl.BlockSpec((tm, w), lambda i: (i, c))

    def rspec(w):
        return pl.BlockSpec((tm, w), lambda i: (i, 0))

    def cspec(w):
        return pl.BlockSpec((1, w), lambda i: (0, 0))

    small_col = (5 * FOX_W + 2 * DSA_KW) // LANES
    in_specs = [zspec(FOX_W, 0), zspec(FOX_W, 1), zspec(FOX_W, 2), zspec(DSA_QW, 3), zspec(IDX_QW, 4),
                zspec(2 * DSA_KW, 5 * FOX_W // (2 * DSA_KW)), zspec(LANES, small_col),
                rspec(LANES), rspec(LANES), rspec(LANES), rspec(LANES),
                cspec(FOX_HEADS), cspec(HEAD_DIM), cspec(HEAD_DIM), cspec(HEAD_DIM), cspec(HEAD_DIM)]
    outs = [(FOX_W, BF16), (FOX_W, F32), (FOX_W, BF16), (FOX_W, F32), (FOX_W, BF16), (FOX_HEADS, F32),
            (DSA_QW, BF16), (DSA_KW, F32), (DSA_KW, BF16), (DSA_KW, F32), (DSA_KW, BF16),
            (IDX_QW, BF16), (IDX_DIM, F32), (LANES, BF16), (LANES, BF16), (IDX_HEADS, F32)]
    tspec = lambda w: pl.BlockSpec((w, tm), lambda i: (0, i))
    res = pl.pallas_call(
        functools.partial(_post_body, q_scale=q_scale),
        grid=(M // tm,),
        in_specs=in_specs,
        out_specs=[rspec(w) for w, _ in outs] + [tspec(FOX_W), tspec(DSA_KW)],
        out_shape=[jax.ShapeDtypeStruct((M, w), d) for w, d in outs]
        + [jax.ShapeDtypeStruct((FOX_W, M), BF16), jax.ShapeDtypeStruct((DSA_KW, M), BF16)],
        compiler_params=_cp(("parallel",)), name="post_proj",
    )(z, z, z, z, z, z, z, cd, sd, ci, si, b_f.reshape(1, -1), g_q_fox.reshape(1, -1),
      g_k_fox.reshape(1, -1), g_q_dsa.reshape(1, -1), g_k_dsa.reshape(1, -1))
    names = ("qf", "kf", "kfb", "vf", "vfb", "lf", "qd", "kd", "kdb", "vd", "vdb", "qi", "ki", "kia",
             "kib", "wi", "vft", "vdt")
    return dict(zip(names, res))


def _split3(x):
    hi = x.astype(BF16)
    r1 = x - hi.astype(F32)
    mid = r1.astype(BF16)
    lo = (r1 - mid.astype(F32)).astype(BF16)
    return hi, mid, lo


def _tri_ones(n):
    r = lax.broadcasted_iota(I32, (n, n), 0)
    c = lax.broadcasted_iota(I32, (n, n), 1)
    return jnp.where(c <= r, 1.0, 0.0).astype(BF16)


def _tri_cumsum(tri, x):
    hi, mid, lo = _split3(x)
    dot = lambda a: jnp.dot(tri, a, preferred_element_type=F32)
    return dot(hi) + (dot(mid) + dot(lo))


def _page_group(n_pages, pref):
    g = min(n_pages, pref)
    while n_pages % g:
        g -= 1
    return g


def _paged_specs(G, n_pages, block):
    def spec(g):
        return pl.BlockSpec(block, lambda b, p, pt: (pt[b * n_pages + p * G + g], 0, 0))
    return [spec(g) for g in range(G)]


def _cumsum_body(pt_ref, *refs, G):
    lf_refs = refs[:G]
    new_ref, c_ref, cn_ref, carry_ref = refs[G:]
    p = pl.program_id(1)
    R = lf_refs[0].shape[0]

    @pl.when(p == 0)
    def _():
        carry_ref[...] = jnp.zeros_like(carry_ref)

    tri = _tri_ones(R)
    carry = carry_ref[...]
    for g in range(G):
        c = _tri_cumsum(tri, lf_refs[g][...]) + carry
        c_ref[g * R:(g + 1) * R, :] = c
        carry = c[-1:, :]
    carry_ref[...] = carry

    @pl.when(p == pl.num_programs(1) - 1)
    def _():
        nr = new_ref.shape[0]
        cn_ref[...] = _tri_cumsum(_tri_ones(nr), new_ref[...]) + carry


def _paged_cumsum(pool, page_table, new_rows, group=16):
    B, n_pages = page_table.shape
    R, H = pool.shape[1:]
    nr = new_rows.shape[1]
    G = _page_group(n_pages, group)
    grid_spec = pltpu.PrefetchScalarGridSpec(
        num_scalar_prefetch=1, grid=(B, n_pages // G),
        in_specs=_paged_specs(G, n_pages, (None, R, H)) + [
            pl.BlockSpec((None, nr, H), lambda b, p, pt: (b, 0, 0))],
        out_specs=[pl.BlockSpec((None, G * R, H), lambda b, p, pt: (b, p, 0)),
                   pl.BlockSpec((None, nr, H), lambda b, p, pt: (b, 0, 0))],
        scratch_shapes=[pltpu.VMEM((1, H), F32)])
    c, cn = pl.pallas_call(
        functools.partial(_cumsum_body, G=G), grid_spec=grid_spec,
        out_shape=[jax.ShapeDtypeStruct((B, n_pages * R, H), F32),
                   jax.ShapeDtypeStruct((B, nr, H), F32)],
        compiler_params=_cp(("parallel", "arbitrary")), name="paged_cumsum",
    )(page_table.reshape(-1).astype(I32), *([pool] * G), new_rows)
    return c, cn


def _softmax_step(s, vs, m_prev, l_prev, acc_prev):
    m_new = jnp.maximum(m_prev, jnp.max(s, axis=-1, keepdims=True))
    alpha = jnp.exp(m_prev - m_new)
    p = jnp.exp(s - m_new)
    l_new = alpha * l_prev + jnp.sum(p, axis=-1, keepdims=True)
    pb = p.astype(BF16)
    pv = None
    for g, v in enumerate(vs):
        d = jnp.dot(pb[:, g * LANES:(g + 1) * LANES], v, preferred_element_type=F32)
        pv = d if pv is None else pv + d
    return m_new, l_new, alpha * acc_prev + pv


def _causal_pairs(nq):
    qs, ks = [], []
    for qi in range(nq):
        for ki in range(qi + 1):
            qs.append(qi); ks.append(ki)
    return np.asarray(qs, np.int32), np.asarray(ks, np.int32)


def _init_softmax_state(m_ref, l_ref, acc_ref):
    m_ref[...] = jnp.full_like(m_ref, NEG)
    l_ref[...] = jnp.zeros_like(l_ref)
    acc_ref[...] = jnp.zeros_like(acc_ref)


def _tile_plan(t, diag):
    plan = []
    for r in range(t // QS):
        for c in range(t // KS):
            if diag and c * KS > r * QS + QS - 1:
                continue
            plan.append((r, c, diag and (c + 1) * KS - 1 > r * QS))
    return plan


def _attn_tiles_t(ks, qs, vts, ms, ls, accs, *, biases=None, masks=None, key_sub=None, qry_add=None):
    n = len(ks)
    ts = [_nt_dot(ks[i], qs[i]) for i in range(n)]
    if key_sub is not None:
        ts = [ts[i] - key_sub[i] for i in range(n)]
    if biases is not None:
        ts = [ts[i] + biases[i] for i in range(n)]
    if masks is not None:
        ts = [ts[i] if masks[i] is None else jnp.where(masks[i], ts[i], NEG) for i in range(n)]
    mx = [jnp.max(ts[i], axis=0, keepdims=True) for i in range(n)]
    if qry_add is not None:
        mx = [mx[i] + qry_add[i] for i in range(n)]
    m_new = [jnp.maximum(ms[i], mx[i]) for i in range(n)]
    shift = [-m_new[i] if qry_add is None else qry_add[i] - m_new[i] for i in range(n)]
    ps = [jnp.exp2(ts[i] + shift[i]) for i in range(n)]
    alphas = [jnp.exp2(ms[i] - m_new[i]) for i in range(n)]
    l_new = [alphas[i] * ls[i] + jnp.sum(ps[i], axis=0, keepdims=True) for i in range(n)]
    pv = [jnp.dot(vts[i], ps[i].astype(BF16), preferred_element_type=F32) for i in range(n)]
    acc_new = [alphas[i] * accs[i] + pv[i] for i in range(n)]
    return m_new, l_new, acc_new


def _tile_causal_t(r, c):
    key = lax.broadcasted_iota(I32, (KS, QS), 0)
    qry = lax.broadcasted_iota(I32, (KS, QS), 1)
    return key + (c * KS - r * QS) <= qry


def _fox_prompt_body(qi_ref, ki_ref, q_ref, k_ref, vt_ref, cq_ref, ck_ref, o_ref, m_ref, l_ref, acc_ref, *, t):
    p = pl.program_id(0)
    qi, ki = qi_ref[p], ki_ref[p]

    @pl.when(ki == 0)
    def _():
        _init_softmax_state(m_ref, l_ref, acc_ref)

    def run(diag):
        plan = _tile_plan(t, diag)
        qcs = [slice(r * QS, (r + 1) * QS) for r in range(t // QS)]
        for h0 in range(0, FOX_HEADS, FOX_TOGETHER):
            heads = list(range(h0, h0 + FOX_TOGETHER))
            hds = {h: slice(h * HEAD_DIM, (h + 1) * HEAD_DIM) for h in heads}
            qs = {(h, r): q_ref[qc, hds[h]] for h in heads for r, qc in enumerate(qcs)}
            cq2 = {(h, r): cq_ref[h:h + 1, qc] * LOG2E for h in heads for r, qc in enumerate(qcs)}
            ms = {(h, r): m_ref[h:h + 1, qc] for h in heads for r, qc in enumerate(qcs)}
            ls = {(h, r): l_ref[h:h + 1, qc] for h in heads for r, qc in enumerate(qcs)}
            accs = {(h, r): acc_ref[hds[h], qc] for h in heads for r, qc in enumerate(qcs)}
            for c in range(t // KS):
                kc = slice(c * KS, (c + 1) * KS)
                act = [(r, msk) for r, cc, msk in plan if cc == c]
                ck2 = {h: jnp.broadcast_to(ck_ref[kc, h:h + 1] * LOG2E, (KS, QS)) for h in heads}
                k = {h: k_ref[kc, hds[h]] for h in heads}
                vt = {h: vt_ref[hds[h], kc] for h in heads}
                for g0 in range(0, len(act), WIDTH):
                    tiles = [(h, r, msk) for h in heads for r, msk in act[g0:g0 + WIDTH]]
                    mo, lo, ao = _attn_tiles_t(
                        [k[h] for h, _, _ in tiles], [qs[h, r] for h, r, _ in tiles], [vt[h] for h, _, _ in tiles],
                        [ms[h, r] for h, r, _ in tiles], [ls[h, r] for h, r, _ in tiles],
                        [accs[h, r] for h, r, _ in tiles],
                        key_sub=[ck2[h] for h, _, _ in tiles], qry_add=[cq2[h, r] for h, r, _ in tiles],
                        masks=[_tile_causal_t(r, c) if msk else None for _, r, msk in tiles])
                    for i, (h, r, _) in enumerate(tiles):
                        ms[h, r], ls[h, r], accs[h, r] = mo[i], lo[i], ao[i]
            for h in heads:
                for r, qc in enumerate(qcs):
                    if diag:
                        o_ref[qc, hds[h]] = (accs[h, r] / ls[h, r]).T.astype(BF16)
                    else:
                        m_ref[h:h + 1, qc] = ms[h, r]
                        l_ref[h:h + 1, qc] = ls[h, r]
                        acc_ref[hds[h], qc] = accs[h, r]

    @pl.when(ki == qi)
    def _():
        run(True)

    @pl.when(ki != qi)
    def _():
        run(False)


def _fox_prompt(q, k, vt, c, t=512):
    T = q.shape[0]
    t = max(_tile(T, t), KS)
    qs, ks = _causal_pairs(T // t)
    grid_spec = pltpu.PrefetchScalarGridSpec(
        num_scalar_prefetch=2, grid=(len(qs),),
        in_specs=[pl.BlockSpec((t, FOX_W), lambda p, qi, ki: (qi[p], 0)),
                  pl.BlockSpec((t, FOX_W), lambda p, qi, ki: (ki[p], 0)),
                  pl.BlockSpec((FOX_W, t), lambda p, qi, ki: (0, ki[p])),
                  pl.BlockSpec((FOX_HEADS, t), lambda p, qi, ki: (0, qi[p])),
                  pl.BlockSpec((t, FOX_HEADS), lambda p, qi, ki: (ki[p], 0))],
        out_specs=pl.BlockSpec((t, FOX_W), lambda p, qi, ki: (qi[p], 0)),
        scratch_shapes=[pltpu.VMEM((FOX_HEADS, t), F32), pltpu.VMEM((FOX_HEADS, t), F32),
                        pltpu.VMEM((FOX_W, t), F32)])
    return pl.pallas_call(
        functools.partial(_fox_prompt_body, t=t), grid_spec=grid_spec,
        out_shape=jax.ShapeDtypeStruct((T, FOX_W), BF16),
        compiler_params=_cp(("arbitrary",)), name="fox_prompt",
    )(jnp.asarray(qs), jnp.asarray(ks), q, k, vt, c.T, c)


def _float_key(x):
    b = pltpu.bitcast(x, I32)
    return b ^ ((b >> 31) & jnp.int32(0x7FFFFFFF))


def _kth_key(keys_ref, nchunks, k, row0, rows, unroll=1):
    def count_ge(cand):
        def body(c, acc):
            for u in range(unroll):
                kk = keys_ref[c * unroll + u, row0:row0 + rows, :]
                acc = acc + jnp.where(kk >= cand, 1.0, 0.0)
            return acc
        acc = lax.fori_loop(0, nchunks // unroll, body, jnp.zeros((rows, LANES), F32))
        return jnp.sum(acc, axis=1, keepdims=True)

    def bit_body(i, t):
        cand = t + lax.shift_left(jnp.int32(1), 31 - i)
        return jnp.where(count_ge(cand) >= k, cand, t)

    return lax.fori_loop(0, 32, bit_body, jnp.full((rows, 1), INT_MIN, I32))


def _idx_prompt_body(q_ref, w_ref, ka_ref, kb_ref, keys_ref, thr_ref, *, tq, kc, n_sel, row_group):
    qi = pl.program_id(0)
    nck = keys_ref.shape[0]
    per = kc // LANES
    n_super = ((qi + 1) * tq + kc - 1) // kc

    def super_body(sc, carry):
        k0 = pl.multiple_of(sc * kc, kc)
        ka = ka_ref[pl.ds(k0, kc), :]
        kb = kb_ref[pl.ds(k0, kc), :]
        acc = jnp.zeros((tq, kc), F32)
        for j in range(IDX_QW // LANES):
            q2 = q_ref[:, j * LANES:(j + 1) * LANES]
            acc = acc + w_ref[:, 2 * j:2 * j + 1] * jnp.maximum(_nt_dot(q2, ka), 0.0)
            acc = acc + w_ref[:, 2 * j + 1:2 * j + 2] * jnp.maximum(_nt_dot(q2, kb), 0.0)
        qpos = qi * tq + lax.broadcasted_iota(I32, (tq, kc), 0)
        kpos = sc * kc + lax.broadcasted_iota(I32, (tq, kc), 1)
        key = jnp.where(kpos <= qpos, _float_key(acc), INT_MIN)
        for c in range(per):
            keys_ref[sc * per + c] = key[:, c * LANES:(c + 1) * LANES]
        return carry

    lax.fori_loop(0, n_super, super_body, 0)

    def fill_body(c, carry):
        keys_ref[c] = jnp.full((tq, LANES), INT_MIN, I32)
        return carry

    lax.fori_loop(n_super * per, nck, fill_body, 0)
    for rg in range(tq // row_group):
        thr_ref[rg * row_group:(rg + 1) * row_group, :] = _kth_key(
            keys_ref, n_super * per, n_sel, rg * row_group, row_group, unroll=per)


def _idx_prompt(qi_bf, wi, kia, kib, n_sel, tq=256, kc=512):
    T = qi_bf.shape[0]
    tq = _tile(T, tq)
    kc = _tile(T, kc)
    nck = T // LANES
    row_group = min(tq, 128)
    keys, thr = pl.pallas_call(
        functools.partial(_idx_prompt_body, tq=tq, kc=kc, n_sel=n_sel, row_group=row_group),
        grid=(T // tq,),
        in_specs=[pl.BlockSpec((tq, IDX_QW), lambda i: (i, 0)),
                  pl.BlockSpec((tq, IDX_HEADS), lambda i: (i, 0)),
                  pl.BlockSpec((T, LANES), lambda i: (0, 0)),
                  pl.BlockSpec((T, LANES), lambda i: (0, 0))],
        out_specs=[pl.BlockSpec((None, nck, tq, LANES), lambda i: (i, 0, 0, 0)),
                   pl.BlockSpec((tq, 1), lambda i: (i, 0))],
        out_shape=[jax.ShapeDtypeStruct((T // tq, nck, tq, LANES), I32),
                   jax.ShapeDtypeStruct((T, 1), I32)],
        compiler_params=_cp(("parallel",)), name="idx_prompt",
    )(qi_bf, wi, kia, kib)
    return keys, thr, tq


def _dsa_prompt_body(qi_ref, ki_ref, q_ref, k_ref, vt_ref, keys_ref, thr_ref, o_ref, m_ref, l_ref, acc_ref, *, t):
    p = pl.program_id(0)
    qi, ki = qi_ref[p], ki_ref[p]

    @pl.when(ki == 0)
    def _():
        _init_softmax_state(m_ref, l_ref, acc_ref)

    def run(diag):
        for r, c, msk in _tile_plan(t, diag):
            qc = slice(r * QS, (r + 1) * QS)
            kc = slice(c * KS, (c + 1) * KS)
            keys = jnp.concatenate([keys_ref[c * (KS // LANES) + j, qc, :] for j in range(KS // LANES)], axis=1)
            bias = jnp.where(keys >= thr_ref[qc, :], 0.0, NEG).T
            if msk:
                bias = jnp.where(_tile_causal_t(r, c), bias, NEG)
            for g0 in range(0, DSA_KV_HEADS, GROUPS):
                hs = [g * DSA_GROUP + i for g in range(g0, g0 + GROUPS) for i in range(DSA_GROUP)]
                kvs = [slice((h // DSA_GROUP) * HEAD_DIM, (h // DSA_GROUP + 1) * HEAD_DIM) for h in hs]
                hds = [slice(h * HEAD_DIM, (h + 1) * HEAD_DIM) for h in hs]
                n = len(hs)
                mo, lo, ao = _attn_tiles_t(
                    [k_ref[kc, kv] for kv in kvs], [q_ref[qc, hd] for hd in hds], [vt_ref[kv, kc] for kv in kvs],
                    [m_ref[h:h + 1, qc] for h in hs], [l_ref[h:h + 1, qc] for h in hs],
                    [acc_ref[hd, qc] for hd in hds], biases=[bias] * n)
                for i, h in enumerate(hs):
                    m_ref[h:h + 1, qc] = mo[i]
                    l_ref[h:h + 1, qc] = lo[i]
                    acc_ref[hds[i], qc] = ao[i]

    @pl.when(ki == qi)
    def _():
        run(True)
        for h in range(DSA_HEADS):
            hd = slice(h * HEAD_DIM, (h + 1) * HEAD_DIM)
            o_ref[:, hd] = (acc_ref[hd, :] / l_ref[h:h + 1, :]).T.astype(BF16)

    @pl.when(ki != qi)
    def _():
        run(False)


def _dsa_prompt(q, k, vt, keys, thr, t):
    T = q.shape[0]
    qs, ks = _causal_pairs(T // t)
    per = t // LANES
    grid_spec = pltpu.PrefetchScalarGridSpec(
        num_scalar_prefetch=2, grid=(len(qs),),
        in_specs=[pl.BlockSpec((t, DSA_QW), lambda p, qi, ki: (qi[p], 0)),
                  pl.BlockSpec((t, DSA_KW), lambda p, qi, ki: (ki[p], 0)),
                  pl.BlockSpec((DSA_KW, t), lambda p, qi, ki: (0, ki[p])),
                  pl.BlockSpec((None, per, t, LANES), lambda p, qi, ki: (qi[p], ki[p], 0, 0)),
                  pl.BlockSpec((t, 1), lambda p, qi, ki: (qi[p], 0))],
        out_specs=pl.BlockSpec((t, DSA_QW), lambda p, qi, ki: (qi[p], 0)),
        scratch_shapes=[pltpu.VMEM((DSA_HEADS, t), F32), pltpu.VMEM((DSA_HEADS, t), F32),
                        pltpu.VMEM((DSA_QW, t), F32)])
    return pl.pallas_call(
        functools.partial(_dsa_prompt_body, t=t), grid_spec=grid_spec,
        out_shape=jax.ShapeDtypeStruct((T, DSA_QW), BF16),
        compiler_params=_cp(("arbitrary",)), name="dsa_prompt",
    )(jnp.asarray(qs), jnp.asarray(ks), q, k, vt, keys, thr)


TPAD = 8


def _head_rows(ref, h, n_heads):
    return ref[pl.ds(h, ref.shape[0] // n_heads, stride=n_heads), :].astype(BF16)


def _decode_step(qs, kget, vget, n_blocks, bias, state, rows_per_q):
    m_ref, l_ref, acc_ref = state
    n_q = len(qs)
    s = jnp.concatenate(
        [jnp.concatenate([_nt_dot(qs[j], kget(b, j)) for j in range(n_q)], axis=0) for b in range(n_blocks)],
        axis=1) * ATT_SCALE + bias
    m_prev = m_ref[...]
    m_new = jnp.maximum(m_prev, jnp.max(s, axis=-1, keepdims=True))
    alpha = jnp.exp(m_prev - m_new)
    p = jnp.exp(s - m_new)
    l_ref[...] = alpha * l_ref[...] + jnp.sum(p, axis=-1, keepdims=True)
    m_ref[...] = m_new
    pv = []
    for j in range(n_q):
        rows = slice(j * rows_per_q, (j + 1) * rows_per_q)
        d = None
        for b in range(n_blocks):
            t = jnp.dot(p[rows, b * LANES:(b + 1) * LANES].astype(BF16), vget(b, j), preferred_element_type=F32)
            d = t if d is None else d + t
        pv.append(d)
    acc_ref[...] = alpha * acc_ref[...] + jnp.concatenate(pv, axis=0)


def _fox_sample_body(pt_ref, q_ref, cq_ref, ck_ref, *refs, G):
    k_refs, v_refs = refs[:G], refs[G:2 * G]
    kn_ref, vn_ref, ckn_ref, o_ref, m_ref, l_ref, acc_ref = refs[2 * G:]
    p = pl.program_id(1)
    state = (m_ref, l_ref, acc_ref)

    @pl.when(p == 0)
    def _():
        _init_softmax_state(*state)

    qs = [q_ref[h].astype(BF16) for h in range(FOX_HEADS)]
    cq = cq_ref[...]

    def forget_bias(ck):
        return cq - jnp.concatenate(
            [jnp.broadcast_to(ck[h:h + 1, :], (TPAD, ck.shape[1])) for h in range(FOX_HEADS)], axis=0)

    _decode_step(qs, lambda b, h: _head_rows(k_refs[b], h, FOX_HEADS), lambda b, h: _head_rows(v_refs[b], h, FOX_HEADS),
                 G, forget_bias(ck_ref[...]), state, TPAD)

    @pl.when(p == pl.num_programs(1) - 1)
    def _():
        rows = FOX_HEADS * TPAD
        tok = lax.broadcasted_iota(I32, (rows, LANES), 0) % TPAD
        col = lax.broadcasted_iota(I32, (rows, LANES), 1)
        bias = jnp.where(col <= tok, forget_bias(ckn_ref[...]), NEG)
        _decode_step(qs, lambda b, h: kn_ref[h], lambda b, h: vn_ref[h], 1, bias, state, TPAD)
        o_ref[...] = acc_ref[...] / l_ref[...]


def _fox_sample(q, cq, ckT, cache_k, cache_v, page_table, knew, vnew, cknT, group=8):
    B, n_pages = page_table.shape
    R = cache_k.shape[1]
    page = R // FOX_HEADS
    rows = FOX_HEADS * TPAD
    G = _page_group(n_pages, group)
    per_b3 = lambda b, p, pt: (b, 0, 0)
    per_b4 = lambda b, p, pt: (b, 0, 0, 0)
    grid_spec = pltpu.PrefetchScalarGridSpec(
        num_scalar_prefetch=1, grid=(B, n_pages // G),
        in_specs=[pl.BlockSpec((None, FOX_HEADS, TPAD, HEAD_DIM), per_b4),
                  pl.BlockSpec((None, rows, 1), per_b3),
                  pl.BlockSpec((None, FOX_HEADS, G * page), lambda b, p, pt: (b, 0, p))]
        + _paged_specs(G, n_pages, (None, R, HEAD_DIM)) + _paged_specs(G, n_pages, (None, R, HEAD_DIM))
        + [pl.BlockSpec((None, FOX_HEADS, LANES, HEAD_DIM), per_b4),
           pl.BlockSpec((None, FOX_HEADS, LANES, HEAD_DIM), per_b4),
           pl.BlockSpec((None, FOX_HEADS, LANES), per_b3)],
        out_specs=pl.BlockSpec((None, rows, HEAD_DIM), per_b3),
        scratch_shapes=[pltpu.VMEM((rows, 1), F32), pltpu.VMEM((rows, 1), F32),
                        pltpu.VMEM((rows, HEAD_DIM), F32)])
    return pl.pallas_call(
        functools.partial(_fox_sample_body, G=G), grid_spec=grid_spec,
        out_shape=jax.ShapeDtypeStruct((B, rows, HEAD_DIM), F32),
        compiler_params=_cp(("parallel", "arbitrary")), name="fox_sample",
    )(page_table.reshape(-1).astype(I32), q, cq, ckT, *([cache_k] * G), *([cache_v] * G), knew, vnew, cknT)


def _idx_sample_body(pt_ref, q_ref, w_ref, *refs, ts, n_sel, G):
    k_refs = refs[:G]
    kn_ref, keys_ref, thr_ref = refs[G:]
    p = pl.program_id(1)
    n_pages = keys_ref.shape[0] - 1
    pad = jnp.full((TPAD - ts, LANES), INT_MIN, I32)

    def score(k):
        r = jnp.maximum(_nt_dot(q_ref[...], k), 0.0) * w_ref[...]
        return jnp.sum(r.reshape(ts, IDX_HEADS, LANES), axis=1)

    for g in range(G):
        keys_ref[p * G + g] = jnp.concatenate([_float_key(score(k_refs[g][...].astype(BF16))), pad], axis=0)

    @pl.when(p == pl.num_programs(1) - 1)
    def _():
        tok = lax.broadcasted_iota(I32, (ts, LANES), 0)
        col = lax.broadcasted_iota(I32, (ts, LANES), 1)
        kn = jnp.where(col <= tok, _float_key(score(kn_ref[...])), INT_MIN)
        keys_ref[n_pages] = jnp.concatenate([kn, pad], axis=0)
        thr_ref[...] = _kth_key(keys_ref, n_pages + 1, n_sel, 0, 8)


def _idx_sample(q, w, cache_k, page_table, knew, ts, n_sel, group=64):
    B, n_pages = page_table.shape
    R = cache_k.shape[1]
    rows = ts * IDX_HEADS
    G = _page_group(n_pages, group)
    per_b = lambda b, p, pt: (b, 0, 0)
    grid_spec = pltpu.PrefetchScalarGridSpec(
        num_scalar_prefetch=1, grid=(B, n_pages // G),
        in_specs=[pl.BlockSpec((None, rows, IDX_DIM), per_b),
                  pl.BlockSpec((None, rows, 1), per_b)]
        + _paged_specs(G, n_pages, (None, R, IDX_DIM))
        + [pl.BlockSpec((None, LANES, IDX_DIM), per_b)],
        out_specs=[pl.BlockSpec((None, n_pages + 1, 8, LANES), lambda b, p, pt: (b, 0, 0, 0)),
                   pl.BlockSpec((None, 8, 1), per_b)])
    return pl.pallas_call(
        functools.partial(_idx_sample_body, ts=ts, n_sel=n_sel, G=G), grid_spec=grid_spec,
        out_shape=[jax.ShapeDtypeStruct((B, n_pages + 1, 8, LANES), I32),
                   jax.ShapeDtypeStruct((B, 8, 1), I32)],
        compiler_params=_cp(("parallel", "arbitrary")), name="idx_sample",
    )(page_table.reshape(-1).astype(I32), q, w, *([cache_k] * G), knew)


def _dsa_sample_body(pt_ref, q_ref, keys_ref, keysn_ref, thr_ref, *refs, G):
    k_refs, v_refs = refs[:G], refs[G:2 * G]
    kn_ref, vn_ref, o_ref, m_ref, l_ref, acc_ref = refs[2 * G:]
    p = pl.program_id(1)
    state = (m_ref, l_ref, acc_ref)

    @pl.when(p == 0)
    def _():
        _init_softmax_state(*state)

    qs = [q_ref[g].astype(BF16) for g in range(DSA_KV_HEADS)]
    thr = thr_ref[...]

    def select_bias(keys):
        sel = jnp.concatenate([jnp.where(kk >= thr, 0.0, NEG) for kk in keys], axis=1)
        return jnp.concatenate([sel] * DSA_HEADS, axis=0)

    _decode_step(qs, lambda b, g: _head_rows(k_refs[b], g, DSA_KV_HEADS),
                 lambda b, g: _head_rows(v_refs[b], g, DSA_KV_HEADS),
                 G, select_bias([keys_ref[b] for b in range(G)]), state, DSA_GROUP * TPAD)

    @pl.when(p == pl.num_programs(1) - 1)
    def _():
        _decode_step(qs, lambda b, g: kn_ref[g], lambda b, g: vn_ref[g], 1, select_bias([keysn_ref[...]]),
                     state, DSA_GROUP * TPAD)
        o_ref[...] = acc_ref[...] / l_ref[...]


def _dsa_sample(q, keys, thr, cache_k, cache_v, page_table, knew, vnew, group=16):
    B, n_pages = page_table.shape
    R = cache_k.shape[1]
    rows = DSA_HEADS * TPAD
    G = _page_group(n_pages, group)
    per_b3 = lambda b, p, pt: (b, 0, 0)
    per_b4 = lambda b, p, pt: (b, 0, 0, 0)
    grid_spec = pltpu.PrefetchScalarGridSpec(
        num_scalar_prefetch=1, grid=(B, n_pages // G),
        in_specs=[pl.BlockSpec((None, DSA_KV_HEADS, DSA_GROUP * TPAD, HEAD_DIM), per_b4),
                  pl.BlockSpec((None, G, TPAD, LANES), lambda b, p, pt: (b, p, 0, 0)),
                  pl.BlockSpec((None, None, TPAD, LANES), lambda b, p, pt: (b, n_pages, 0, 0)),
                  pl.BlockSpec((None, TPAD, 1), per_b3)]
        + _paged_specs(G, n_pages, (None, R, HEAD_DIM)) + _paged_specs(G, n_pages, (None, R, HEAD_DIM))
        + [pl.BlockSpec((None, DSA_KV_HEADS, LANES, HEAD_DIM), per_b4),
           pl.BlockSpec((None, DSA_KV_HEADS, LANES, HEAD_DIM), per_b4)],
        out_specs=pl.BlockSpec((None, rows, HEAD_DIM), per_b3),
        scratch_shapes=[pltpu.VMEM((rows, 1), F32), pltpu.VMEM((rows, 1), F32),
                        pltpu.VMEM((rows, HEAD_DIM), F32)])
    return pl.pallas_call(
        functools.partial(_dsa_sample_body, G=G), grid_spec=grid_spec,
        out_shape=jax.ShapeDtypeStruct((B, rows, HEAD_DIM), F32),
        compiler_params=_cp(("parallel", "arbitrary")), name="dsa_sample",
    )(page_table.reshape(-1).astype(I32), q, keys, keys, thr, *([cache_k] * G), *([cache_v] * G), knew, vnew)


def _top_rows(x, n):
    out = []
    for _ in range(n):
        m = jnp.max(x, axis=0, keepdims=True)
        out.append(m)
        x = jnp.where(x == m, -jnp.inf, x)
    return out


def _peer_select_body(q_ref, keys_ref, s0_ref, s1_ref, st_ref, *, n_heads):
    nk = keys_ref.shape[2]
    thr, off = [], []
    for h in range(n_heads):
        tops = []
        for c, s_ref in enumerate((s0_ref, s1_ref)):
            col = (2 * h + c) * nk
            st = _nt_dot(q_ref[:, col:col + nk].astype(BF16), keys_ref[h, c]).T
            s_ref[h * nk:(h + 1) * nk, :] = st
            tops.append(jnp.concatenate(_top_rows(st, PEER_TOPK), axis=0))
        cand = jnp.concatenate([tops[0][r:r + 1, :] + tops[1] for r in range(PEER_TOPK)], axis=0)
        best = _top_rows(cand, PEER_TOPK)
        z = jnp.ones_like(best[0])
        for r in range(1, PEER_TOPK):
            z = z + jnp.exp(best[r] - best[0])
        thr.append(best[-1])
        off.append(-(best[0] + jnp.log(z)))
    st_ref[...] = jnp.concatenate(thr + off, axis=0)


def _peer_select(q, keys_bf, tm=128):
    M = q.shape[0]
    n_heads, _, nk, _ = keys_bf.shape
    tm = _tile(M, tm)
    col = lambda i: (0, i)
    return pl.pallas_call(
        functools.partial(_peer_select_body, n_heads=n_heads),
        grid=(M // tm,),
        in_specs=[pl.BlockSpec((tm, q.shape[1]), lambda i: (i, 0)),
                  pl.BlockSpec(keys_bf.shape, lambda i: (0, 0, 0, 0))],
        out_specs=[pl.BlockSpec((n_heads * nk, tm), col), pl.BlockSpec((n_heads * nk, tm), col),
                   pl.BlockSpec((2 * n_heads, tm), col)],
        out_shape=[jax.ShapeDtypeStruct((n_heads * nk, M), F32), jax.ShapeDtypeStruct((n_heads * nk, M), F32),
                   jax.ShapeDtypeStruct((2 * n_heads, M), F32)],
        compiler_params=_cp(("parallel",)), name="peer_select",
    )(q, keys_bf)


def _peer_expert_body(x_ref, u_ref, v_ref, s0_ref, s1_ref, st_ref, o_ref, *, n_heads, nk, ni):
    e = pl.program_id(1)

    @pl.when(e == 0)
    def _():
        o_ref[...] = jnp.zeros_like(o_ref)

    a = _nt_dot(u_ref[...], x_ref[...])
    act = 0.5 * a * (1.0 + lax.erf(a * (2.0 ** -0.5)))
    gates = []
    for il in range(ni):
        w = None
        for h in range(n_heads):
            tot = s0_ref[il * n_heads + h:il * n_heads + h + 1, :] + s1_ref[h * nk:(h + 1) * nk, :]
            val = jnp.where(tot >= st_ref[h:h + 1, :], jnp.exp(tot + st_ref[n_heads + h:n_heads + h + 1, :]), 0.0)
            w = val if w is None else w + val
        gates.append(w)
    ga = (jnp.concatenate(gates, axis=0) * act).astype(BF16)
    o_ref[...] += lax.dot_general(ga, v_ref[...], (((0,), (0,)), ((), ())), preferred_element_type=F32)


def _peer_expert(hn_bf, u_bf, v_bf, s0t, s1t, st, n_heads, nk, tm=512, te=512):
    M, D = hn_bf.shape
    E = u_bf.shape[0]
    tm = _tile(M, tm)
    te = _tile(E, te)
    ni = te // nk
    s0g = jnp.transpose(s0t.reshape(n_heads, E // te, ni, M), (1, 2, 0, 3)).reshape(E // te, ni * n_heads, M)
    return pl.pallas_call(
        functools.partial(_peer_expert_body, n_heads=n_heads, nk=nk, ni=ni),
        grid=(M // tm, E // te),
        in_specs=[pl.BlockSpec((tm, D), lambda i, e: (i, 0)),
                  pl.BlockSpec((te, D), lambda i, e: (e, 0)),
                  pl.BlockSpec((te, D), lambda i, e: (e, 0)),
                  pl.BlockSpec((None, ni * n_heads, tm), lambda i, e: (e, 0, i)),
                  pl.BlockSpec((n_heads * nk, tm), lambda i, e: (0, i)),
                  pl.BlockSpec((2 * n_heads, tm), lambda i, e: (0, i))],
        out_specs=pl.BlockSpec((tm, D), lambda i, e: (i, 0)),
        out_shape=jax.ShapeDtypeStruct((M, D), F32),
        compiler_params=_cp(("parallel", "arbitrary")), name="peer_expert",
    )(hn_bf, u_bf, v_bf, s0g, s1t, st)


def _prep_w_in(w_in):
    o = np.cumsum([0, FOX_W, FOX_W, FOX_W, FOX_HEADS, DSA_QW, DSA_KW, DSA_KW, IDX_QW, IDX_DIM, IDX_HEADS])
    seg = lambda i: w_in[:, o[i]:o[i + 1]]
    cols = [seg(i).astype(BF16) for i in (0, 1, 2, 4, 7, 5, 6, 8, 9, 3)]
    used = int(o[-1])
    total = -(-used // 768) * 768
    cols.append(jnp.zeros((w_in.shape[0], total - used), BF16))
    return jnp.concatenate(cols, axis=1)


def _channel(x, o_fox, o_dsa, p_l, w_o_bf, g_norm2, w_pq_bf, keys_bf, u_bf, v_bf, g_norm3, w_gate_bf, w_proj_bf,
             g_ple):
    n_heads, _, nk, _ = keys_bf.shape
    h1 = _mm2_res(o_fox, o_dsa, w_o_bf, x)
    q, hn_bf = _mm(h1, w_pq_bf, g=g_norm2, emit_xn=True)
    s0t, s1t, st = _peer_select(q, keys_bf)
    moe = _peer_expert(hn_bf, u_bf, v_bf, s0t, s1t, st, n_heads, nk)
    pn = _rownorm_mm(p_l, w_proj_bf, g_ple)
    return _mm(h1, w_gate_bf, g=g_norm3, x2=moe, res=h1, res2=moe, aux=pn, epi="gate", tm=512, tn=256)


def _head_major(x, B, ts, n_heads, n_rows):
    x4 = jnp.transpose(x.reshape(B, ts, n_heads, HEAD_DIM), (0, 2, 1, 3))
    return jnp.pad(x4, ((0, 0), (0, 0), (0, n_rows - ts), (0, 0)))


def _token_major(o, B, ts, n_heads):
    o4 = o.reshape(B, n_heads, TPAD, HEAD_DIM)[:, :, :ts]
    return jnp.transpose(o4, (0, 2, 1, 3)).reshape(B * ts, n_heads * HEAD_DIM)


def _pad_rows(x, n):
    return jnp.pad(x, ((0, 0), (0, n - x.shape[1]), (0, 0)))


def kernel(x_prompt, x_sample, cache_fox_k, cache_fox_v, cache_fox_logf, cache_dsa_k, cache_dsa_v, cache_idx_k, page_table, p_prompt, p_sample, g_norm1, w_in, b_f, g_q_fox, g_k_fox, g_q_dsa, g_k_dsa, w_o, g_norm2, w_peer_q, peer_keys, peer_u, peer_v, g_norm3, w_ple_gate, w_ple_proj, g_ple):
    Bp, Tp, D = x_prompt.shape
    Bs, Ts = x_sample.shape[:2]
    depth = w_in.shape[0]
    n_pages = page_table.shape[1]
    page = cache_fox_k.shape[2]
    past = n_pages * page
    n_phys = cache_fox_k.shape[1]
    assert Bp == 1 and Ts <= TPAD and page == LANES

    def pool(cache):
        return cache.reshape((depth * n_phys, -1) + cache.shape[-1:])

    fox_k_pool, fox_v_pool, dsa_k_pool, dsa_v_pool = pool(cache_fox_k), pool(cache_fox_v), pool(cache_dsa_k), pool(cache_dsa_v)
    fox_lf_pool = cache_fox_logf.reshape(depth * n_phys, page, FOX_HEADS)
    idx_k_pool = cache_idx_k.reshape(depth * n_phys, page, IDX_DIM)

    h_p = x_prompt.reshape(Tp, D)
    h_s = x_sample.reshape(Bs * Ts, D)
    pos_p = jnp.arange(Tp, dtype=I32)
    pos_s = jnp.tile(past + jnp.arange(Ts, dtype=I32), Bs)
    outs = [[] for _ in range(12)]
    for l in range(depth):
        w_in_bf = _prep_w_in(w_in[l])
        chan_w = (w_o[l].astype(BF16), g_norm2[l], w_peer_q[l].astype(BF16), peer_keys[l].astype(BF16),
                  peer_u[l].astype(BF16), peer_v[l].astype(BF16), g_norm3[l], w_ple_gate[l].astype(BF16),
                  w_ple_proj[l].astype(BF16), g_ple[l])
        norm_w = (b_f[l], g_q_fox[l], g_k_fox[l], g_q_dsa[l], g_k_dsa[l])
        pt_l = page_table + l * n_phys

        z = _mm(h_p, w_in_bf, g=g_norm1[l], tn=768)
        pr = _post(z, pos_p, *norm_w, ATT_SCALE * LOG2E)
        ident = jnp.arange(Tp // page, dtype=I32).reshape(1, -1)
        c, _ = _paged_cumsum(pr["lf"].reshape(Tp // page, page, FOX_HEADS), ident,
                             jnp.zeros((1, 8, FOX_HEADS), F32))
        o_fox = _fox_prompt(pr["qf"], pr["kfb"], pr["vft"], c[0])
        keys, thr, tq = _idx_prompt(pr["qi"], pr["wi"], pr["kia"], pr["kib"], min(TOPK_MAX, Tp // 4))
        o_dsa = _dsa_prompt(pr["qd"], pr["kdb"], pr["vdt"], keys, thr, tq)
        h_p = _channel(h_p, o_fox, o_dsa, p_prompt[l].reshape(Tp, -1), *chan_w)
        for i, (name, shape) in enumerate((("kf", (Bp, Tp, FOX_HEADS, HEAD_DIM)), ("vf", (Bp, Tp, FOX_HEADS, HEAD_DIM)),
                                           ("lf", (Bp, Tp, FOX_HEADS)), ("kd", (Bp, Tp, DSA_KV_HEADS, HEAD_DIM)),
                                           ("vd", (Bp, Tp, DSA_KV_HEADS, HEAD_DIM)), ("ki", (Bp, Tp, IDX_DIM)))):
            outs[i].append(pr[name].reshape(shape))

        zs = _mm(h_s, w_in_bf, g=g_norm1[l], tn=768)
        sr = _post(zs, pos_s, *norm_w, 1.0)
        lf_new = _pad_rows(sr["lf"].reshape(Bs, Ts, FOX_HEADS), TPAD)
        c_past, c_new = _paged_cumsum(fox_lf_pool, pt_l, lf_new)
        cq = jnp.transpose(c_new, (0, 2, 1)).reshape(Bs, FOX_HEADS * TPAD, 1)
        ckT = jnp.transpose(c_past, (0, 2, 1))
        cknT = jnp.transpose(_pad_rows(c_new, LANES), (0, 2, 1))
        o = _fox_sample(_head_major(sr["qf"], Bs, Ts, FOX_HEADS, TPAD).astype(F32), cq, ckT, fox_k_pool, fox_v_pool, pt_l,
                        _head_major(sr["kfb"], Bs, Ts, FOX_HEADS, LANES), _head_major(sr["vfb"], Bs, Ts, FOX_HEADS, LANES),
                        cknT)
        o_fox_s = _token_major(o, Bs, Ts, FOX_HEADS)
        n_sel = min(TOPK_MAX, (past + Ts) // 4)
        keys_s, thr_s = _idx_sample(sr["qi"].reshape(Bs, Ts * IDX_HEADS, IDX_DIM),
                                    sr["wi"].reshape(Bs, Ts * IDX_HEADS, 1), idx_k_pool, pt_l,
                                    _pad_rows(sr["kia"][:, :IDX_DIM].reshape(Bs, Ts, IDX_DIM), LANES), Ts, n_sel)
        qd = _head_major(sr["qd"], Bs, Ts, DSA_HEADS, TPAD).astype(F32)
        od = _dsa_sample(qd.reshape(Bs, DSA_KV_HEADS, DSA_GROUP * TPAD, HEAD_DIM), keys_s, thr_s,
                         dsa_k_pool, dsa_v_pool, pt_l, _head_major(sr["kdb"], Bs, Ts, DSA_KV_HEADS, LANES),
                         _head_major(sr["vdb"], Bs, Ts, DSA_KV_HEADS, LANES))
        o_dsa_s = _token_major(od, Bs, Ts, DSA_HEADS)
        h_s = _channel(h_s, o_fox_s.astype(BF16), o_dsa_s.astype(BF16), p_sample[l].reshape(Bs * Ts, -1), *chan_w)
        for i, (name, shape) in enumerate((("kf", (Bs, Ts, FOX_HEADS, HEAD_DIM)), ("vf", (Bs, Ts, FOX_HEADS, HEAD_DIM)),
                                           ("lf", (Bs, Ts, FOX_HEADS)), ("kd", (Bs, Ts, DSA_KV_HEADS, HEAD_DIM)),
                                           ("vd", (Bs, Ts, DSA_KV_HEADS, HEAD_DIM)), ("ki", (Bs, Ts, IDX_DIM)))):
            outs[6 + i].append(sr[name].reshape(shape))
    return (h_p.reshape(Bp, Tp, D), h_s.reshape(Bs, Ts, D)) + tuple(jnp.stack(o) for o in outs)
```
